```python
import jax, jax.numpy as jnp
from jax import lax
import numpy as np

D_MODEL = 1024
BATCH = 2
SEQ = 16384
DEPTH = 1
DEC_BATCH = 8
DEC_SEQ = 2048
PAST_LEN = 128

N_FOURIER_GROUPS = 4
FOURIER_GROUP_DIM = D_MODEL // 8
FOURIER_WIDTH = N_FOURIER_GROUPS * FOURIER_GROUP_DIM
POOL_WINDOWS = (2, 4, 8, 16)
N_POOL_GROUPS = len(POOL_WINDOWS)
POOL_GROUP_DIM = D_MODEL // 8
POOL_WIDTH = N_POOL_GROUPS * POOL_GROUP_DIM
N_BRANCHES = 2
IN_WIDTH = FOURIER_WIDTH + POOL_WIDTH + N_BRANCHES * D_MODEL
D_FF = ((8 * D_MODEL // 3 + 255) // 256) * 256
N_ADA = 6
EPS = 1e-6

kernel_name = "fourier_pool_hybrid_encoder"


def _rmsnorm(x, g):
    xf = x.astype(jnp.float32)
    y = xf * lax.rsqrt(jnp.mean(xf * xf, axis=-1, keepdims=True) + EPS)
    return (y * g.astype(jnp.float32)).astype(x.dtype)


def _modulate(h, shift, scale):
    return h * (1 + scale[:, None, :]) + shift[:, None, :]


def _fourier_mix(u):
    b, s, _ = u.shape
    ug = u.reshape(b, s, N_FOURIER_GROUPS, FOURIER_GROUP_DIM).astype(jnp.float32)
    f = jnp.fft.fft2(ug, axes=(1, 3), norm="ortho").real
    return f.reshape(b, s, FOURIER_WIDTH).astype(u.dtype)


def _centred_mean(u, w):
    s = u.shape[1]
    left = w // 2
    right = w - 1 - left
    up = jnp.pad(u.astype(jnp.float32), ((0, 0), (left, right), (0, 0)))
    cs = jnp.pad(jnp.cumsum(up, axis=1), ((0, 0), (1, 0), (0, 0)))
    win = cs[:, w:w + s] - cs[:, :s]
    t = jnp.arange(s)
    cnt = jnp.minimum(t + right, s - 1) - jnp.maximum(t - left, 0) + 1
    return win / cnt.astype(jnp.float32)[None, :, None]


def _pool_mix(u, w_pg, pool_scale):
    b, s, _ = u.shape
    ug = u.reshape(b, s, N_POOL_GROUPS, POOL_GROUP_DIM)
    diffs = [_centred_mean(ug[:, :, i], POOL_WINDOWS[i]) - ug[:, :, i].astype(jnp.float32)
             for i in range(N_POOL_GROUPS)]
    d = jnp.stack(diffs, axis=2).astype(u.dtype)
    y = jnp.einsum("bsgc,gcd->bsgd", d, w_pg)
    return y.reshape(b, s, POOL_WIDTH) * pool_scale


def _layer(x, c, w_ada, b_ada, g_pre_mix, w_in, w_fo, w_pg, pool_scale, w_po, w_out,
           g_post_mix, g_pre_ffn, w_gate, w_up, w_down, g_post_ffn):
    ada = jnp.einsum("bd,de->be", jax.nn.silu(c), w_ada) + b_ada
    sh1, sc1, gt1, sh2, sc2, gt2 = jnp.split(ada, N_ADA, axis=-1)

    h = _modulate(_rmsnorm(x, g_pre_mix), sh1, sc1)
    proj = jnp.einsum("bsd,de->bse", h, w_in)
    u_f = proj[..., :FOURIER_WIDTH]
    u_p = proj[..., FOURIER_WIDTH:FOURIER_WIDTH + POOL_WIDTH]
    gate_logits = proj[..., FOURIER_WIDTH + POOL_WIDTH:]
    y_f = jnp.einsum("bsf,fd->bsd", _fourier_mix(u_f), w_fo)
    y_p = jnp.einsum("bsp,pd->bsd", _pool_mix(u_p, w_pg, pool_scale), w_po)
    gates = jax.nn.sigmoid(gate_logits)
    g_f = gates[..., :D_MODEL]
    g_p = gates[..., D_MODEL:]
    m = jnp.einsum("bsd,de->bse", g_f * y_f + g_p * y_p, w_out)
    x = x + gt1[:, None, :] * _rmsnorm(m, g_post_mix)

    h = _modulate(_rmsnorm(x, g_pre_ffn), sh2, sc2)
    a = jnp.einsum("bsd,df->bsf", h, w_gate)
    bup = jnp.einsum("bsd,df->bsf", h, w_up)
    z = jnp.einsum("bsf,fd->bsd", jax.nn.silu(a) * bup, w_down)
    x = x + gt2[:, None, :] * _rmsnorm(z, g_post_ffn)
    return x


def setup_inputs(seed: int = 0) -> dict:
    key = jax.random.key(seed)
    ks = jax.random.split(key, 20)
    D = D_MODEL

    def nrm(k, shape, fan_in):
        return jax.random.normal(k, shape, jnp.float32) * (fan_in ** -0.5)

    def gain(k, shape):
        return 1.0 + 0.1 * jax.random.normal(k, shape, jnp.float32)

    return {
        "x_prompt": jax.random.normal(ks[0], (BATCH, SEQ, D), jnp.float32),
        "x_sample": jax.random.normal(ks[1], (DEC_BATCH, DEC_SEQ, D), jnp.float32),
        "c_prompt": jax.random.normal(ks[2], (BATCH, D), jnp.float32),
        "c_sample": jax.random.normal(ks[3], (DEC_BATCH, D), jnp.float32),
        "w_ada": nrm(ks[4], (DEPTH, D, N_ADA * D), D),
        "b_ada": 0.01 * jax.random.normal(ks[5], (DEPTH, N_ADA * D), jnp.float32),
        "g_pre_mix": gain(ks[6], (DEPTH, D)),
        "w_in": nrm(ks[7], (DEPTH, D, IN_WIDTH), D),
        "w_fo": nrm(ks[8], (DEPTH, FOURIER_WIDTH, D), FOURIER_WIDTH),
        "w_pg": nrm(ks[9], (DEPTH, N_POOL_GROUPS, POOL_GROUP_DIM, POOL_GROUP_DIM), POOL_GROUP_DIM),
        "pool_scale": gain(ks[10], (DEPTH, POOL_WIDTH)),
        "w_po": nrm(ks[11], (DEPTH, POOL_WIDTH, D), POOL_WIDTH),
        "w_out": nrm(ks[12], (DEPTH, D, D), D),
        "g_post_mix": gain(ks[13], (DEPTH, D)),
        "g_pre_ffn": gain(ks[14], (DEPTH, D)),
        "w_gate": nrm(ks[15], (DEPTH, D, D_FF), D),
        "w_up": nrm(ks[16], (DEPTH, D, D_FF), D),
        "w_down": nrm(ks[17], (DEPTH, D_FF, D), D_FF),
        "g_post_ffn": gain(ks[18], (DEPTH, D)),
    }


def reference(x_prompt, x_sample, c_prompt, c_sample, w_ada, b_ada, g_pre_mix, w_in, w_fo,
              w_pg, pool_scale, w_po, w_out, g_post_mix, g_pre_ffn, w_gate, w_up, w_down,
              g_post_ffn):
    y_prompt = x_prompt
    y_sample = x_sample
    for l in range(DEPTH):
        params = (w_ada[l], b_ada[l], g_pre_mix[l], w_in[l], w_fo[l], w_pg[l], pool_scale[l],
                  w_po[l], w_out[l], g_post_mix[l], g_pre_ffn[l], w_gate[l], w_up[l],
                  w_down[l], g_post_ffn[l])
        y_prompt = _layer(y_prompt, c_prompt, *params)
        y_sample = _layer(y_sample, c_sample, *params)
    return (y_prompt, y_sample)
```

```python
import functools

import numpy as np
import jax
import jax.numpy as jnp
from jax import lax
from jax.experimental import pallas as pl
from jax.experimental.pallas import tpu as pltpu

D_MODEL = 1024
N_GROUPS = 4
GROUP_DIM = 128
BRANCH_WIDTH = N_GROUPS * GROUP_DIM
POOL_WINDOWS = (2, 4, 8, 16)
D_FF = 2816
N_ADA = 6
EPS = 1e-6

LANES = 128
BF16_ROWS = 16
MXU_DIM = 256
VMEM_LIMIT_BYTES = 56 * 1024 * 1024

SEQ_MINOR = 128
HALO = BF16_ROWS
TOKEN_TILE = 512
ROW_BLOCKS_PER_STEP = 16
ADA_COLS = 512
FF_CHUNKS = ((0, 1024), (1024, 1024), (2048, 768))

BF16 = jnp.bfloat16
F32 = jnp.float32


def _dot(a, b):
    return jnp.dot(a, b, preferred_element_type=F32)


def _sigmoid(v):
    return 1.0 / (1.0 + jnp.exp(-v))


def _rms_scale(v):
    return lax.rsqrt(jnp.mean(v * v, axis=-1, keepdims=True) + EPS)


def _ada_kernel(c_ref, w_ref, b_ref, o_ref):
    c = c_ref[...]
    s = (c * _sigmoid(c)).astype(BF16)
    o_ref[...] = _dot(s, w_ref[...].astype(BF16)) + b_ref[...]


def _ada(c_pad, w_ada, b_ada):
    rows, d = c_pad.shape
    n_out = w_ada.shape[1]
    return pl.pallas_call(
        _ada_kernel,
        grid=(n_out // ADA_COLS,),
        in_specs=[
            pl.BlockSpec((rows, d), lambda j: (0, 0)),
            pl.BlockSpec((d, ADA_COLS), lambda j: (0, j)),
            pl.BlockSpec((1, ADA_COLS), lambda j: (0, j)),
        ],
        out_specs=pl.BlockSpec((rows, ADA_COLS), lambda j: (0, j)),
        out_shape=jax.ShapeDtypeStruct((rows, n_out), F32),
        compiler_params=pltpu.CompilerParams(
            dimension_semantics=("arbitrary",), vmem_limit_bytes=VMEM_LIMIT_BYTES),
        name="ada",
    )(c_pad, w_ada, b_ada)


def _mix_in_kernel(tiles_per_seq, seq_len,
                   x_ref, xp_ref, xn_ref, mod_ref, g_ref, wf_ref, wp_ref, wg_ref,
                   wpg_ref, ps_ref, wpo_ref,
                   uf_ref, gf_ref, p_ref,
                   hext_ref, up_ref):
    t = TOKEN_TILE
    ti = lax.rem(pl.program_id(0), tiles_per_seq)
    mod = mod_ref[0]
    scale = g_ref[...] * (1.0 + mod[1:2, :])
    shift = mod[0:1, :]

    def normed(v):
        return (v * _rms_scale(v) * scale + shift).astype(BF16)

    h = normed(x_ref[...])
    hext_ref[pl.ds(0, HALO), :] = normed(xp_ref[...])
    hext_ref[pl.ds(HALO, t), :] = h
    hext_ref[pl.ds(HALO + t, HALO), :] = normed(xn_ref[...])

    uf_ref[...] = _dot(h, wf_ref[...]).astype(uf_ref.dtype)

    gates = _sigmoid(_dot(h, wg_ref[...]))
    gf_ref[...] = gates[:, :D_MODEL].astype(gf_ref.dtype)
    g_p = gates[:, D_MODEL:]

    up_ref[...] = _dot(hext_ref[...], wp_ref[...])

    @pl.when(ti == 0)
    def _():
        up_ref[pl.ds(0, HALO), :] = jnp.zeros((HALO, BRANCH_WIDTH), F32)

    @pl.when(ti == tiles_per_seq - 1)
    def _():
        up_ref[pl.ds(HALO + t, HALO), :] = jnp.zeros((HALO, BRANCH_WIDTH), F32)

    pos = ti * t + lax.broadcasted_iota(jnp.int32, (t, GROUP_DIM), 0)
    diffs = []
    for g, w in enumerate(POOL_WINDOWS):
        left = w // 2
        right = w - 1 - left
        lanes = pl.ds(g * GROUP_DIM, GROUP_DIM)
        acc = up_ref[pl.ds(HALO - left, t), lanes]
        for j in range(-left + 1, right + 1):
            acc = acc + up_ref[pl.ds(HALO + j, t), lanes]
        cnt = (jnp.minimum(pos + right, seq_len - 1) - jnp.maximum(pos - left, 0) + 1)
        diffs.append(acc / cnt.astype(F32) - up_ref[pl.ds(HALO, t), lanes])
    ys = []
    for pair in range(N_GROUPS // 2):
        d2 = jnp.concatenate(diffs[2 * pair:2 * pair + 2], axis=1).astype(BF16)
        ys.append(_dot(d2, wpg_ref[pair]))
    y = (jnp.concatenate(ys, axis=1) * ps_ref[...]).astype(BF16)
    p_ref[...] = (g_p * _dot(y, wpo_ref[...])).astype(p_ref.dtype)


def _const_spec(shape):
    zeros = (0,) * len(shape)
    return pl.BlockSpec(shape, lambda i: zeros, pipeline_mode=pl.Buffered(1))


def _mix_in(x2d, mods, seq_len, g_pre, w_f, w_p, w_g, wpg_bd, pool_scale, w_po):
    n_tok = x2d.shape[0]
    t = TOKEN_TILE
    tiles_per_seq = seq_len // t
    halo_blocks_per_tile = t // HALO
    n_halo_blocks = n_tok // HALO
    row = lambda i: (i, 0)
    return pl.pallas_call(
        functools.partial(_mix_in_kernel, tiles_per_seq, seq_len),
        grid=(n_tok // t,),
        in_specs=[
            pl.BlockSpec((t, D_MODEL), row),
            pl.BlockSpec((HALO, D_MODEL),
                         lambda i: (jnp.maximum(i * halo_blocks_per_tile - 1, 0), 0)),
            pl.BlockSpec((HALO, D_MODEL),
                         lambda i: (jnp.minimum((i + 1) * halo_blocks_per_tile,
                                                n_halo_blocks - 1), 0)),
            pl.BlockSpec((1, N_ADA, D_MODEL), lambda i: (i // tiles_per_seq, 0, 0)),
            _const_spec((1, D_MODEL)),
            _const_spec((D_MODEL, BRANCH_WIDTH)),
            _const_spec((D_MODEL, BRANCH_WIDTH)),
            _const_spec((D_MODEL, 2 * D_MODEL)),
            _const_spec((N_GROUPS // 2, MXU_DIM, MXU_DIM)),
            _const_spec((1, BRANCH_WIDTH)),
            _const_spec((BRANCH_WIDTH, D_MODEL)),
        ],
        out_specs=[
            pl.BlockSpec((t, BRANCH_WIDTH), row),
            pl.BlockSpec((t, D_MODEL), row),
            pl.BlockSpec((t, D_MODEL), row),
        ],
        out_shape=[
            jax.ShapeDtypeStruct((n_tok, BRANCH_WIDTH), BF16),
            jax.ShapeDtypeStruct((n_tok, D_MODEL), BF16),
            jax.ShapeDtypeStruct((n_tok, D_MODEL), BF16),
        ],
        scratch_shapes=[
            pltpu.VMEM((t + 2 * HALO, D_MODEL), BF16),
            pltpu.VMEM((t + 2 * HALO, BRANCH_WIDTH), F32),
        ],
        compiler_params=pltpu.CompilerParams(
            dimension_semantics=("arbitrary",), vmem_limit_bytes=VMEM_LIMIT_BYTES),
        name="mix_in",
    )(x2d, x2d, x2d, mods, g_pre, w_f, w_p, w_g, wpg_bd, pool_scale, w_po)


def _dft_stage1_kernel(x_ref, w_ref, o_ref):
    w = w_ref[...]
    for j in range(ROW_BLOCKS_PER_STEP):
        o_ref[j] = _dot(w, x_ref[j]).astype(o_ref.dtype)


def _dft_stage1(x_blocks, w_bd):
    n_blocks = x_blocks.shape[0]
    rb = ROW_BLOCKS_PER_STEP
    return pl.pallas_call(
        _dft_stage1_kernel,
        grid=(n_blocks // rb,),
        in_specs=[
            pl.BlockSpec((rb, SEQ_MINOR, BRANCH_WIDTH), lambda i: (i, 0, 0)),
            _const_spec((2 * SEQ_MINOR, SEQ_MINOR)),
        ],
        out_specs=pl.BlockSpec((rb, 2 * SEQ_MINOR, BRANCH_WIDTH), lambda i: (i, 0, 0)),
        out_shape=jax.ShapeDtypeStruct((n_blocks, 2 * SEQ_MINOR, BRANCH_WIDTH), BF16),
        compiler_params=pltpu.CompilerParams(
            dimension_semantics=("arbitrary",), vmem_limit_bytes=VMEM_LIMIT_BYTES),
        name="dft_stage1",
    )(x_blocks, w_bd)


def _dft_stage2_kernel(n1, z_ref, c2_ref, s2_ref, tc_ref, ts_ref, o_ref):
    c2 = c2_ref[...]
    s2 = s2_ref[...]
    base = pl.program_id(0) * ROW_BLOCKS_PER_STEP
    for j in range(ROW_BLOCKS_PER_STEP):
        k1 = lax.rem(base + j, n1)
        tc = tc_ref[pl.ds(k1, 1), :]
        ts = ts_ref[pl.ds(k1, 1), :]
        cos_t = c2 * tc - s2 * ts
        sin_t = s2 * tc + c2 * ts
        lhs = jnp.concatenate(
            [jnp.concatenate([cos_t, sin_t], axis=1),
             jnp.concatenate([-sin_t, cos_t], axis=1)], axis=0).astype(BF16)
        o_ref[j] = _dot(lhs, z_ref[j]).astype(o_ref.dtype)


def _dft_stage2(z_blocks, n1, c2, s2, tc, ts):
    n_blocks = z_blocks.shape[0]
    rb = ROW_BLOCKS_PER_STEP
    return pl.pallas_call(
        functools.partial(_dft_stage2_kernel, n1),
        grid=(n_blocks // rb,),
        in_specs=[
            pl.BlockSpec((rb, 2 * SEQ_MINOR, BRANCH_WIDTH), lambda i: (i, 0, 0)),
            _const_spec((SEQ_MINOR, SEQ_MINOR)),
            _const_spec((SEQ_MINOR, SEQ_MINOR)),
            _const_spec((n1, SEQ_MINOR)),
            _const_spec((n1, SEQ_MINOR)),
        ],
        out_specs=pl.BlockSpec((rb, 2 * SEQ_MINOR, BRANCH_WIDTH), lambda i: (i, 0, 0)),
        out_shape=jax.ShapeDtypeStruct((n_blocks, 2 * SEQ_MINOR, BRANCH_WIDTH), BF16),
        compiler_params=pltpu.CompilerParams(
            dimension_semantics=("arbitrary",), vmem_limit_bytes=VMEM_LIMIT_BYTES),
        name="dft_stage2",
    )(z_blocks, c2, s2, tc, ts)


@functools.lru_cache(maxsize=None)
def _dft_tables(seq_len):
    n2 = SEQ_MINOR
    n1 = seq_len // n2
    reps = n2 // n1
    k = np.arange(n1, dtype=np.float64)
    ang1 = 2.0 * np.pi * np.outer(k, k) / n1
    w1 = np.concatenate([np.cos(ang1), -np.sin(ang1)], axis=0)
    w_bd = np.kron(np.eye(reps), w1)
    m = np.arange(n2, dtype=np.float64)
    ang2 = 2.0 * np.pi * np.outer(m, m) / n2
    angt = 2.0 * np.pi * np.outer(k, m) / seq_len
    as32 = lambda a: np.asarray(a, dtype=np.float32)
    return (as32(w_bd), as32(np.cos(ang2)), as32(np.sin(ang2)),
            as32(np.cos(angt)), as32(np.sin(angt)))


@functools.lru_cache(maxsize=None)
def _channel_dft_table(seq_len):
    c = np.arange(GROUP_DIM, dtype=np.float64)
    ang = 2.0 * np.pi * np.outer(c, c) / GROUP_DIM
    norm = 1.0 / np.sqrt(float(seq_len) * GROUP_DIM)
    return np.asarray(np.concatenate([np.cos(ang), np.sin(ang)], axis=0) * norm,
                      dtype=np.float32)


def _sequence_dft(u_f, batch, seq_len):
    n2 = SEQ_MINOR
    n1 = seq_len // n2
    w_bd, c2, s2, tc, ts = _dft_tables(seq_len)
    x = u_f.reshape(batch, n1, n2, BRANCH_WIDTH).transpose(0, 2, 1, 3)
    x = x.reshape(batch * seq_len // n2, n2, BRANCH_WIDTH)
    z = _dft_stage1(x, jnp.asarray(w_bd).astype(BF16))
    z = z.reshape(batch, n2, 2, n1, BRANCH_WIDTH).transpose(0, 3, 2, 1, 4)
    z = z.reshape(batch * n1, 2 * n2, BRANCH_WIDTH)
    q = _dft_stage2(z, n1, jnp.asarray(c2), jnp.asarray(s2), jnp.asarray(tc), jnp.asarray(ts))
    q = q.reshape(batch, n1, 2, n2, BRANCH_WIDTH).transpose(0, 3, 1, 2, 4)
    return q.reshape(batch * seq_len, 2 * BRANCH_WIDTH)


def _mix_out_kernel(x_ref, q_ref, gf_ref, p_ref, mod_ref, gpm_ref, gpf_ref, gqf_ref,
                    cs_ref, wfo_ref, wout_ref, wgate_ref, wup_ref, wdown_ref, o_ref):
    mod = mod_ref[0]
    x = x_ref[...]
    q = q_ref[...]
    cs = cs_ref[...]
    ys = []
    for g in range(N_GROUPS):
        qg = jnp.concatenate(
            [q[:, g * GROUP_DIM:(g + 1) * GROUP_DIM],
             q[:, BRANCH_WIDTH + g * GROUP_DIM:BRANCH_WIDTH + (g + 1) * GROUP_DIM]], axis=1)
        ys.append(_dot(qg, cs))
    y = jnp.concatenate(ys, axis=1).astype(BF16)
    y_f = _dot(y, wfo_ref[...])
    mix = (gf_ref[...].astype(F32) * y_f + p_ref[...].astype(F32)).astype(BF16)
    m = _dot(mix, wout_ref[...])
    x1 = x + (mod[2:3, :] * gpm_ref[...]) * (m * _rms_scale(m))

    scale2 = gpf_ref[...] * (1.0 + mod[4:5, :])
    h2 = (x1 * _rms_scale(x1) * scale2 + mod[3:4, :]).astype(BF16)
    z = None
    for start, size in FF_CHUNKS:
        cols = pl.ds(start, size)
        a = _dot(h2, wgate_ref[:, cols])
        b = _dot(h2, wup_ref[:, cols])
        s = (a * _sigmoid(a) * b).astype(BF16)
        part = _dot(s, wdown_ref[cols, :])
        z = part if z is None else z + part
    o_ref[...] = x1 + (mod[5:6, :] * gqf_ref[...]) * (z * _rms_scale(z))


def _mix_out(x2d, q, g_f, p, mods, seq_len, g_post_mix, g_pre_ffn, g_post_ffn,
             cs, w_fo, w_out, w_gate, w_up, w_down):
    n_tok = x2d.shape[0]
    t = TOKEN_TILE
    tiles_per_seq = seq_len // t
    row = lambda i: (i, 0)
    return pl.pallas_call(
        _mix_out_kernel,
        grid=(n_tok // t,),
        in_specs=[
            pl.BlockSpec((t, D_MODEL), row),
            pl.BlockSpec((t, 2 * BRANCH_WIDTH), row),
            pl.BlockSpec((t, D_MODEL), row),
            pl.BlockSpec((t, D_MODEL), row),
            pl.BlockSpec((1, N_ADA, D_MODEL), lambda i: (i // tiles_per_seq, 0, 0)),
            _const_spec((1, D_MODEL)),
            _const_spec((1, D_MODEL)),
            _const_spec((1, D_MODEL)),
            _const_spec((2 * GROUP_DIM, GROUP_DIM)),
            _const_spec((BRANCH_WIDTH, D_MODEL)),
            _const_spec((D_MODEL, D_MODEL)),
            _const_spec((D_MODEL, D_FF)),
            _const_spec((D_MODEL, D_FF)),
            _const_spec((D_FF, D_MODEL)),
        ],
        out_specs=pl.BlockSpec((t, D_MODEL), row),
        out_shape=jax.ShapeDtypeStruct((n_tok, D_MODEL), F32),
        compiler_params=pltpu.CompilerParams(
            dimension_semantics=("arbitrary",), vmem_limit_bytes=VMEM_LIMIT_BYTES),
        name="mix_out",
    )(x2d, q, g_f, p, mods, g_post_mix, g_pre_ffn, g_post_ffn,
      cs, w_fo, w_out, w_gate, w_up, w_down)


def _layer(x, mods, weights):
    (g_pre_mix, w_f, w_p, w_g, wpg_bd, pool_scale, w_po, g_post_mix, g_pre_ffn,
     g_post_ffn, w_fo, w_out, w_gate, w_up, w_down) = weights
    batch, seq_len, d = x.shape
    x2d = x.reshape(batch * seq_len, d)
    u_f, g_f, p = _mix_in(x2d, mods, seq_len, g_pre_mix, w_f, w_p, w_g, wpg_bd,
                          pool_scale, w_po)
    q = _sequence_dft(u_f, batch, seq_len)
    cs = jnp.asarray(_channel_dft_table(seq_len)).astype(BF16)
    y = _mix_out(x2d, q, g_f, p, mods, seq_len, g_post_mix, g_pre_ffn, g_post_ffn,
                 cs, w_fo, w_out, w_gate, w_up, w_down)
    return y.reshape(batch, seq_len, d)


def kernel(x_prompt, x_sample, c_prompt, c_sample, w_ada, b_ada, g_pre_mix, w_in, w_fo, w_pg,
           pool_scale, w_po, w_out, g_post_mix, g_pre_ffn, w_gate, w_up, w_down, g_post_ffn):
    depth = w_ada.shape[0]
    y_prompt, y_sample = x_prompt, x_sample
    bp, bs = c_prompt.shape[0], c_sample.shape[0]
    pad_rows = -(bp + bs) % BF16_ROWS
    c_pad = jnp.concatenate(
        [c_prompt, c_sample, jnp.zeros((pad_rows, D_MODEL), F32)], axis=0)
    for l in range(depth):
        ada = _ada(c_pad, w_ada[l], b_ada[l][None, :])
        mods_p = ada[:bp].reshape(bp, N_ADA, D_MODEL)
        mods_s = ada[bp:bp + bs].reshape(bs, N_ADA, D_MODEL)
        w_in_l = w_in[l].astype(BF16)
        zero_blk = jnp.zeros((GROUP_DIM, GROUP_DIM), F32)
        wpg_bd = jnp.stack([
            jnp.block([[w_pg[l, 2 * pr], zero_blk], [zero_blk, w_pg[l, 2 * pr + 1]]])
            for pr in range(N_GROUPS // 2)]).astype(BF16)
        weights = (
            g_pre_mix[l][None, :],
            w_in_l[:, :BRANCH_WIDTH],
            w_in_l[:, BRANCH_WIDTH:2 * BRANCH_WIDTH],
            w_in_l[:, 2 * BRANCH_WIDTH:],
            wpg_bd,
            pool_scale[l][None, :],
            w_po[l].astype(BF16),
            g_post_mix[l][None, :],
            g_pre_ffn[l][None, :],
            g_post_ffn[l][None, :],
            w_fo[l].astype(BF16),
            w_out[l].astype(BF16),
            w_gate[l].astype(BF16),
            w_up[l].astype(BF16),
            w_down[l].astype(BF16),
        )
        y_prompt = _layer(y_prompt, mods_p, weights)
        y_sample = _layer(y_sample, mods_s, weights)
    return (y_prompt, y_sample)
```

```python
import functools

import numpy as np
import jax
import jax.numpy as jnp
from jax import lax
from jax.experimental import pallas as pl
from jax.experimental.pallas import tpu as pltpu

D_MODEL = 1024
N_GROUPS = 4
GROUP_DIM = 128
BRANCH_WIDTH = N_GROUPS * GROUP_DIM
POOL_WINDOWS = (2, 4, 8, 16)
D_FF = 2816
N_ADA = 6
EPS = 1e-6

LANES = 128
BF16_ROWS = 16
MXU_DIM = 256
VMEM_LIMIT_BYTES = 56 * 1024 * 1024

SEQ_MINOR = 128
HALO = BF16_ROWS
TOKEN_TILE = 512
SUB_TILES = 2
ROW_BLOCKS_PER_STEP = 16
ADA_COLS = 512
FF_CHUNKS = ((0, 1024), (1024, 1024), (2048, 768))

BF16 = jnp.bfloat16
F32 = jnp.float32


def _dot(a, b):
    return jnp.dot(a, b, preferred_element_type=F32)


def _sigmoid(v):
    return 1.0 / (1.0 + jnp.exp(-v))


def _rms_scale(v):
    return lax.rsqrt(jnp.mean(v * v, axis=-1, keepdims=True) + EPS)


def _ada_kernel(c_ref, w_ref, b_ref, o_ref):
    c = c_ref[...]
    s = (c * _sigmoid(c)).astype(BF16)
    o_ref[...] = _dot(s, w_ref[...].astype(BF16)) + b_ref[...]


def _ada(c_pad, w_ada, b_ada):
    rows, d = c_pad.shape
    n_out = w_ada.shape[1]
    return pl.pallas_call(
        _ada_kernel,
        grid=(n_out // ADA_COLS,),
        in_specs=[
            pl.BlockSpec((rows, d), lambda j: (0, 0)),
            pl.BlockSpec((d, ADA_COLS), lambda j: (0, j)),
            pl.BlockSpec((1, ADA_COLS), lambda j: (0, j)),
        ],
        out_specs=pl.BlockSpec((rows, ADA_COLS), lambda j: (0, j)),
        out_shape=jax.ShapeDtypeStruct((rows, n_out), F32),
        compiler_params=pltpu.CompilerParams(
            dimension_semantics=("arbitrary",), vmem_limit_bytes=VMEM_LIMIT_BYTES),
        name="ada",
    )(c_pad, w_ada, b_ada)


def _window_sum(ext, w):
    n = ext.shape[0]
    s = ext
    k = 1
    while k < w // 2:
        s = s + pltpu.roll(s, n - k, axis=0)
        k *= 2
    return s + pltpu.roll(s, w // 2, axis=0)


def _mix_in_kernel(tiles_per_seq, seq_len,
                   x_ref, xp_ref, xn_ref, mod_ref, g_ref, wf_ref, wp_ref, wg_ref,
                   wpg_ref, ps_ref, wpo_ref,
                   uf_ref, gf_ref, p_ref):
    t = TOKEN_TILE
    ti = lax.rem(pl.program_id(0), tiles_per_seq)
    mod = mod_ref[0]
    scale = g_ref[...] * (1.0 + mod[1:2, :])
    shift = mod[0:1, :]

    def normed(v):
        return v * _rms_scale(v) * scale + shift

    ts = t // SUB_TILES
    subs = [pl.ds(s * ts, ts) for s in range(SUB_TILES)]
    hs = [normed(x_ref[rows, :]).astype(BF16) for rows in subs]
    h_prev = jnp.where(ti == 0, 0.0, normed(xp_ref[...])).astype(BF16)
    h_next = jnp.where(ti == tiles_per_seq - 1, 0.0, normed(xn_ref[...])).astype(BF16)
    befores = [h_prev] + [h[ts - HALO:] for h in hs[:-1]]
    afters = [h[:HALO] for h in hs[1:]] + [h_next]

    ups, gates = [], []
    for s in range(SUB_TILES):
        ups.append(_dot(jnp.concatenate([befores[s], hs[s], afters[s]], axis=0), wp_ref[...]))
        uf_ref[subs[s], :] = _dot(hs[s], wf_ref[...]).astype(uf_ref.dtype)
        gates.append(_dot(hs[s], wg_ref[...]))

    slab = 8
    row = lax.broadcasted_iota(jnp.int32, (slab, GROUP_DIM), 0)

    def pool_diff(up, first_pos):
        diffs = []
        for g, w in enumerate(POOL_WINDOWS):
            assert w & (w - 1) == 0 and w // 2 <= slab
            left = w // 2
            right = w - 1 - left
            ext = up[:, g * GROUP_DIM:(g + 1) * GROUP_DIM]
            win = _window_sum(ext, w)[HALO:HALO + ts]
            u = ext[HALO:HALO + ts]

            def count(pos):
                return (jnp.minimum(pos + right, seq_len - 1)
                        - jnp.maximum(pos - left, 0) + 1).astype(F32)

            diffs.append(jnp.concatenate([
                win[:slab] / count(first_pos + row) - u[:slab],
                win[slab:ts - slab] * (1.0 / w) - u[slab:ts - slab],
                win[ts - slab:] / count(first_pos + (ts - slab) + row) - u[ts - slab:],
            ], axis=0))
        return diffs

    for s in range(SUB_TILES):
        diffs = pool_diff(ups[s], ti * t + s * ts)
        ys = []
        for pair in range(N_GROUPS // 2):
            d2 = jnp.concatenate(diffs[2 * pair:2 * pair + 2], axis=1).astype(BF16)
            ys.append(_dot(d2, wpg_ref[pair]))
        y = (jnp.concatenate(ys, axis=1) * ps_ref[...]).astype(BF16)
        g = _sigmoid(gates[s])
        gf_ref[subs[s], :] = g[:, :D_MODEL].astype(gf_ref.dtype)
        p_ref[subs[s], :] = (g[:, D_MODEL:] * _dot(y, wpo_ref[...])).astype(p_ref.dtype)


def _const_spec(shape):
    zeros = (0,) * len(shape)
    return pl.BlockSpec(shape, lambda i: zeros, pipeline_mode=pl.Buffered(1))


def _mix_in(x2d, mods, seq_len, g_pre, w_f, w_p, w_g, wpg_bd, pool_scale, w_po):
    n_tok = x2d.shape[0]
    t = TOKEN_TILE
    tiles_per_seq = seq_len // t
    halo_blocks_per_tile = t // HALO
    n_halo_blocks = n_tok // HALO
    row = lambda i: (i, 0)
    return pl.pallas_call(
        functools.partial(_mix_in_kernel, tiles_per_seq, seq_len),
        grid=(n_tok // t,),
        in_specs=[
            pl.BlockSpec((t, D_MODEL), row),
            pl.BlockSpec((HALO, D_MODEL),
                         lambda i: (jnp.maximum(i * halo_blocks_per_tile - 1, 0), 0)),
            pl.BlockSpec((HALO, D_MODEL),
                         lambda i: (jnp.minimum((i + 1) * halo_blocks_per_tile,
                                                n_halo_blocks - 1), 0)),
            pl.BlockSpec((1, N_ADA, D_MODEL), lambda i: (i // tiles_per_seq, 0, 0)),
            _const_spec((1, D_MODEL)),
            _const_spec((D_MODEL, BRANCH_WIDTH)),
            _const_spec((D_MODEL, BRANCH_WIDTH)),
            _const_spec((D_MODEL, 2 * D_MODEL)),
            _const_spec((N_GROUPS // 2, MXU_DIM, MXU_DIM)),
            _const_spec((1, BRANCH_WIDTH)),
            _const_spec((BRANCH_WIDTH, D_MODEL)),
        ],
        out_specs=[
            pl.BlockSpec((t, BRANCH_WIDTH), row),
            pl.BlockSpec((t, D_MODEL), row),
            pl.BlockSpec((t, D_MODEL), row),
        ],
        out_shape=[
            jax.ShapeDtypeStruct((n_tok, BRANCH_WIDTH), BF16),
            jax.ShapeDtypeStruct((n_tok, D_MODEL), BF16),
            jax.ShapeDtypeStruct((n_tok, D_MODEL), BF16),
        ],
        compiler_params=pltpu.CompilerParams(
            dimension_semantics=("arbitrary",), vmem_limit_bytes=VMEM_LIMIT_BYTES),
        name="mix_in",
    )(x2d, x2d, x2d, mods, g_pre, w_f, w_p, w_g, wpg_bd, pool_scale, w_po)


def _dft_stage1_kernel(x_ref, w_ref, o_ref):
    w = w_ref[...]
    for j in range(ROW_BLOCKS_PER_STEP):
        o_ref[j] = _dot(w, x_ref[j]).astype(o_ref.dtype)


def _dft_stage1(x_blocks, w_bd):
    n_blocks = x_blocks.shape[0]
    rb = ROW_BLOCKS_PER_STEP
    return pl.pallas_call(
        _dft_stage1_kernel,
        grid=(n_blocks // rb,),
        in_specs=[
            pl.BlockSpec((rb, SEQ_MINOR, BRANCH_WIDTH), lambda i: (i, 0, 0)),
            _const_spec((2 * SEQ_MINOR, SEQ_MINOR)),
        ],
        out_specs=pl.BlockSpec((rb, 2 * SEQ_MINOR, BRANCH_WIDTH), lambda i: (i, 0, 0)),
        out_shape=jax.ShapeDtypeStruct((n_blocks, 2 * SEQ_MINOR, BRANCH_WIDTH), BF16),
        compiler_params=pltpu.CompilerParams(
            dimension_semantics=("arbitrary",), vmem_limit_bytes=VMEM_LIMIT_BYTES),
        name="dft_stage1",
    )(x_blocks, w_bd)


def _dft_stage2_kernel(n1, z_ref, c2_ref, s2_ref, tc_ref, ts_ref, o_ref):
    c2 = c2_ref[...]
    s2 = s2_ref[...]
    base = pl.program_id(0) * ROW_BLOCKS_PER_STEP
    for j in range(ROW_BLOCKS_PER_STEP):
        k1 = lax.rem(base + j, n1)
        tc = tc_ref[pl.ds(k1, 1), :]
        ts = ts_ref[pl.ds(k1, 1), :]
        cos_t = c2 * tc - s2 * ts
        sin_t = s2 * tc + c2 * ts
        lhs = jnp.concatenate(
            [jnp.concatenate([cos_t, sin_t], axis=1),
             jnp.concatenate([-sin_t, cos_t], axis=1)], axis=0).astype(BF16)
        o_ref[j] = _dot(lhs, z_ref[j]).astype(o_ref.dtype)


def _dft_stage2(z_blocks, n1, c2, s2, tc, ts):
    n_blocks = z_blocks.shape[0]
    rb = ROW_BLOCKS_PER_STEP
    return pl.pallas_call(
        functools.partial(_dft_stage2_kernel, n1),
        grid=(n_blocks // rb,),
        in_specs=[
            pl.BlockSpec((rb, 2 * SEQ_MINOR, BRANCH_WIDTH), lambda i: (i, 0, 0)),
            _const_spec((SEQ_MINOR, SEQ_MINOR)),
            _const_spec((SEQ_MINOR, SEQ_MINOR)),
            _const_spec((n1, SEQ_MINOR)),
            _const_spec((n1, SEQ_MINOR)),
        ],
        out_specs=pl.BlockSpec((rb, 2 * SEQ_MINOR, BRANCH_WIDTH), lambda i: (i, 0, 0)),
        out_shape=jax.ShapeDtypeStruct((n_blocks, 2 * SEQ_MINOR, BRANCH_WIDTH), BF16),
        compiler_params=pltpu.CompilerParams(
            dimension_semantics=("arbitrary",), vmem_limit_bytes=VMEM_LIMIT_BYTES),
        name="dft_stage2",
    )(z_blocks, c2, s2, tc, ts)


@functools.lru_cache(maxsize=None)
def _dft_tables(seq_len):
    n2 = SEQ_MINOR
    n1 = seq_len // n2
    reps = n2 // n1
    k = np.arange(n1, dtype=np.float64)
    ang1 = 2.0 * np.pi * np.outer(k, k) / n1
    w1 = np.concatenate([np.cos(ang1), -np.sin(ang1)], axis=0)
    w_bd = np.kron(np.eye(reps), w1)
    m = np.arange(n2, dtype=np.float64)
    ang2 = 2.0 * np.pi * np.outer(m, m) / n2
    angt = 2.0 * np.pi * np.outer(k, m) / seq_len
    as32 = lambda a: np.asarray(a, dtype=np.float32)
    return (as32(w_bd), as32(np.cos(ang2)), as32(np.sin(ang2)),
            as32(np.cos(angt)), as32(np.sin(angt)))


@functools.lru_cache(maxsize=None)
def _channel_dft_table(seq_len):
    c = np.arange(GROUP_DIM, dtype=np.float64)
    ang = 2.0 * np.pi * np.outer(c, c) / GROUP_DIM
    norm = 1.0 / np.sqrt(float(seq_len) * GROUP_DIM)
    return np.asarray(np.concatenate([np.cos(ang), np.sin(ang)], axis=0) * norm,
                      dtype=np.float32)


def _sequence_dft(u_f, batch, seq_len):
    n2 = SEQ_MINOR
    n1 = seq_len // n2
    w_bd, c2, s2, tc, ts = _dft_tables(seq_len)
    x = u_f.reshape(batch, n1, n2, BRANCH_WIDTH).transpose(0, 2, 1, 3)
    x = x.reshape(batch * seq_len // n2, n2, BRANCH_WIDTH)
    z = _dft_stage1(x, jnp.asarray(w_bd).astype(BF16))
    z = z.reshape(batch, n2, 2, n1, BRANCH_WIDTH).transpose(0, 3, 2, 1, 4)
    z = z.reshape(batch * n1, 2 * n2, BRANCH_WIDTH)
    q = _dft_stage2(z, n1, jnp.asarray(c2), jnp.asarray(s2), jnp.asarray(tc), jnp.asarray(ts))
    q = q.reshape(batch, n1, 2, n2, BRANCH_WIDTH).transpose(0, 3, 1, 2, 4)
    return q.reshape(batch * seq_len, 2 * BRANCH_WIDTH)


def _mix_out_kernel(x_ref, q_ref, gf_ref, p_ref, mod_ref, gpm_ref, gpf_ref, gqf_ref,
                    cs_ref, wfo_ref, wout_ref, wgate_ref, wup_ref, wdown_ref, o_ref):
    mod = mod_ref[0]
    cs = cs_ref[...]
    gate1 = mod[2:3, :] * gpm_ref[...]
    scale2 = gpf_ref[...] * (1.0 + mod[4:5, :])
    shift2 = mod[3:4, :]
    gate2 = mod[5:6, :] * gqf_ref[...]
    subs = [pl.ds(s * (TOKEN_TILE // SUB_TILES), TOKEN_TILE // SUB_TILES)
            for s in range(SUB_TILES)]

    def channel_dft(rows):
        q = q_ref[rows, :]
        ys = []
        for g in range(N_GROUPS):
            qg = jnp.concatenate(
                [q[:, g * GROUP_DIM:(g + 1) * GROUP_DIM],
                 q[:, BRANCH_WIDTH + g * GROUP_DIM:BRANCH_WIDTH + (g + 1) * GROUP_DIM]],
                axis=1)
            ys.append(_dot(qg, cs))
        return jnp.concatenate(ys, axis=1).astype(BF16)

    def up_proj(h2, chunk):
        cols = pl.ds(*chunk)
        return _dot(h2, wgate_ref[:, cols]), _dot(h2, wup_ref[:, cols])

    def down_proj(ab, chunk):
        a, b = ab
        return _dot((a * _sigmoid(a) * b).astype(BF16), wdown_ref[pl.ds(*chunk), :])

    ys = [channel_dft(rows) for rows in subs]
    y_fs = [_dot(y, wfo_ref[...]) for y in ys]
    ms = [_dot((gf_ref[rows, :].astype(F32) * y_f + p_ref[rows, :].astype(F32)).astype(BF16),
               wout_ref[...]) for rows, y_f in zip(subs, y_fs)]
    x1s = [x_ref[rows, :] + gate1 * (m * _rms_scale(m)) for rows, m in zip(subs, ms)]
    h2s = [(x1 * _rms_scale(x1) * scale2 + shift2).astype(BF16) for x1 in x1s]

    zs = [None] * SUB_TILES
    pending = [up_proj(h2, FF_CHUNKS[0]) for h2 in h2s]
    for c, chunk in enumerate(FF_CHUNKS):
        nxt = ([up_proj(h2, FF_CHUNKS[c + 1]) for h2 in h2s]
               if c + 1 < len(FF_CHUNKS) else None)
        for s in range(SUB_TILES):
            part = down_proj(pending[s], chunk)
            zs[s] = part if zs[s] is None else zs[s] + part
        pending = nxt
    for rows, x1, z in zip(subs, x1s, zs):
        o_ref[rows, :] = x1 + gate2 * (z * _rms_scale(z))


def _mix_out(x2d, q, g_f, p, mods, seq_len, g_post_mix, g_pre_ffn, g_post_ffn,
             cs, w_fo, w_out, w_gate, w_up, w_down):
    n_tok = x2d.shape[0]
    t = TOKEN_TILE
    tiles_per_seq = seq_len // t
    row = lambda i: (i, 0)
    return pl.pallas_call(
        _mix_out_kernel,
        grid=(n_tok // t,),
        in_specs=[
            pl.BlockSpec((t, D_MODEL), row),
            pl.BlockSpec((t, 2 * BRANCH_WIDTH), row),
            pl.BlockSpec((t, D_MODEL), row),
            pl.BlockSpec((t, D_MODEL), row),
            pl.BlockSpec((1, N_ADA, D_MODEL), lambda i: (i // tiles_per_seq, 0, 0)),
            _const_spec((1, D_MODEL)),
            _const_spec((1, D_MODEL)),
            _const_spec((1, D_MODEL)),
            _const_spec((2 * GROUP_DIM, GROUP_DIM)),
            _const_spec((BRANCH_WIDTH, D_MODEL)),
            _const_spec((D_MODEL, D_MODEL)),
            _const_spec((D_MODEL, D_FF)),
            _const_spec((D_MODEL, D_FF)),
            _const_spec((D_FF, D_MODEL)),
        ],
        out_specs=pl.BlockSpec((t, D_MODEL), row),
        out_shape=jax.ShapeDtypeStruct((n_tok, D_MODEL), F32),
        compiler_params=pltpu.CompilerParams(
            dimension_semantics=("arbitrary",), vmem_limit_bytes=VMEM_LIMIT_BYTES),
        name="mix_out",
    )(x2d, q, g_f, p, mods, g_post_mix, g_pre_ffn, g_post_ffn,
      cs, w_fo, w_out, w_gate, w_up, w_down)


def _layer(x, mods, weights):
    (g_pre_mix, w_f, w_p, w_g, wpg_bd, pool_scale, w_po, g_post_mix, g_pre_ffn,
     g_post_ffn, w_fo, w_out, w_gate, w_up, w_down) = weights
    batch, seq_len, d = x.shape
    x2d = x.reshape(batch * seq_len, d)
    u_f, g_f, p = _mix_in(x2d, mods, seq_len, g_pre_mix, w_f, w_p, w_g, wpg_bd,
                          pool_scale, w_po)
    q = _sequence_dft(u_f, batch, seq_len)
    cs = jnp.asarray(_channel_dft_table(seq_len)).astype(BF16)
    y = _mix_out(x2d, q, g_f, p, mods, seq_len, g_post_mix, g_pre_ffn, g_post_ffn,
                 cs, w_fo, w_out, w_gate, w_up, w_down)
    return y.reshape(batch, seq_len, d)


def kernel(x_prompt, x_sample, c_prompt, c_sample, w_ada, b_ada, g_pre_mix, w_in, w_fo, w_pg,
           pool_scale, w_po, w_out, g_post_mix, g_pre_ffn, w_gate, w_up, w_down, g_post_ffn):
    depth = w_ada.shape[0]
    y_prompt, y_sample = x_prompt, x_sample
    bp, bs = c_prompt.shape[0], c_sample.shape[0]
    pad_rows = -(bp + bs) % BF16_ROWS
    c_pad = jnp.concatenate(
        [c_prompt, c_sample, jnp.zeros((pad_rows, D_MODEL), F32)], axis=0)
    for l in range(depth):
        ada = _ada(c_pad, w_ada[l], b_ada[l][None, :])
        mods_p = ada[:bp].reshape(bp, N_ADA, D_MODEL)
        mods_s = ada[bp:bp + bs].reshape(bs, N_ADA, D_MODEL)
        w_in_l = w_in[l].astype(BF16)
        zero_blk = jnp.zeros((GROUP_DIM, GROUP_DIM), F32)
        wpg_bd = jnp.stack([
            jnp.block([[w_pg[l, 2 * pr], zero_blk], [zero_blk, w_pg[l, 2 * pr + 1]]])
            for pr in range(N_GROUPS // 2)]).astype(BF16)
        weights = (
            g_pre_mix[l][None, :],
            w_in_l[:, :BRANCH_WIDTH],
            w_in_l[:, BRANCH_WIDTH:2 * BRANCH_WIDTH],
            w_in_l[:, 2 * BRANCH_WIDTH:],
            wpg_bd,
            pool_scale[l][None, :],
            w_po[l].astype(BF16),
            g_post_mix[l][None, :],
            g_pre_ffn[l][None, :],
            g_post_ffn[l][None, :],
            w_fo[l].astype(BF16),
            w_out[l].astype(BF16),
            w_gate[l].astype(BF16),
            w_up[l].astype(BF16),
            w_down[l].astype(BF16),
        )
        y_prompt = _layer(y_prompt, mods_p, weights)
        y_sample = _layer(y_sample, mods_s, weights)
    return (y_prompt, y_sample)
```

```python
import functools

import numpy as np
import jax
import jax.numpy as jnp
from jax import lax
from jax.experimental import pallas as pl
from jax.experimental.pallas import tpu as pltpu

D_MODEL = 1024
N_GROUPS = 4
GROUP_DIM = 128
BRANCH_WIDTH = N_GROUPS * GROUP_DIM
POOL_WINDOWS = (2, 4, 8, 16)
D_FF = 2816
N_ADA = 6
EPS = 1e-6

LANES = 128
BF16_ROWS = 16
MXU_DIM = 256
VMEM_LIMIT_BYTES = 56 * 1024 * 1024

SEQ_MINOR = 128
HALO = BF16_ROWS
SUB_TILE = 256
MIX_IN_TILE = 1024
MIX_OUT_TILE = 512
ROW_BLOCKS_PER_STEP = 16
ADA_COLS = 512
FF_CHUNKS = ((0, 1024), (1024, 1024), (2048, 768))

BF16 = jnp.bfloat16
F32 = jnp.float32


def _dot(a, b):
    return jnp.dot(a, b, preferred_element_type=F32)


def _sigmoid(v):
    return 1.0 / (1.0 + jnp.exp(-v))


def _rms_scale(v):
    return lax.rsqrt(jnp.mean(v * v, axis=-1, keepdims=True) + EPS)


def _ada_kernel(c_ref, w_ref, b_ref, o_ref):
    c = c_ref[...]
    s = (c * _sigmoid(c)).astype(BF16)
    o_ref[...] = _dot(s, w_ref[...].astype(BF16)) + b_ref[...]


def _ada(c_pad, w_ada, b_ada):
    rows, d = c_pad.shape
    n_out = w_ada.shape[1]
    return pl.pallas_call(
        _ada_kernel,
        grid=(n_out // ADA_COLS,),
        in_specs=[
            pl.BlockSpec((rows, d), lambda j: (0, 0)),
            pl.BlockSpec((d, ADA_COLS), lambda j: (0, j)),
            pl.BlockSpec((1, ADA_COLS), lambda j: (0, j)),
        ],
        out_specs=pl.BlockSpec((rows, ADA_COLS), lambda j: (0, j)),
        out_shape=jax.ShapeDtypeStruct((rows, n_out), F32),
        compiler_params=pltpu.CompilerParams(
            dimension_semantics=("arbitrary",), vmem_limit_bytes=VMEM_LIMIT_BYTES),
        name="ada",
    )(c_pad, w_ada, b_ada)


def _window_sum(ext, w):
    n = ext.shape[0]
    s = ext
    k = 1
    while k < w // 2:
        s = s + pltpu.roll(s, n - k, axis=0)
        k *= 2
    return s + pltpu.roll(s, w // 2, axis=0)


def _mix_in_kernel(tiles_per_seq, seq_len,
                   x_ref, xp_ref, xn_ref, mod_ref, g_ref, wf_ref, wp_ref, wg_ref,
                   wpg_ref, ps_ref, wpo_ref,
                   uf_ref, gf_ref, p_ref):
    t = MIX_IN_TILE
    n_sub = t // SUB_TILE
    ti = lax.rem(pl.program_id(0), tiles_per_seq)
    mod = mod_ref[0]
    scale = g_ref[...] * (1.0 + mod[1:2, :])
    shift = mod[0:1, :]

    def normed(v):
        return v * _rms_scale(v) * scale + shift

    ts = SUB_TILE
    subs = [pl.ds(s * ts, ts) for s in range(n_sub)]
    hs = [normed(x_ref[rows, :]).astype(BF16) for rows in subs]
    h_prev = jnp.where(ti == 0, 0.0, normed(xp_ref[...])).astype(BF16)
    h_next = jnp.where(ti == tiles_per_seq - 1, 0.0, normed(xn_ref[...])).astype(BF16)
    befores = [h_prev] + [h[ts - HALO:] for h in hs[:-1]]
    afters = [h[:HALO] for h in hs[1:]] + [h_next]

    ups, gates = [], []
    for s in range(n_sub):
        ups.append(_dot(jnp.concatenate([befores[s], hs[s], afters[s]], axis=0), wp_ref[...]))
        uf_ref[subs[s], :] = _dot(hs[s], wf_ref[...]).astype(uf_ref.dtype)
        gates.append(_dot(hs[s], wg_ref[...]))

    slab = 8
    row = lax.broadcasted_iota(jnp.int32, (slab, GROUP_DIM), 0)

    def pool_diff(up, first_pos):
        diffs = []
        for g, w in enumerate(POOL_WINDOWS):
            assert w & (w - 1) == 0 and w // 2 <= slab
            left = w // 2
            right = w - 1 - left
            ext = up[:, g * GROUP_DIM:(g + 1) * GROUP_DIM]
            win = _window_sum(ext, w)[HALO:HALO + ts]
            u = ext[HALO:HALO + ts]

            def count(pos):
                return (jnp.minimum(pos + right, seq_len - 1)
                        - jnp.maximum(pos - left, 0) + 1).astype(F32)

            diffs.append(jnp.concatenate([
                win[:slab] / count(first_pos + row) - u[:slab],
                win[slab:ts - slab] * (1.0 / w) - u[slab:ts - slab],
                win[ts - slab:] / count(first_pos + (ts - slab) + row) - u[ts - slab:],
            ], axis=0))
        return diffs

    for s in range(n_sub):
        diffs = pool_diff(ups[s], ti * t + s * ts)
        ys = []
        for pair in range(N_GROUPS // 2):
            d2 = jnp.concatenate(diffs[2 * pair:2 * pair + 2], axis=1).astype(BF16)
            ys.append(_dot(d2, wpg_ref[pair]))
        y = (jnp.concatenate(ys, axis=1) * ps_ref[...]).astype(BF16)
        g = _sigmoid(gates[s])
        gf_ref[subs[s], :] = g[:, :D_MODEL].astype(gf_ref.dtype)
        p_ref[subs[s], :] = (g[:, D_MODEL:] * _dot(y, wpo_ref[...])).astype(p_ref.dtype)


def _const_spec(shape):
    zeros = (0,) * len(shape)
    return pl.BlockSpec(shape, lambda i: zeros, pipeline_mode=pl.Buffered(1))


def _mix_in(x2d, mods, seq_len, g_pre, w_f, w_p, w_g, wpg_bd, pool_scale, w_po):
    n_tok = x2d.shape[0]
    t = MIX_IN_TILE
    tiles_per_seq = seq_len // t
    halo_blocks_per_tile = t // HALO
    n_halo_blocks = n_tok // HALO
    row = lambda i: (i, 0)
    return pl.pallas_call(
        functools.partial(_mix_in_kernel, tiles_per_seq, seq_len),
        grid=(n_tok // t,),
        in_specs=[
            pl.BlockSpec((t, D_MODEL), row),
            pl.BlockSpec((HALO, D_MODEL),
                         lambda i: (jnp.maximum(i * halo_blocks_per_tile - 1, 0), 0)),
            pl.BlockSpec((HALO, D_MODEL),
                         lambda i: (jnp.minimum((i + 1) * halo_blocks_per_tile,
                                                n_halo_blocks - 1), 0)),
            pl.BlockSpec((1, N_ADA, D_MODEL), lambda i: (i // tiles_per_seq, 0, 0)),
            _const_spec((1, D_MODEL)),
            _const_spec((D_MODEL, BRANCH_WIDTH)),
            _const_spec((D_MODEL, BRANCH_WIDTH)),
            _const_spec((D_MODEL, 2 * D_MODEL)),
            _const_spec((N_GROUPS // 2, MXU_DIM, MXU_DIM)),
            _const_spec((1, BRANCH_WIDTH)),
            _const_spec((BRANCH_WIDTH, D_MODEL)),
        ],
        out_specs=[
            pl.BlockSpec((t, BRANCH_WIDTH), row),
            pl.BlockSpec((t, D_MODEL), row),
            pl.BlockSpec((t, D_MODEL), row),
        ],
        out_shape=[
            jax.ShapeDtypeStruct((n_tok, BRANCH_WIDTH), BF16),
            jax.ShapeDtypeStruct((n_tok, D_MODEL), BF16),
            jax.ShapeDtypeStruct((n_tok, D_MODEL), BF16),
        ],
        compiler_params=pltpu.CompilerParams(
            dimension_semantics=("arbitrary",), vmem_limit_bytes=VMEM_LIMIT_BYTES),
        name="mix_in",
    )(x2d, x2d, x2d, mods, g_pre, w_f, w_p, w_g, wpg_bd, pool_scale, w_po)


def _dft_stage1_kernel(x_ref, w_ref, o_ref):
    w = w_ref[...]
    for j in range(ROW_BLOCKS_PER_STEP):
        o_ref[j] = _dot(w, x_ref[j]).astype(o_ref.dtype)


def _dft_stage1(x_blocks, w_bd):
    n_blocks = x_blocks.shape[0]
    rb = ROW_BLOCKS_PER_STEP
    return pl.pallas_call(
        _dft_stage1_kernel,
        grid=(n_blocks // rb,),
        in_specs=[
            pl.BlockSpec((rb, SEQ_MINOR, BRANCH_WIDTH), lambda i: (i, 0, 0)),
            _const_spec((2 * SEQ_MINOR, SEQ_MINOR)),
        ],
        out_specs=pl.BlockSpec((rb, 2 * SEQ_MINOR, BRANCH_WIDTH), lambda i: (i, 0, 0)),
        out_shape=jax.ShapeDtypeStruct((n_blocks, 2 * SEQ_MINOR, BRANCH_WIDTH), BF16),
        compiler_params=pltpu.CompilerParams(
            dimension_semantics=("arbitrary",), vmem_limit_bytes=VMEM_LIMIT_BYTES),
        name="dft_stage1",
    )(x_blocks, w_bd)


def _dft_stage2_kernel(n1, z_ref, c2_ref, s2_ref, tc_ref, ts_ref, cs_ref, o_ref):
    c2 = c2_ref[...]
    s2 = s2_ref[...]
    cs = cs_ref[...]
    base = pl.program_id(0) * ROW_BLOCKS_PER_STEP
    for j in range(ROW_BLOCKS_PER_STEP):
        k1 = lax.rem(base + j, n1)
        tc = tc_ref[pl.ds(k1, 1), :]
        ts = ts_ref[pl.ds(k1, 1), :]
        cos_t = c2 * tc - s2 * ts
        sin_t = s2 * tc + c2 * ts
        lhs = jnp.concatenate(
            [jnp.concatenate([cos_t, sin_t], axis=1),
             jnp.concatenate([-sin_t, cos_t], axis=1)], axis=0).astype(BF16)
        q = _dot(lhs, z_ref[j]).astype(BF16)
        ys = []
        for g in range(N_GROUPS):
            lanes = slice(g * GROUP_DIM, (g + 1) * GROUP_DIM)
            qg = jnp.concatenate([q[:SEQ_MINOR, lanes], q[SEQ_MINOR:, lanes]], axis=1)
            ys.append(_dot(qg, cs))
        o_ref[j] = jnp.concatenate(ys, axis=1).astype(o_ref.dtype)


def _dft_stage2(z_blocks, n1, c2, s2, tc, ts, cs):
    n_blocks = z_blocks.shape[0]
    rb = ROW_BLOCKS_PER_STEP
    return pl.pallas_call(
        functools.partial(_dft_stage2_kernel, n1),
        grid=(n_blocks // rb,),
        in_specs=[
            pl.BlockSpec((rb, 2 * SEQ_MINOR, BRANCH_WIDTH), lambda i: (i, 0, 0)),
            _const_spec((SEQ_MINOR, SEQ_MINOR)),
            _const_spec((SEQ_MINOR, SEQ_MINOR)),
            _const_spec((n1, SEQ_MINOR)),
            _const_spec((n1, SEQ_MINOR)),
            _const_spec((2 * GROUP_DIM, GROUP_DIM)),
        ],
        out_specs=pl.BlockSpec((rb, SEQ_MINOR, BRANCH_WIDTH), lambda i: (i, 0, 0)),
        out_shape=jax.ShapeDtypeStruct((n_blocks, SEQ_MINOR, BRANCH_WIDTH), BF16),
        compiler_params=pltpu.CompilerParams(
            dimension_semantics=("arbitrary",), vmem_limit_bytes=VMEM_LIMIT_BYTES),
        name="dft_stage2",
    )(z_blocks, c2, s2, tc, ts, cs)


@functools.lru_cache(maxsize=None)
def _dft_tables(seq_len):
    n2 = SEQ_MINOR
    n1 = seq_len // n2
    reps = n2 // n1
    k = np.arange(n1, dtype=np.float64)
    ang1 = 2.0 * np.pi * np.outer(k, k) / n1
    w1 = np.concatenate([np.cos(ang1), -np.sin(ang1)], axis=0)
    w_bd = np.kron(np.eye(reps), w1)
    m = np.arange(n2, dtype=np.float64)
    ang2 = 2.0 * np.pi * np.outer(m, m) / n2
    angt = 2.0 * np.pi * np.outer(k, m) / seq_len
    as32 = lambda a: np.asarray(a, dtype=np.float32)
    return (as32(w_bd), as32(np.cos(ang2)), as32(np.sin(ang2)),
            as32(np.cos(angt)), as32(np.sin(angt)))


@functools.lru_cache(maxsize=None)
def _channel_dft_table(seq_len):
    c = np.arange(GROUP_DIM, dtype=np.float64)
    ang = 2.0 * np.pi * np.outer(c, c) / GROUP_DIM
    norm = 1.0 / np.sqrt(float(seq_len) * GROUP_DIM)
    return np.asarray(np.concatenate([np.cos(ang), np.sin(ang)], axis=0) * norm,
                      dtype=np.float32)


def _fourier_mix(u_f, batch, seq_len):
    n2 = SEQ_MINOR
    n1 = seq_len // n2
    w_bd, c2, s2, tc, ts = _dft_tables(seq_len)
    cs = jnp.asarray(_channel_dft_table(seq_len)).astype(BF16)
    x = u_f.reshape(batch, n1, n2, BRANCH_WIDTH).transpose(0, 2, 1, 3)
    x = x.reshape(batch * seq_len // n2, n2, BRANCH_WIDTH)
    z = _dft_stage1(x, jnp.asarray(w_bd).astype(BF16))
    z = z.reshape(batch, n2, 2, n1, BRANCH_WIDTH).transpose(0, 3, 2, 1, 4)
    z = z.reshape(batch * n1, 2 * n2, BRANCH_WIDTH)
    y = _dft_stage2(z, n1, jnp.asarray(c2), jnp.asarray(s2), jnp.asarray(tc), jnp.asarray(ts),
                    cs)
    y = y.reshape(batch, n1, n2, BRANCH_WIDTH).transpose(0, 2, 1, 3)
    return y.reshape(batch * seq_len, BRANCH_WIDTH)


def _mix_out_kernel(x_ref, y_ref, gf_ref, p_ref, mod_ref, gpm_ref, gpf_ref, gqf_ref,
                    wfo_ref, wout_ref, wgate_ref, wup_ref, wdown_ref, o_ref):
    mod = mod_ref[0]
    n_sub = MIX_OUT_TILE // SUB_TILE
    gate1 = mod[2:3, :] * gpm_ref[...]
    scale2 = gpf_ref[...] * (1.0 + mod[4:5, :])
    shift2 = mod[3:4, :]
    gate2 = mod[5:6, :] * gqf_ref[...]
    subs = [pl.ds(s * SUB_TILE, SUB_TILE) for s in range(n_sub)]

    def up_proj(h2, chunk):
        cols = pl.ds(*chunk)
        return _dot(h2, wgate_ref[:, cols]), _dot(h2, wup_ref[:, cols])

    def down_proj(ab, chunk):
        a, b = ab
        return _dot((a * _sigmoid(a) * b).astype(BF16), wdown_ref[pl.ds(*chunk), :])

    y_fs = [_dot(y_ref[rows, :], wfo_ref[...]) for rows in subs]
    ms = [_dot((gf_ref[rows, :].astype(F32) * y_f + p_ref[rows, :].astype(F32)).astype(BF16),
               wout_ref[...]) for rows, y_f in zip(subs, y_fs)]
    x1s = [x_ref[rows, :] + gate1 * (m * _rms_scale(m)) for rows, m in zip(subs, ms)]
    h2s = [(x1 * _rms_scale(x1) * scale2 + shift2).astype(BF16) for x1 in x1s]

    zs = [None] * n_sub
    pending = [up_proj(h2, FF_CHUNKS[0]) for h2 in h2s]
    for c, chunk in enumerate(FF_CHUNKS):
        nxt = ([up_proj(h2, FF_CHUNKS[c + 1]) for h2 in h2s]
               if c + 1 < len(FF_CHUNKS) else None)
        for s in range(n_sub):
            part = down_proj(pending[s], chunk)
            zs[s] = part if zs[s] is None else zs[s] + part
        pending = nxt
    for rows, x1, z in zip(subs, x1s, zs):
        o_ref[rows, :] = x1 + gate2 * (z * _rms_scale(z))


def _mix_out(x2d, y, g_f, p, mods, seq_len, g_post_mix, g_pre_ffn, g_post_ffn,
             w_fo, w_out, w_gate, w_up, w_down):
    n_tok = x2d.shape[0]
    t = MIX_OUT_TILE
    tiles_per_seq = seq_len // t
    row = lambda i: (i, 0)
    return pl.pallas_call(
        _mix_out_kernel,
        grid=(n_tok // t,),
        in_specs=[
            pl.BlockSpec((t, D_MODEL), row),
            pl.BlockSpec((t, BRANCH_WIDTH), row),
            pl.BlockSpec((t, D_MODEL), row),
            pl.BlockSpec((t, D_MODEL), row),
            pl.BlockSpec((1, N_ADA, D_MODEL), lambda i: (i // tiles_per_seq, 0, 0)),
            _const_spec((1, D_MODEL)),
            _const_spec((1, D_MODEL)),
            _const_spec((1, D_MODEL)),
            _const_spec((BRANCH_WIDTH, D_MODEL)),
            _const_spec((D_MODEL, D_MODEL)),
            _const_spec((D_MODEL, D_FF)),
            _const_spec((D_MODEL, D_FF)),
            _const_spec((D_FF, D_MODEL)),
        ],
        out_specs=pl.BlockSpec((t, D_MODEL), row),
        out_shape=jax.ShapeDtypeStruct((n_tok, D_MODEL), F32),
        compiler_params=pltpu.CompilerParams(
            dimension_semantics=("arbitrary",), vmem_limit_bytes=VMEM_LIMIT_BYTES),
        name="mix_out",
    )(x2d, y, g_f, p, mods, g_post_mix, g_pre_ffn, g_post_ffn,
      w_fo, w_out, w_gate, w_up, w_down)


def _layer(x, mods, weights):
    (g_pre_mix, w_f, w_p, w_g, wpg_bd, pool_scale, w_po, g_post_mix, g_pre_ffn,
     g_post_ffn, w_fo, w_out, w_gate, w_up, w_down) = weights
    batch, seq_len, d = x.shape
    x2d = x.reshape(batch * seq_len, d)
    u_f, g_f, p = _mix_in(x2d, mods, seq_len, g_pre_mix, w_f, w_p, w_g, wpg_bd,
                          pool_scale, w_po)
    y_f = _fourier_mix(u_f, batch, seq_len)
    out = _mix_out(x2d, y_f, g_f, p, mods, seq_len, g_post_mix, g_pre_ffn, g_post_ffn,
                   w_fo, w_out, w_gate, w_up, w_down)
    return out.reshape(batch, seq_len, d)


def kernel(x_prompt, x_sample, c_prompt, c_sample, w_ada, b_ada, g_pre_mix, w_in, w_fo, w_pg,
           pool_scale, w_po, w_out, g_post_mix, g_pre_ffn, w_gate, w_up, w_down, g_post_ffn):
    depth = w_ada.shape[0]
    y_prompt, y_sample = x_prompt, x_sample
    bp, bs = c_prompt.shape[0], c_sample.shape[0]
    pad_rows = -(bp + bs) % BF16_ROWS
    c_pad = jnp.concatenate(
        [c_prompt, c_sample, jnp.zeros((pad_rows, D_MODEL), F32)], axis=0)
    for l in range(depth):
        ada = _ada(c_pad, w_ada[l], b_ada[l][None, :])
        mods_p = ada[:bp].reshape(bp, N_ADA, D_MODEL)
        mods_s = ada[bp:bp + bs].reshape(bs, N_ADA, D_MODEL)
        w_in_l = w_in[l].astype(BF16)
        zero_blk = jnp.zeros((GROUP_DIM, GROUP_DIM), F32)
        wpg_bd = jnp.stack([
            jnp.block([[w_pg[l, 2 * pr], zero_blk], [zero_blk, w_pg[l, 2 * pr + 1]]])
            for pr in range(N_GROUPS // 2)]).astype(BF16)
        weights = (
            g_pre_mix[l][None, :],
            w_in_l[:, :BRANCH_WIDTH],
            w_in_l[:, BRANCH_WIDTH:2 * BRANCH_WIDTH],
            w_in_l[:, 2 * BRANCH_WIDTH:],
            wpg_bd,
            pool_scale[l][None, :],
            w_po[l].astype(BF16),
            g_post_mix[l][None, :],
            g_pre_ffn[l][None, :],
            g_post_ffn[l][None, :],
            w_fo[l].astype(BF16),
            w_out[l].astype(BF16),
            w_gate[l].astype(BF16),
            w_up[l].astype(BF16),
            w_down[l].astype(BF16),
        )
        y_prompt = _layer(y_prompt, mods_p, weights)
        y_sample = _layer(y_sample, mods_s, weights)
    return (y_prompt, y_sample)
```

```python
import functools

import numpy as np
import jax
import jax.numpy as jnp
from jax import lax
from jax.experimental import pallas as pl
from jax.experimental.pallas import tpu as pltpu

D_MODEL = 1024
N_GROUPS = 4
GROUP_DIM = 128
BRANCH_WIDTH = N_GROUPS * GROUP_DIM
POOL_WINDOWS = (2, 4, 8, 16)
D_FF = 2816
N_ADA = 6
EPS = 1e-6

LANES = 128
BF16_ROWS = 16
MXU_DIM = 256
VMEM_LIMIT_BYTES = 56 * 1024 * 1024

SEQ_MINOR = 128
HALO = BF16_ROWS
SUB_TILE = 256
MIX_IN_TILE = 1024
MIX_OUT_TILE = 512
K1_PER_TILE = BF16_ROWS
K2_PER_TILE = MIX_OUT_TILE // K1_PER_TILE
ROW_BLOCKS_PER_STEP = 16
ADA_COLS = 512
FF_CHUNKS = ((0, 1024), (1024, 1024), (2048, 768))

BF16 = jnp.bfloat16
F32 = jnp.float32


def _dot(a, b):
    return jnp.dot(a, b, preferred_element_type=F32)


def _sigmoid(v):
    return 1.0 / (1.0 + jnp.exp(-v))


def _rms_scale(v):
    return lax.rsqrt(jnp.mean(v * v, axis=-1, keepdims=True) + EPS)


def _ada_kernel(c_ref, w_ref, b_ref, o_ref):
    c = c_ref[...]
    s = (c * _sigmoid(c)).astype(BF16)
    o_ref[...] = _dot(s, w_ref[...].astype(BF16)) + b_ref[...]


def _ada(c_pad, w_ada, b_ada):
    rows, d = c_pad.shape
    n_out = w_ada.shape[1]
    return pl.pallas_call(
        _ada_kernel,
        grid=(n_out // ADA_COLS,),
        in_specs=[
            pl.BlockSpec((rows, d), lambda j: (0, 0)),
            pl.BlockSpec((d, ADA_COLS), lambda j: (0, j)),
            pl.BlockSpec((1, ADA_COLS), lambda j: (0, j)),
        ],
        out_specs=pl.BlockSpec((rows, ADA_COLS), lambda j: (0, j)),
        out_shape=jax.ShapeDtypeStruct((rows, n_out), F32),
        compiler_params=pltpu.CompilerParams(
            dimension_semantics=("arbitrary",), vmem_limit_bytes=VMEM_LIMIT_BYTES),
        name="ada",
    )(c_pad, w_ada, b_ada)


def _window_sum(ext, w):
    n = ext.shape[0]
    s = ext
    k = 1
    while k < w // 2:
        s = s + pltpu.roll(s, n - k, axis=0)
        k *= 2
    return s + pltpu.roll(s, w // 2, axis=0)


def _mix_in_kernel(tiles_per_seq, seq_len,
                   x_ref, xp_ref, xn_ref, mod_ref, g_ref, wf_ref, wp_ref, wg_ref,
                   wpg_ref, ps_ref, wpo_ref,
                   uf_ref, gf_ref, p_ref):
    t = MIX_IN_TILE
    n_sub = t // SUB_TILE
    ti = lax.rem(pl.program_id(0), tiles_per_seq)
    mod = mod_ref[0]
    scale = g_ref[...] * (1.0 + mod[1:2, :])
    shift = mod[0:1, :]

    def normed(v):
        return v * _rms_scale(v) * scale + shift

    ts = SUB_TILE
    subs = [pl.ds(s * ts, ts) for s in range(n_sub)]
    hs = [normed(x_ref[rows, :]).astype(BF16) for rows in subs]
    h_prev = jnp.where(ti == 0, 0.0, normed(xp_ref[...])).astype(BF16)
    h_next = jnp.where(ti == tiles_per_seq - 1, 0.0, normed(xn_ref[...])).astype(BF16)
    befores = [h_prev] + [h[ts - HALO:] for h in hs[:-1]]
    afters = [h[:HALO] for h in hs[1:]] + [h_next]

    ups, gates = [], []
    for s in range(n_sub):
        ups.append(_dot(jnp.concatenate([befores[s], hs[s], afters[s]], axis=0), wp_ref[...]))
        uf_ref[subs[s], :] = _dot(hs[s], wf_ref[...]).astype(uf_ref.dtype)
        gates.append(_dot(hs[s], wg_ref[...]))

    slab = 8
    row = lax.broadcasted_iota(jnp.int32, (slab, GROUP_DIM), 0)

    def pool_diff(up, first_pos):
        diffs = []
        for g, w in enumerate(POOL_WINDOWS):
            assert w & (w - 1) == 0 and w // 2 <= slab
            left = w // 2
            right = w - 1 - left
            ext = up[:, g * GROUP_DIM:(g + 1) * GROUP_DIM]
            win = _window_sum(ext, w)[HALO:HALO + ts]
            u = ext[HALO:HALO + ts]

            def count(pos):
                return (jnp.minimum(pos + right, seq_len - 1)
                        - jnp.maximum(pos - left, 0) + 1).astype(F32)

            diffs.append(jnp.concatenate([
                win[:slab] / count(first_pos + row) - u[:slab],
                win[slab:ts - slab] * (1.0 / w) - u[slab:ts - slab],
                win[ts - slab:] / count(first_pos + (ts - slab) + row) - u[ts - slab:],
            ], axis=0))
        return diffs

    for s in range(n_sub):
        diffs = pool_diff(ups[s], ti * t + s * ts)
        ys = []
        for pair in range(N_GROUPS // 2):
            d2 = jnp.concatenate(diffs[2 * pair:2 * pair + 2], axis=1).astype(BF16)
            ys.append(_dot(d2, wpg_ref[pair]))
        y = (jnp.concatenate(ys, axis=1) * ps_ref[...]).astype(BF16)
        g = _sigmoid(gates[s])
        gf_ref[subs[s], :] = g[:, :D_MODEL].astype(gf_ref.dtype)
        p_ref[subs[s], :] = (g[:, D_MODEL:] * _dot(y, wpo_ref[...])).astype(p_ref.dtype)


def _const_spec(shape):
    zeros = (0,) * len(shape)
    return pl.BlockSpec(shape, lambda *_: zeros, pipeline_mode=pl.Buffered(1))


def _mix_in(x2d, mods, seq_len, g_pre, w_f, w_p, w_g, wpg_bd, pool_scale, w_po):
    n_tok = x2d.shape[0]
    t = MIX_IN_TILE
    tiles_per_seq = seq_len // t
    halo_blocks_per_tile = t // HALO
    n_halo_blocks = n_tok // HALO
    row = lambda i: (i, 0)
    return pl.pallas_call(
        functools.partial(_mix_in_kernel, tiles_per_seq, seq_len),
        grid=(n_tok // t,),
        in_specs=[
            pl.BlockSpec((t, D_MODEL), row),
            pl.BlockSpec((HALO, D_MODEL),
                         lambda i: (jnp.maximum(i * halo_blocks_per_tile - 1, 0), 0)),
            pl.BlockSpec((HALO, D_MODEL),
                         lambda i: (jnp.minimum((i + 1) * halo_blocks_per_tile,
                                                n_halo_blocks - 1), 0)),
            pl.BlockSpec((1, N_ADA, D_MODEL), lambda i: (i // tiles_per_seq, 0, 0)),
            _const_spec((1, D_MODEL)),
            _const_spec((D_MODEL, BRANCH_WIDTH)),
            _const_spec((D_MODEL, BRANCH_WIDTH)),
            _const_spec((D_MODEL, 2 * D_MODEL)),
            _const_spec((N_GROUPS // 2, MXU_DIM, MXU_DIM)),
            _const_spec((1, BRANCH_WIDTH)),
            _const_spec((BRANCH_WIDTH, D_MODEL)),
        ],
        out_specs=[
            pl.BlockSpec((t, BRANCH_WIDTH), row),
            pl.BlockSpec((t, D_MODEL), row),
            pl.BlockSpec((t, D_MODEL), row),
        ],
        out_shape=[
            jax.ShapeDtypeStruct((n_tok, BRANCH_WIDTH), BF16),
            jax.ShapeDtypeStruct((n_tok, D_MODEL), BF16),
            jax.ShapeDtypeStruct((n_tok, D_MODEL), BF16),
        ],
        compiler_params=pltpu.CompilerParams(
            dimension_semantics=("arbitrary",), vmem_limit_bytes=VMEM_LIMIT_BYTES),
        name="mix_in",
    )(x2d, x2d, x2d, mods, g_pre, w_f, w_p, w_g, wpg_bd, pool_scale, w_po)


def _dft_stage1_kernel(x_ref, w_ref, o_ref):
    n1, c2, width = x_ref.shape
    w = w_ref[...]
    xt = jnp.swapaxes(x_ref[...], 0, 1).reshape(ROW_BLOCKS_PER_STEP, SEQ_MINOR, width)
    z = jnp.stack([_dot(w, xt[g]).astype(BF16) for g in range(ROW_BLOCKS_PER_STEP)])
    o_ref[...] = jnp.swapaxes(z.reshape(c2, 2 * n1, width), 0, 1)


def _dft_stage1(u4, w_bd):
    batch, n1, n2, width = u4.shape
    c2 = ROW_BLOCKS_PER_STEP * SEQ_MINOR // n1
    assert n2 % c2 == 0 and c2 % BF16_ROWS == 0
    return pl.pallas_call(
        _dft_stage1_kernel,
        grid=(batch, n2 // c2),
        in_specs=[
            pl.BlockSpec((None, n1, c2, width), lambda b, c: (b, 0, c, 0)),
            _const_spec((2 * SEQ_MINOR, SEQ_MINOR)),
        ],
        out_specs=pl.BlockSpec((None, 2 * n1, c2, width), lambda b, c: (b, 0, c, 0)),
        out_shape=jax.ShapeDtypeStruct((batch, 2 * n1, n2, width), BF16),
        compiler_params=pltpu.CompilerParams(
            dimension_semantics=("arbitrary", "arbitrary"),
            vmem_limit_bytes=VMEM_LIMIT_BYTES),
        name="dft_stage1",
    )(u4, w_bd)


def _dft_stage2_kernel(zr_ref, zi_ref, c2_ref, s2_ref, tc_ref, ts_ref, cs_ref, o_ref):
    c2 = c2_ref[...]
    s2 = s2_ref[...]
    base = pl.program_id(1) * ROW_BLOCKS_PER_STEP
    qs = []
    for j in range(ROW_BLOCKS_PER_STEP):
        tc = tc_ref[pl.ds(base + j, 1), :]
        ts = ts_ref[pl.ds(base + j, 1), :]
        cos_t = c2 * tc - s2 * ts
        sin_t = s2 * tc + c2 * ts
        lhs = jnp.concatenate(
            [jnp.concatenate([cos_t, sin_t], axis=1),
             jnp.concatenate([-sin_t, cos_t], axis=1)], axis=0).astype(BF16)
        z = jnp.concatenate([zr_ref[j], zi_ref[j]], axis=0)
        qs.append(_dot(lhs, z).astype(BF16))
    halves = []
    for pair in range(N_GROUPS // 2):
        rows = []
        for q in qs:
            parts = []
            for g in (2 * pair, 2 * pair + 1):
                lanes = slice(g * GROUP_DIM, (g + 1) * GROUP_DIM)
                parts += [q[:SEQ_MINOR, lanes], q[SEQ_MINOR:, lanes]]
            rows.append(jnp.concatenate(parts, axis=1))
        halves.append(_dot(jnp.concatenate(rows, axis=0), cs_ref[...]))
    y = jnp.concatenate(halves, axis=1).astype(o_ref.dtype)
    o_ref[...] = y.reshape(o_ref.shape)


def _dft_stage2(z4, c2, s2, tc, ts, cs_bd):
    batch, two_n1, n2, width = z4.shape
    n1 = two_n1 // 2
    rb = ROW_BLOCKS_PER_STEP
    assert n1 % rb == 0 and n2 == SEQ_MINOR
    steps = n1 // rb
    return pl.pallas_call(
        _dft_stage2_kernel,
        grid=(batch, steps),
        in_specs=[
            pl.BlockSpec((None, rb, n2, width), lambda b, i: (b, i, 0, 0)),
            pl.BlockSpec((None, rb, n2, width), lambda b, i: (b, steps + i, 0, 0)),
            _const_spec((SEQ_MINOR, SEQ_MINOR)),
            _const_spec((SEQ_MINOR, SEQ_MINOR)),
            _const_spec((n1, SEQ_MINOR)),
            _const_spec((n1, SEQ_MINOR)),
            _const_spec((2 * MXU_DIM, MXU_DIM)),
        ],
        out_specs=pl.BlockSpec((None, rb, n2, width), lambda b, i: (b, i, 0, 0)),
        out_shape=jax.ShapeDtypeStruct((batch, n1, n2, width), BF16),
        compiler_params=pltpu.CompilerParams(
            dimension_semantics=("arbitrary", "arbitrary"),
            vmem_limit_bytes=VMEM_LIMIT_BYTES),
        name="dft_stage2",
    )(z4, z4, c2, s2, tc, ts, cs_bd)


@functools.lru_cache(maxsize=None)
def _dft_tables(seq_len):
    n2 = SEQ_MINOR
    n1 = seq_len // n2
    reps = n2 // n1
    k = np.arange(n1, dtype=np.float64)
    ang1 = 2.0 * np.pi * np.outer(k, k) / n1
    w1 = np.concatenate([np.cos(ang1), -np.sin(ang1)], axis=0)
    w_bd = np.kron(np.eye(reps), w1)
    m = np.arange(n2, dtype=np.float64)
    ang2 = 2.0 * np.pi * np.outer(m, m) / n2
    angt = 2.0 * np.pi * np.outer(k, m) / seq_len
    as32 = lambda a: np.asarray(a, dtype=np.float32)
    return (as32(w_bd), as32(np.cos(ang2)), as32(np.sin(ang2)),
            as32(np.cos(angt)), as32(np.sin(angt)))


@functools.lru_cache(maxsize=None)
def _channel_dft_table(seq_len):
    c = np.arange(GROUP_DIM, dtype=np.float64)
    ang = 2.0 * np.pi * np.outer(c, c) / GROUP_DIM
    norm = 1.0 / np.sqrt(float(seq_len) * GROUP_DIM)
    cs = np.concatenate([np.cos(ang), np.sin(ang)], axis=0) * norm
    return np.asarray(np.kron(np.eye(2), cs), dtype=np.float32)


def _fourier_mix(u_f, batch, seq_len):
    n2 = SEQ_MINOR
    n1 = seq_len // n2
    w_bd, c2, s2, tc, ts = _dft_tables(seq_len)
    cs_bd = jnp.asarray(_channel_dft_table(seq_len)).astype(BF16)
    z = _dft_stage1(u_f.reshape(batch, n1, n2, BRANCH_WIDTH), jnp.asarray(w_bd).astype(BF16))
    return _dft_stage2(z, jnp.asarray(c2), jnp.asarray(s2), jnp.asarray(tc), jnp.asarray(ts),
                       cs_bd)


def _mix_out_kernel(x_ref, y_ref, gf_ref, p_ref, mod_ref, gpm_ref, gpf_ref, gqf_ref,
                    wfo_ref, wout_ref, wgate_ref, wup_ref, wdown_ref, o_ref):
    mod = mod_ref[0]
    n_sub = MIX_OUT_TILE // SUB_TILE
    gate1 = mod[2:3, :] * gpm_ref[...]
    scale2 = gpf_ref[...] * (1.0 + mod[4:5, :])
    shift2 = mod[3:4, :]
    gate2 = mod[5:6, :] * gqf_ref[...]
    k2_sub = SUB_TILE // K1_PER_TILE
    subs = [pl.ds(s * k2_sub, k2_sub) for s in range(n_sub)]

    def rows_of(ref, k2s):
        return ref[k2s].reshape(SUB_TILE, ref.shape[-1])

    def up_proj(h2, chunk):
        cols = pl.ds(*chunk)
        return _dot(h2, wgate_ref[:, cols]), _dot(h2, wup_ref[:, cols])

    def down_proj(ab, chunk):
        a, b = ab
        return _dot((a * _sigmoid(a) * b).astype(BF16), wdown_ref[pl.ds(*chunk), :])

    ys = [jnp.swapaxes(y_ref[:, k2s, :], 0, 1).reshape(SUB_TILE, BRANCH_WIDTH)
          for k2s in subs]
    y_fs = [_dot(y, wfo_ref[...]) for y in ys]
    ms = [_dot((rows_of(gf_ref, k2s).astype(F32) * y_f
                + rows_of(p_ref, k2s).astype(F32)).astype(BF16), wout_ref[...])
          for k2s, y_f in zip(subs, y_fs)]
    x1s = [rows_of(x_ref, k2s) + gate1 * (m * _rms_scale(m)) for k2s, m in zip(subs, ms)]
    h2s = [(x1 * _rms_scale(x1) * scale2 + shift2).astype(BF16) for x1 in x1s]

    zs = [None] * n_sub
    pending = [up_proj(h2, FF_CHUNKS[0]) for h2 in h2s]
    for c, chunk in enumerate(FF_CHUNKS):
        nxt = ([up_proj(h2, FF_CHUNKS[c + 1]) for h2 in h2s]
               if c + 1 < len(FF_CHUNKS) else None)
        for s in range(n_sub):
            part = down_proj(pending[s], chunk)
            zs[s] = part if zs[s] is None else zs[s] + part
        pending = nxt
    for k2s, x1, z in zip(subs, x1s, zs):
        out = x1 + gate2 * (z * _rms_scale(z))
        o_ref[k2s] = out.reshape(k2_sub, K1_PER_TILE, D_MODEL)


def _mix_out(x, y4, g_f, p, mods, g_post_mix, g_pre_ffn, g_post_ffn,
             w_fo, w_out, w_gate, w_up, w_down):
    batch, seq_len, d = x.shape
    n1 = y4.shape[1]
    n2 = SEQ_MINOR
    k1t, k2t = K1_PER_TILE, K2_PER_TILE
    assert n1 % k1t == 0 and n2 % k2t == 0 and k1t * k2t == MIX_OUT_TILE
    by_k = lambda a: a.reshape(batch, n2, n1 // k1t, k1t, a.shape[-1])
    tok_spec = pl.BlockSpec((None, k2t, None, k1t, d), lambda b, a, t: (b, a, t, 0, 0))
    return pl.pallas_call(
        _mix_out_kernel,
        grid=(batch, n2 // k2t, n1 // k1t),
        in_specs=[
            tok_spec,
            pl.BlockSpec((None, k1t, k2t, BRANCH_WIDTH), lambda b, a, t: (b, t, a, 0)),
            tok_spec,
            tok_spec,
            pl.BlockSpec((1, N_ADA, D_MODEL), lambda b, a, t: (b, 0, 0)),
            _const_spec((1, D_MODEL)),
            _const_spec((1, D_MODEL)),
            _const_spec((1, D_MODEL)),
            _const_spec((BRANCH_WIDTH, D_MODEL)),
            _const_spec((D_MODEL, D_MODEL)),
            _const_spec((D_MODEL, D_FF)),
            _const_spec((D_MODEL, D_FF)),
            _const_spec((D_FF, D_MODEL)),
        ],
        out_specs=tok_spec,
        out_shape=jax.ShapeDtypeStruct((batch, n2, n1 // k1t, k1t, d), F32),
        compiler_params=pltpu.CompilerParams(
            dimension_semantics=("arbitrary", "arbitrary", "arbitrary"),
            vmem_limit_bytes=VMEM_LIMIT_BYTES),
        name="mix_out",
    )(by_k(x), y4, by_k(g_f), by_k(p), mods, g_post_mix, g_pre_ffn, g_post_ffn,
      w_fo, w_out, w_gate, w_up, w_down).reshape(batch, seq_len, d)


def _layer(x, mods, weights):
    (g_pre_mix, w_f, w_p, w_g, wpg_bd, pool_scale, w_po, g_post_mix, g_pre_ffn,
     g_post_ffn, w_fo, w_out, w_gate, w_up, w_down) = weights
    batch, seq_len, d = x.shape
    x2d = x.reshape(batch * seq_len, d)
    u_f, g_f, p = _mix_in(x2d, mods, seq_len, g_pre_mix, w_f, w_p, w_g, wpg_bd,
                          pool_scale, w_po)
    y4 = _fourier_mix(u_f, batch, seq_len)
    return _mix_out(x, y4, g_f, p, mods, g_post_mix, g_pre_ffn, g_post_ffn,
                    w_fo, w_out, w_gate, w_up, w_down)


def kernel(x_prompt, x_sample, c_prompt, c_sample, w_ada, b_ada, g_pre_mix, w_in, w_fo, w_pg,
           pool_scale, w_po, w_out, g_post_mix, g_pre_ffn, w_gate, w_up, w_down, g_post_ffn):
    depth = w_ada.shape[0]
    y_prompt, y_sample = x_prompt, x_sample
    bp, bs = c_prompt.shape[0], c_sample.shape[0]
    pad_rows = -(bp + bs) % BF16_ROWS
    c_pad = jnp.concatenate(
        [c_prompt, c_sample, jnp.zeros((pad_rows, D_MODEL), F32)], axis=0)
    for l in range(depth):
        ada = _ada(c_pad, w_ada[l], b_ada[l][None, :])
        mods_p = ada[:bp].reshape(bp, N_ADA, D_MODEL)
        mods_s = ada[bp:bp + bs].reshape(bs, N_ADA, D_MODEL)
        w_in_l = w_in[l].astype(BF16)
        zero_blk = jnp.zeros((GROUP_DIM, GROUP_DIM), F32)
        wpg_bd = jnp.stack([
            jnp.block([[w_pg[l, 2 * pr], zero_blk], [zero_blk, w_pg[l, 2 * pr + 1]]])
            for pr in range(N_GROUPS // 2)]).astype(BF16)
        weights = (
            g_pre_mix[l][None, :],
            w_in_l[:, :BRANCH_WIDTH],
            w_in_l[:, BRANCH_WIDTH:2 * BRANCH_WIDTH],
            w_in_l[:, 2 * BRANCH_WIDTH:],
            wpg_bd,
            pool_scale[l][None, :],
            w_po[l].astype(BF16),
            g_post_mix[l][None, :],
            g_pre_ffn[l][None, :],
            g_post_ffn[l][None, :],
            w_fo[l].astype(BF16),
            w_out[l].astype(BF16),
            w_gate[l].astype(BF16),
            w_up[l].astype(BF16),
            w_down[l].astype(BF16),
        )
        y_prompt = _layer(y_prompt, mods_p, weights)
        y_sample = _layer(y_sample, mods_s, weights)
    return (y_prompt, y_sample)
```

```python
import functools

import numpy as np
import jax
import jax.numpy as jnp
from jax import lax
from jax.experimental import pallas as pl
from jax.experimental.pallas import tpu as pltpu

D_MODEL = 1024
N_GROUPS = 4
GROUP_DIM = 128
BRANCH_WIDTH = N_GROUPS * GROUP_DIM
POOL_WINDOWS = (2, 4, 8, 16)
D_FF = 2816
N_ADA = 6
EPS = 1e-6

LANES = 128
BF16_ROWS = 16
MXU_DIM = 256
VMEM_LIMIT_BYTES = 60 * 1024 * 1024

SEQ_MINOR = 128
HALO = BF16_ROWS
SUB_TILE = 256
MIX_IN_TILE = 1024
MIX_OUT_TILE = 1024
SUBS_PER_GROUP = 2
K1_PER_TILE = BF16_ROWS
K2_PER_TILE = MIX_OUT_TILE // K1_PER_TILE
ROW_BLOCKS_PER_STEP = 16
ADA_COLS = 512
FF_CHUNKS = ((0, 1024), (1024, 1024), (2048, 768))

BF16 = jnp.bfloat16
F32 = jnp.float32


def _dot(a, b):
    return jnp.dot(a, b, preferred_element_type=F32)


def _sigmoid(v):
    return 1.0 / (1.0 + jnp.exp(-v))


def _rms_scale(v):
    return lax.rsqrt(jnp.mean(v * v, axis=-1, keepdims=True) + EPS)


def _ada_kernel(c_ref, w_ref, b_ref, o_ref):
    c = c_ref[...]
    s = (c * _sigmoid(c)).astype(BF16)
    o_ref[...] = _dot(s, w_ref[...].astype(BF16)) + b_ref[...]


def _ada(c_pad, w_ada, b_ada):
    rows, d = c_pad.shape
    n_out = w_ada.shape[1]
    return pl.pallas_call(
        _ada_kernel,
        grid=(n_out // ADA_COLS,),
        in_specs=[
            pl.BlockSpec((rows, d), lambda j: (0, 0)),
            pl.BlockSpec((d, ADA_COLS), lambda j: (0, j)),
            pl.BlockSpec((1, ADA_COLS), lambda j: (0, j)),
        ],
        out_specs=pl.BlockSpec((rows, ADA_COLS), lambda j: (0, j)),
        out_shape=jax.ShapeDtypeStruct((rows, n_out), F32),
        compiler_params=pltpu.CompilerParams(
            dimension_semantics=("arbitrary",), vmem_limit_bytes=VMEM_LIMIT_BYTES),
        name="ada",
    )(c_pad, w_ada, b_ada)


def _window_sum(ext, w):
    n = ext.shape[0]
    s = ext
    k = 1
    while k < w // 2:
        s = s + pltpu.roll(s, n - k, axis=0)
        k *= 2
    return s + pltpu.roll(s, w // 2, axis=0)


def _mix_in_kernel(tiles_per_seq, seq_len,
                   x_ref, xp_ref, xn_ref, mod_ref, g_ref, wf_ref, wp_ref, wg_ref,
                   wpg_ref, ps_ref, wpo_ref,
                   uf_ref, gf_ref, p_ref):
    t = MIX_IN_TILE
    n_sub = t // SUB_TILE
    ti = lax.rem(pl.program_id(0), tiles_per_seq)
    mod = mod_ref[0]
    scale = g_ref[...] * (1.0 + mod[1:2, :])
    shift = mod[0:1, :]

    def normed(v):
        return v * _rms_scale(v) * scale + shift

    ts = SUB_TILE
    subs = [pl.ds(s * ts, ts) for s in range(n_sub)]
    hs = [normed(x_ref[rows, :]).astype(BF16) for rows in subs]
    h_prev = jnp.where(ti == 0, 0.0, normed(xp_ref[...])).astype(BF16)
    h_next = jnp.where(ti == tiles_per_seq - 1, 0.0, normed(xn_ref[...])).astype(BF16)
    befores = [h_prev] + [h[ts - HALO:] for h in hs[:-1]]
    afters = [h[:HALO] for h in hs[1:]] + [h_next]


    slab = 8
    row = lax.broadcasted_iota(jnp.int32, (slab, GROUP_DIM), 0)

    def pool_diff(up, first_pos):
        diffs = []
        for g, w in enumerate(POOL_WINDOWS):
            assert w & (w - 1) == 0 and w // 2 <= slab
            left = w // 2
            right = w - 1 - left
            ext = up[:, g * GROUP_DIM:(g + 1) * GROUP_DIM]
            win = _window_sum(ext, w)[HALO:HALO + ts]
            u = ext[HALO:HALO + ts]

            def count(pos):
                return (jnp.minimum(pos + right, seq_len - 1)
                        - jnp.maximum(pos - left, 0) + 1).astype(F32)

            diffs.append(jnp.concatenate([
                win[:slab] / count(first_pos + row) - u[:slab],
                win[slab:ts - slab] * (1.0 / w) - u[slab:ts - slab],
                win[ts - slab:] / count(first_pos + (ts - slab) + row) - u[ts - slab:],
            ], axis=0))
        return diffs

    def project(s):
        up = _dot(jnp.concatenate([befores[s], hs[s], afters[s]], axis=0), wp_ref[...])
        uf_ref[subs[s], :] = _dot(hs[s], wf_ref[...]).astype(uf_ref.dtype)
        return up, _dot(hs[s], wg_ref[...])

    def group_mix(s, up):
        diffs = pool_diff(up, ti * t + s * ts)
        ys = []
        for pair in range(N_GROUPS // 2):
            d2 = jnp.concatenate(diffs[2 * pair:2 * pair + 2], axis=1).astype(BF16)
            ys.append(_dot(d2, wpg_ref[pair]))
        return (jnp.concatenate(ys, axis=1) * ps_ref[...]).astype(BF16)

    def finish(s, y, gate_logits):
        g = _sigmoid(gate_logits)
        gf_ref[subs[s], :] = g[:, :D_MODEL].astype(gf_ref.dtype)
        p_ref[subs[s], :] = (g[:, D_MODEL:] * _dot(y, wpo_ref[...])).astype(p_ref.dtype)

    proj, mixed = {}, {}
    for step in range(n_sub + 2):
        if step < n_sub:
            proj[step] = project(step)
        if 0 <= step - 1 < n_sub:
            mixed[step - 1] = group_mix(step - 1, proj[step - 1][0])
        if 0 <= step - 2 < n_sub:
            finish(step - 2, mixed[step - 2], proj[step - 2][1])


def _const_spec(shape):
    zeros = (0,) * len(shape)
    return pl.BlockSpec(shape, lambda *_: zeros, pipeline_mode=pl.Buffered(1))


def _mix_in(x2d, mods, seq_len, g_pre, w_f, w_p, w_g, wpg_bd, pool_scale, w_po):
    n_tok = x2d.shape[0]
    t = MIX_IN_TILE
    tiles_per_seq = seq_len // t
    halo_blocks_per_tile = t // HALO
    n_halo_blocks = n_tok // HALO
    row = lambda i: (i, 0)
    return pl.pallas_call(
        functools.partial(_mix_in_kernel, tiles_per_seq, seq_len),
        grid=(n_tok // t,),
        in_specs=[
            pl.BlockSpec((t, D_MODEL), row),
            pl.BlockSpec((HALO, D_MODEL),
                         lambda i: (jnp.maximum(i * halo_blocks_per_tile - 1, 0), 0)),
            pl.BlockSpec((HALO, D_MODEL),
                         lambda i: (jnp.minimum((i + 1) * halo_blocks_per_tile,
                                                n_halo_blocks - 1), 0)),
            pl.BlockSpec((1, N_ADA, D_MODEL), lambda i: (i // tiles_per_seq, 0, 0)),
            _const_spec((1, D_MODEL)),
            _const_spec((D_MODEL, BRANCH_WIDTH)),
            _const_spec((D_MODEL, BRANCH_WIDTH)),
            _const_spec((D_MODEL, 2 * D_MODEL)),
            _const_spec((N_GROUPS // 2, MXU_DIM, MXU_DIM)),
            _const_spec((1, BRANCH_WIDTH)),
            _const_spec((BRANCH_WIDTH, D_MODEL)),
        ],
        out_specs=[
            pl.BlockSpec((t, BRANCH_WIDTH), row),
            pl.BlockSpec((t, D_MODEL), row),
            pl.BlockSpec((t, D_MODEL), row),
        ],
        out_shape=[
            jax.ShapeDtypeStruct((n_tok, BRANCH_WIDTH), BF16),
            jax.ShapeDtypeStruct((n_tok, D_MODEL), BF16),
            jax.ShapeDtypeStruct((n_tok, D_MODEL), BF16),
        ],
        compiler_params=pltpu.CompilerParams(
            dimension_semantics=("arbitrary",), vmem_limit_bytes=VMEM_LIMIT_BYTES),
        name="mix_in",
    )(x2d, x2d, x2d, mods, g_pre, w_f, w_p, w_g, wpg_bd, pool_scale, w_po)


def _dft_stage1_kernel(x_ref, w_ref, o_ref):
    n1, c2, width = x_ref.shape
    w = w_ref[...]
    xt = jnp.swapaxes(x_ref[...], 0, 1).reshape(ROW_BLOCKS_PER_STEP, SEQ_MINOR, width)
    z = jnp.stack([_dot(w, xt[g]).astype(BF16) for g in range(ROW_BLOCKS_PER_STEP)])
    o_ref[...] = jnp.swapaxes(z.reshape(c2, 2 * n1, width), 0, 1)


def _dft_stage1(u4, w_bd):
    batch, n1, n2, width = u4.shape
    c2 = ROW_BLOCKS_PER_STEP * SEQ_MINOR // n1
    assert n2 % c2 == 0 and c2 % BF16_ROWS == 0
    return pl.pallas_call(
        _dft_stage1_kernel,
        grid=(batch, n2 // c2),
        in_specs=[
            pl.BlockSpec((None, n1, c2, width), lambda b, c: (b, 0, c, 0)),
            _const_spec((2 * SEQ_MINOR, SEQ_MINOR)),
        ],
        out_specs=pl.BlockSpec((None, 2 * n1, c2, width), lambda b, c: (b, 0, c, 0)),
        out_shape=jax.ShapeDtypeStruct((batch, 2 * n1, n2, width), BF16),
        compiler_params=pltpu.CompilerParams(
            dimension_semantics=("arbitrary", "arbitrary"),
            vmem_limit_bytes=VMEM_LIMIT_BYTES),
        name="dft_stage1",
    )(u4, w_bd)


def _dft_stage2_kernel(zr_ref, zi_ref, c2_ref, s2_ref, tc_ref, ts_ref, cs_ref, o_ref):
    c2 = c2_ref[...]
    s2 = s2_ref[...]
    base = pl.program_id(1) * ROW_BLOCKS_PER_STEP
    qs = []
    for j in range(ROW_BLOCKS_PER_STEP):
        tc = tc_ref[pl.ds(base + j, 1), :]
        ts = ts_ref[pl.ds(base + j, 1), :]
        cos_t = c2 * tc - s2 * ts
        sin_t = s2 * tc + c2 * ts
        lhs = jnp.concatenate(
            [jnp.concatenate([cos_t, sin_t], axis=1),
             jnp.concatenate([-sin_t, cos_t], axis=1)], axis=0).astype(BF16)
        z = jnp.concatenate([zr_ref[j], zi_ref[j]], axis=0)
        qs.append(_dot(lhs, z).astype(BF16))
    halves = []
    for pair in range(N_GROUPS // 2):
        rows = []
        for q in qs:
            parts = []
            for g in (2 * pair, 2 * pair + 1):
                lanes = slice(g * GROUP_DIM, (g + 1) * GROUP_DIM)
                parts += [q[:SEQ_MINOR, lanes], q[SEQ_MINOR:, lanes]]
            rows.append(jnp.concatenate(parts, axis=1))
        halves.append(_dot(jnp.concatenate(rows, axis=0), cs_ref[...]))
    y = jnp.concatenate(halves, axis=1).astype(o_ref.dtype)
    o_ref[...] = y.reshape(o_ref.shape)


def _dft_stage2(z4, c2, s2, tc, ts, cs_bd):
    batch, two_n1, n2, width = z4.shape
    n1 = two_n1 // 2
    rb = ROW_BLOCKS_PER_STEP
    assert n1 % rb == 0 and n2 == SEQ_MINOR
    steps = n1 // rb
    return pl.pallas_call(
        _dft_stage2_kernel,
        grid=(batch, steps),
        in_specs=[
            pl.BlockSpec((None, rb, n2, width), lambda b, i: (b, i, 0, 0)),
            pl.BlockSpec((None, rb, n2, width), lambda b, i: (b, steps + i, 0, 0)),
            _const_spec((SEQ_MINOR, SEQ_MINOR)),
            _const_spec((SEQ_MINOR, SEQ_MINOR)),
            _const_spec((n1, SEQ_MINOR)),
            _const_spec((n1, SEQ_MINOR)),
            _const_spec((2 * MXU_DIM, MXU_DIM)),
        ],
        out_specs=pl.BlockSpec((None, rb, n2, width), lambda b, i: (b, i, 0, 0)),
        out_shape=jax.ShapeDtypeStruct((batch, n1, n2, width), BF16),
        compiler_params=pltpu.CompilerParams(
            dimension_semantics=("arbitrary", "arbitrary"),
            vmem_limit_bytes=VMEM_LIMIT_BYTES),
        name="dft_stage2",
    )(z4, z4, c2, s2, tc, ts, cs_bd)


@functools.lru_cache(maxsize=None)
def _dft_tables(seq_len):
    n2 = SEQ_MINOR
    n1 = seq_len // n2
    reps = n2 // n1
    k = np.arange(n1, dtype=np.float64)
    ang1 = 2.0 * np.pi * np.outer(k, k) / n1
    w1 = np.concatenate([np.cos(ang1), -np.sin(ang1)], axis=0)
    w_bd = np.kron(np.eye(reps), w1)
    m = np.arange(n2, dtype=np.float64)
    ang2 = 2.0 * np.pi * np.outer(m, m) / n2
    angt = 2.0 * np.pi * np.outer(k, m) / seq_len
    as32 = lambda a: np.asarray(a, dtype=np.float32)
    return (as32(w_bd), as32(np.cos(ang2)), as32(np.sin(ang2)),
            as32(np.cos(angt)), as32(np.sin(angt)))


@functools.lru_cache(maxsize=None)
def _channel_dft_table(seq_len):
    c = np.arange(GROUP_DIM, dtype=np.float64)
    ang = 2.0 * np.pi * np.outer(c, c) / GROUP_DIM
    norm = 1.0 / np.sqrt(float(seq_len) * GROUP_DIM)
    cs = np.concatenate([np.cos(ang), np.sin(ang)], axis=0) * norm
    return np.asarray(np.kron(np.eye(2), cs), dtype=np.float32)


def _fourier_mix(u_f, batch, seq_len):
    n2 = SEQ_MINOR
    n1 = seq_len // n2
    w_bd, c2, s2, tc, ts = _dft_tables(seq_len)
    cs_bd = jnp.asarray(_channel_dft_table(seq_len)).astype(BF16)
    z = _dft_stage1(u_f.reshape(batch, n1, n2, BRANCH_WIDTH), jnp.asarray(w_bd).astype(BF16))
    return _dft_stage2(z, jnp.asarray(c2), jnp.asarray(s2), jnp.asarray(tc), jnp.asarray(ts),
                       cs_bd)


def _mix_out_kernel(x_ref, y_ref, gf_ref, p_ref, mod_ref, gpm_ref, gpf_ref, gqf_ref,
                    wfo_ref, wout_ref, wgate_ref, wup_ref, wdown_ref, o_ref):
    mod = mod_ref[0]
    n_sub = MIX_OUT_TILE // SUB_TILE
    gate1 = mod[2:3, :] * gpm_ref[...]
    scale2 = gpf_ref[...] * (1.0 + mod[4:5, :])
    shift2 = mod[3:4, :]
    gate2 = mod[5:6, :] * gqf_ref[...]
    k2_sub = SUB_TILE // K1_PER_TILE
    subs = [pl.ds(s * k2_sub, k2_sub) for s in range(n_sub)]

    def rows_of(ref, k2s):
        return ref[k2s].reshape(SUB_TILE, ref.shape[-1])

    def up_proj(h2, chunk):
        cols = pl.ds(*chunk)
        return _dot(h2, wgate_ref[:, cols]), _dot(h2, wup_ref[:, cols])

    def down_proj(ab, chunk):
        a, b = ab
        return _dot((a * _sigmoid(a) * b).astype(BF16), wdown_ref[pl.ds(*chunk), :])

    def front(group):
        ys = [jnp.swapaxes(y_ref[:, k2s, :], 0, 1).reshape(SUB_TILE, BRANCH_WIDTH)
              for k2s in group]
        y_fs = [_dot(y, wfo_ref[...]) for y in ys]
        ms = [_dot((rows_of(gf_ref, k2s).astype(F32) * y_f
                    + rows_of(p_ref, k2s).astype(F32)).astype(BF16), wout_ref[...])
              for k2s, y_f in zip(group, y_fs)]
        x1s = [rows_of(x_ref, k2s) + gate1 * (m * _rms_scale(m))
               for k2s, m in zip(group, ms)]
        h2s = [(x1 * _rms_scale(x1) * scale2 + shift2).astype(BF16) for x1 in x1s]
        return x1s, h2s

    def ffn(h2s, before_last_down):
        zs = [None] * len(h2s)
        pending = [up_proj(h2, FF_CHUNKS[0]) for h2 in h2s]
        result = None
        for c, chunk in enumerate(FF_CHUNKS):
            last = c + 1 == len(FF_CHUNKS)
            nxt = None if last else [up_proj(h2, FF_CHUNKS[c + 1]) for h2 in h2s]
            if last:
                result = before_last_down()
            for s in range(len(h2s)):
                part = down_proj(pending[s], chunk)
                zs[s] = part if zs[s] is None else zs[s] + part
            pending = nxt
        return zs, result

    groups = [subs[i:i + SUBS_PER_GROUP] for i in range(0, n_sub, SUBS_PER_GROUP)]
    x1s, h2s = front(groups[0])
    for gi, group in enumerate(groups):
        following = groups[gi + 1] if gi + 1 < len(groups) else None
        zs, nxt = ffn(h2s, (lambda f=following: front(f)) if following else (lambda: None))
        for k2s, x1, z in zip(group, x1s, zs):
            out = x1 + gate2 * (z * _rms_scale(z))
            o_ref[k2s] = out.reshape(k2_sub, K1_PER_TILE, D_MODEL)
        if nxt is not None:
            x1s, h2s = nxt


def _mix_out(x, y4, g_f, p, mods, g_post_mix, g_pre_ffn, g_post_ffn,
             w_fo, w_out, w_gate, w_up, w_down):
    batch, seq_len, d = x.shape
    n1 = y4.shape[1]
    n2 = SEQ_MINOR
    k1t, k2t = K1_PER_TILE, K2_PER_TILE
    assert n1 % k1t == 0 and n2 % k2t == 0 and k1t * k2t == MIX_OUT_TILE
    by_k = lambda a: a.reshape(batch, n2, n1 // k1t, k1t, a.shape[-1])
    tok_spec = pl.BlockSpec((None, k2t, None, k1t, d), lambda b, a, t: (b, a, t, 0, 0))
    return pl.pallas_call(
        _mix_out_kernel,
        grid=(batch, n2 // k2t, n1 // k1t),
        in_specs=[
            tok_spec,
            pl.BlockSpec((None, k1t, k2t, BRANCH_WIDTH), lambda b, a, t: (b, t, a, 0)),
            tok_spec,
            tok_spec,
            pl.BlockSpec((1, N_ADA, D_MODEL), lambda b, a, t: (b, 0, 0)),
            _const_spec((1, D_MODEL)),
            _const_spec((1, D_MODEL)),
            _const_spec((1, D_MODEL)),
            _const_spec((BRANCH_WIDTH, D_MODEL)),
            _const_spec((D_MODEL, D_MODEL)),
            _const_spec((D_MODEL, D_FF)),
            _const_spec((D_MODEL, D_FF)),
            _const_spec((D_FF, D_MODEL)),
        ],
        out_specs=tok_spec,
        out_shape=jax.ShapeDtypeStruct((batch, n2, n1 // k1t, k1t, d), F32),
        compiler_params=pltpu.CompilerParams(
            dimension_semantics=("arbitrary", "arbitrary", "arbitrary"),
            vmem_limit_bytes=VMEM_LIMIT_BYTES),
        name="mix_out",
    )(by_k(x), y4, by_k(g_f), by_k(p), mods, g_post_mix, g_pre_ffn, g_post_ffn,
      w_fo, w_out, w_gate, w_up, w_down).reshape(batch, seq_len, d)


def _layer(x, mods, weights):
    (g_pre_mix, w_f, w_p, w_g, wpg_bd, pool_scale, w_po, g_post_mix, g_pre_ffn,
     g_post_ffn, w_fo, w_out, w_gate, w_up, w_down) = weights
    batch, seq_len, d = x.shape
    x2d = x.reshape(batch * seq_len, d)
    u_f, g_f, p = _mix_in(x2d, mods, seq_len, g_pre_mix, w_f, w_p, w_g, wpg_bd,
                          pool_scale, w_po)
    y4 = _fourier_mix(u_f, batch, seq_len)
    return _mix_out(x, y4, g_f, p, mods, g_post_mix, g_pre_ffn, g_post_ffn,
                    w_fo, w_out, w_gate, w_up, w_down)


def kernel(x_prompt, x_sample, c_prompt, c_sample, w_ada, b_ada, g_pre_mix, w_in, w_fo, w_pg,
           pool_scale, w_po, w_out, g_post_mix, g_pre_ffn, w_gate, w_up, w_down, g_post_ffn):
    depth = w_ada.shape[0]
    y_prompt, y_sample = x_prompt, x_sample
    bp, bs = c_prompt.shape[0], c_sample.shape[0]
    pad_rows = -(bp + bs) % BF16_ROWS
    c_pad = jnp.concatenate(
        [c_prompt, c_sample, jnp.zeros((pad_rows, D_MODEL), F32)], axis=0)
    for l in range(depth):
        ada = _ada(c_pad, w_ada[l], b_ada[l][None, :])
        mods_p = ada[:bp].reshape(bp, N_ADA, D_MODEL)
        mods_s = ada[bp:bp + bs].reshape(bs, N_ADA, D_MODEL)
        w_in_l = w_in[l].astype(BF16)
        zero_blk = jnp.zeros((GROUP_DIM, GROUP_DIM), F32)
        wpg_bd = jnp.stack([
            jnp.block([[w_pg[l, 2 * pr], zero_blk], [zero_blk, w_pg[l, 2 * pr + 1]]])
            for pr in range(N_GROUPS // 2)]).astype(BF16)
        weights = (
            g_pre_mix[l][None, :],
            w_in_l[:, :BRANCH_WIDTH],
            w_in_l[:, BRANCH_WIDTH:2 * BRANCH_WIDTH],
            w_in_l[:, 2 * BRANCH_WIDTH:],
            wpg_bd,
            pool_scale[l][None, :],
            w_po[l].astype(BF16),
            g_post_mix[l][None, :],
            g_pre_ffn[l][None, :],
            g_post_ffn[l][None, :],
            w_fo[l].astype(BF16),
            w_out[l].astype(BF16),
            w_gate[l].astype(BF16),
            w_up[l].astype(BF16),
            w_down[l].astype(BF16),
        )
        y_prompt = _layer(y_prompt, mods_p, weights)
        y_sample = _layer(y_sample, mods_s, weights)
    return (y_prompt, y_sample)
```

```python
import functools

import numpy as np
import jax
import jax.numpy as jnp
from jax import lax
from jax.experimental import pallas as pl
from jax.experimental.pallas import tpu as pltpu

D_MODEL = 1024
N_GROUPS = 4
GROUP_DIM = 128
BRANCH_WIDTH = N_GROUPS * GROUP_DIM
POOL_WINDOWS = (2, 4, 8, 16)
D_FF = 2816
N_ADA = 6
EPS = 1e-6

F32_ROWS = 8
BF16_ROWS = 16
MXU_DIM = 256
VMEM_LIMIT_BYTES = 60 * 1024 * 1024

SEQ_MINOR = 128
HALO = BF16_ROWS
SUB_TILE = 256
MIX_IN_TILE = 1024
MIX_OUT_TILE = 512
MIX_OUT_SUB = 256
SUBS_PER_GROUP = 2
K1_PER_TILE = BF16_ROWS
K2_PER_TILE = MIX_OUT_TILE // K1_PER_TILE
ROW_BLOCKS_PER_STEP = 16
ADA_COLS = 1024
FF_CHUNKS = ((0, 1024), (1024, 1024), (2048, 768))

BF16 = jnp.bfloat16
F32 = jnp.float32


def _dot(a, b):
    return jnp.dot(a, b, preferred_element_type=F32)


def _sigmoid(v):
    return 1.0 / (1.0 + jnp.exp(-v))


def _rms_scale(v):
    return lax.rsqrt(jnp.mean(v * v, axis=-1, keepdims=True) + EPS)


def _ada_kernel(c_ref, w_ref, b_ref, o_ref):
    c = c_ref[...]
    s = (c * _sigmoid(c)).astype(BF16)
    o_ref[...] = _dot(s, w_ref[...].astype(BF16)) + b_ref[...]


def _ada(c_pad, w_ada, b_ada):
    rows, d = c_pad.shape
    n_out = w_ada.shape[1]
    return pl.pallas_call(
        _ada_kernel,
        grid=(n_out // ADA_COLS,),
        in_specs=[
            pl.BlockSpec((rows, d), lambda j: (0, 0)),
            pl.BlockSpec((d, ADA_COLS), lambda j: (0, j)),
            pl.BlockSpec((1, ADA_COLS), lambda j: (0, j)),
        ],
        out_specs=pl.BlockSpec((rows, ADA_COLS), lambda j: (0, j)),
        out_shape=jax.ShapeDtypeStruct((rows, n_out), F32),
        compiler_params=pltpu.CompilerParams(
            dimension_semantics=("arbitrary",), vmem_limit_bytes=VMEM_LIMIT_BYTES),
        name="ada",
    )(c_pad, w_ada, b_ada)


def _window_sum(ext, w):
    n = ext.shape[0]
    s = ext
    k = 1
    while k < w // 2:
        s = s + pltpu.roll(s, n - k, axis=0)
        k *= 2
    return s + pltpu.roll(s, w // 2, axis=0)


def _mix_in_kernel(tiles_per_seq, seq_len, n_cast,
                   x_ref, xp_ref, xn_ref, mod_ref, g_ref, wf_ref, wp_ref, wg_ref,
                   wpg_ref, ps_ref, wpo_ref, *rest):
    cast_in, (uf_ref, gf_ref, p_ref), cast_out = (
        rest[:n_cast], rest[n_cast:n_cast + 3], rest[n_cast + 3:])
    for src, dst in zip(cast_in, cast_out):
        dst[...] = src[...].astype(dst.dtype)
    t = MIX_IN_TILE
    n_sub = t // SUB_TILE
    ti = lax.rem(pl.program_id(0), tiles_per_seq)
    mod = mod_ref[0]
    scale = g_ref[...] * (1.0 + mod[1:2, :])
    shift = mod[0:1, :]

    def normed(v):
        return v * _rms_scale(v) * scale + shift

    ts = SUB_TILE
    subs = [pl.ds(s * ts, ts) for s in range(n_sub)]
    hs = [normed(x_ref[rows, :]).astype(BF16) for rows in subs]
    h_prev = jnp.where(ti == 0, 0.0, normed(xp_ref[...])).astype(BF16)
    h_next = jnp.where(ti == tiles_per_seq - 1, 0.0, normed(xn_ref[...])).astype(BF16)
    befores = [h_prev] + [h[ts - HALO:] for h in hs[:-1]]
    afters = [h[:HALO] for h in hs[1:]] + [h_next]


    slab = F32_ROWS
    row = lax.broadcasted_iota(jnp.int32, (slab, GROUP_DIM), 0)

    def pool_diff(up, first_pos):
        diffs = []
        for g, w in enumerate(POOL_WINDOWS):
            assert w & (w - 1) == 0 and w // 2 <= slab
            left = w // 2
            right = w - 1 - left
            ext = up[:, g * GROUP_DIM:(g + 1) * GROUP_DIM]
            win = _window_sum(ext, w)[HALO:HALO + ts]
            u = ext[HALO:HALO + ts]

            def count(pos):
                return (jnp.minimum(pos + right, seq_len - 1)
                        - jnp.maximum(pos - left, 0) + 1).astype(F32)

            diffs.append(jnp.concatenate([
                win[:slab] / count(first_pos + row) - u[:slab],
                win[slab:ts - slab] * (1.0 / w) - u[slab:ts - slab],
                win[ts - slab:] / count(first_pos + (ts - slab) + row) - u[ts - slab:],
            ], axis=0))
        return diffs

    def project(s):
        up = _dot(jnp.concatenate([befores[s], hs[s], afters[s]], axis=0), wp_ref[...])
        uf_ref[subs[s], :] = _dot(hs[s], wf_ref[...]).astype(uf_ref.dtype)
        return up, _dot(hs[s], wg_ref[...])

    def group_mix(s, up):
        diffs = pool_diff(up, ti * t + s * ts)
        ys = []
        for pair in range(N_GROUPS // 2):
            d2 = jnp.concatenate(diffs[2 * pair:2 * pair + 2], axis=1).astype(BF16)
            ys.append(_dot(d2, wpg_ref[pair]))
        return (jnp.concatenate(ys, axis=1) * ps_ref[...]).astype(BF16)

    def finish(s, y, gate_logits):
        g = _sigmoid(gate_logits)
        gf_ref[subs[s], :] = g[:, :D_MODEL].astype(gf_ref.dtype)
        p_ref[subs[s], :] = (g[:, D_MODEL:] * _dot(y, wpo_ref[...])).astype(p_ref.dtype)

    proj, mixed = {}, {}
    for step in range(n_sub + 2):
        if step < n_sub:
            proj[step] = project(step)
        if 0 <= step - 1 < n_sub:
            mixed[step - 1] = group_mix(step - 1, proj[step - 1][0])
        if 0 <= step - 2 < n_sub:
            finish(step - 2, mixed[step - 2], proj[step - 2][1])


def _const_spec(shape):
    zeros = (0,) * len(shape)
    return pl.BlockSpec(shape, lambda *_: zeros, pipeline_mode=pl.Buffered(1))


def _cast_specs(weights, n_steps):
    specs, shapes = [], []
    for w in weights:
        rows, cols = w.shape
        n_blocks = n_steps
        while rows % n_blocks or (rows // n_blocks) % BF16_ROWS:
            n_blocks //= 2
        assert n_blocks >= 1
        spec = pl.BlockSpec((rows // n_blocks, cols),
                            lambda i, last=n_blocks - 1: (jnp.minimum(i, last), 0))
        specs.append(spec)
        shapes.append(jax.ShapeDtypeStruct(w.shape, BF16))
    return specs, shapes


def _mix_in(x2d, mods, seq_len, g_pre, w_f, w_p, w_g, wpg_bd, pool_scale, w_po,
            cast_weights=()):
    n_tok = x2d.shape[0]
    t = MIX_IN_TILE
    tiles_per_seq = seq_len // t
    halo_blocks_per_tile = t // HALO
    n_halo_blocks = n_tok // HALO
    row = lambda i: (i, 0)
    cast_specs, cast_shapes = _cast_specs(cast_weights, n_tok // t)
    return pl.pallas_call(
        functools.partial(_mix_in_kernel, tiles_per_seq, seq_len, len(cast_weights)),
        grid=(n_tok // t,),
        in_specs=[
            pl.BlockSpec((t, D_MODEL), row),
            pl.BlockSpec((HALO, D_MODEL),
                         lambda i: (jnp.maximum(i * halo_blocks_per_tile - 1, 0), 0)),
            pl.BlockSpec((HALO, D_MODEL),
                         lambda i: (jnp.minimum((i + 1) * halo_blocks_per_tile,
                                                n_halo_blocks - 1), 0)),
            pl.BlockSpec((1, N_ADA, D_MODEL), lambda i: (i // tiles_per_seq, 0, 0)),
            _const_spec((1, D_MODEL)),
            _const_spec((D_MODEL, BRANCH_WIDTH)),
            _const_spec((D_MODEL, BRANCH_WIDTH)),
            _const_spec((D_MODEL, 2 * D_MODEL)),
            _const_spec((N_GROUPS // 2, MXU_DIM, MXU_DIM)),
            _const_spec((1, BRANCH_WIDTH)),
            _const_spec((BRANCH_WIDTH, D_MODEL)),
        ] + cast_specs,
        out_specs=[
            pl.BlockSpec((t, BRANCH_WIDTH), row),
            pl.BlockSpec((t, D_MODEL), row),
            pl.BlockSpec((t, D_MODEL), row),
        ] + cast_specs,
        out_shape=[
            jax.ShapeDtypeStruct((n_tok, BRANCH_WIDTH), BF16),
            jax.ShapeDtypeStruct((n_tok, D_MODEL), BF16),
            jax.ShapeDtypeStruct((n_tok, D_MODEL), BF16),
        ] + cast_shapes,
        compiler_params=pltpu.CompilerParams(
            dimension_semantics=("arbitrary",), vmem_limit_bytes=VMEM_LIMIT_BYTES),
        name="mix_in",
    )(x2d, x2d, x2d, mods, g_pre, w_f, w_p, w_g, wpg_bd, pool_scale, w_po, *cast_weights)


def _dft_stage1_kernel(x_ref, w_ref, o_ref):
    n1, c2, width = x_ref.shape
    w = w_ref[...]
    xt = jnp.swapaxes(x_ref[...], 0, 1).reshape(ROW_BLOCKS_PER_STEP, SEQ_MINOR, width)
    z = jnp.stack([_dot(w, xt[g]).astype(BF16) for g in range(ROW_BLOCKS_PER_STEP)])
    o_ref[...] = jnp.swapaxes(z.reshape(c2, 2 * n1, width), 0, 1)


def _dft_stage1(u4, w_bd):
    batch, n1, n2, width = u4.shape
    c2 = ROW_BLOCKS_PER_STEP * SEQ_MINOR // n1
    assert n2 % c2 == 0 and c2 % BF16_ROWS == 0
    return pl.pallas_call(
        _dft_stage1_kernel,
        grid=(batch, n2 // c2),
        in_specs=[
            pl.BlockSpec((None, n1, c2, width), lambda b, c: (b, 0, c, 0)),
            _const_spec((2 * SEQ_MINOR, SEQ_MINOR)),
        ],
        out_specs=pl.BlockSpec((None, 2 * n1, c2, width), lambda b, c: (b, 0, c, 0)),
        out_shape=jax.ShapeDtypeStruct((batch, 2 * n1, n2, width), BF16),
        compiler_params=pltpu.CompilerParams(
            dimension_semantics=("arbitrary", "arbitrary"),
            vmem_limit_bytes=VMEM_LIMIT_BYTES),
        name="dft_stage1",
    )(u4, w_bd)


def _dft_stage2_kernel(zr_ref, zi_ref, c2_ref, s2_ref, tc_ref, ts_ref, cs_ref, o_ref):
    c2 = c2_ref[...]
    s2 = s2_ref[...]
    base = pl.program_id(1) * ROW_BLOCKS_PER_STEP
    qs = []
    for j in range(ROW_BLOCKS_PER_STEP):
        tc = tc_ref[pl.ds(base + j, 1), :]
        ts = ts_ref[pl.ds(base + j, 1), :]
        cos_t = c2 * tc - s2 * ts
        sin_t = s2 * tc + c2 * ts
        lhs = jnp.concatenate(
            [jnp.concatenate([cos_t, sin_t], axis=1),
             jnp.concatenate([-sin_t, cos_t], axis=1)], axis=0).astype(BF16)
        z = jnp.concatenate([zr_ref[j], zi_ref[j]], axis=0)
        qs.append(_dot(lhs, z).astype(BF16))
    halves = []
    for pair in range(N_GROUPS // 2):
        rows = []
        for q in qs:
            parts = []
            for g in (2 * pair, 2 * pair + 1):
                lanes = slice(g * GROUP_DIM, (g + 1) * GROUP_DIM)
                parts += [q[:SEQ_MINOR, lanes], q[SEQ_MINOR:, lanes]]
            rows.append(jnp.concatenate(parts, axis=1))
        halves.append(_dot(jnp.concatenate(rows, axis=0), cs_ref[...]))
    y = jnp.concatenate(halves, axis=1).astype(o_ref.dtype)
    o_ref[...] = y.reshape(o_ref.shape)


def _dft_stage2(z4, c2, s2, tc, ts, cs_bd):
    batch, two_n1, n2, width = z4.shape
    n1 = two_n1 // 2
    rb = ROW_BLOCKS_PER_STEP
    assert n1 % rb == 0 and n2 == SEQ_MINOR
    steps = n1 // rb
    return pl.pallas_call(
        _dft_stage2_kernel,
        grid=(batch, steps),
        in_specs=[
            pl.BlockSpec((None, rb, n2, width), lambda b, i: (b, i, 0, 0)),
            pl.BlockSpec((None, rb, n2, width), lambda b, i: (b, steps + i, 0, 0)),
            _const_spec((SEQ_MINOR, SEQ_MINOR)),
            _const_spec((SEQ_MINOR, SEQ_MINOR)),
            _const_spec((n1, SEQ_MINOR)),
            _const_spec((n1, SEQ_MINOR)),
            _const_spec((2 * MXU_DIM, MXU_DIM)),
        ],
        out_specs=pl.BlockSpec((None, rb, n2, width), lambda b, i: (b, i, 0, 0)),
        out_shape=jax.ShapeDtypeStruct((batch, n1, n2, width), BF16),
        compiler_params=pltpu.CompilerParams(
            dimension_semantics=("arbitrary", "arbitrary"),
            vmem_limit_bytes=VMEM_LIMIT_BYTES),
        name="dft_stage2",
    )(z4, z4, c2, s2, tc, ts, cs_bd)


@functools.lru_cache(maxsize=None)
def _dft_tables(seq_len):
    n2 = SEQ_MINOR
    n1 = seq_len // n2
    reps = n2 // n1
    k = np.arange(n1, dtype=np.float64)
    ang1 = 2.0 * np.pi * np.outer(k, k) / n1
    w1 = np.concatenate([np.cos(ang1), -np.sin(ang1)], axis=0)
    w_bd = np.kron(np.eye(reps), w1)
    m = np.arange(n2, dtype=np.float64)
    ang2 = 2.0 * np.pi * np.outer(m, m) / n2
    angt = 2.0 * np.pi * np.outer(k, m) / seq_len
    as32 = lambda a: np.asarray(a, dtype=np.float32)
    return (as32(w_bd), as32(np.cos(ang2)), as32(np.sin(ang2)),
            as32(np.cos(angt)), as32(np.sin(angt)))


@functools.lru_cache(maxsize=None)
def _channel_dft_table(seq_len):
    c = np.arange(GROUP_DIM, dtype=np.float64)
    ang = 2.0 * np.pi * np.outer(c, c) / GROUP_DIM
    norm = 1.0 / np.sqrt(float(seq_len) * GROUP_DIM)
    cs = np.concatenate([np.cos(ang), np.sin(ang)], axis=0) * norm
    return np.asarray(np.kron(np.eye(2), cs), dtype=np.float32)


def _fourier_mix(u_f, batch, seq_len):
    n2 = SEQ_MINOR
    n1 = seq_len // n2
    w_bd, c2, s2, tc, ts = _dft_tables(seq_len)
    cs_bd = jnp.asarray(_channel_dft_table(seq_len)).astype(BF16)
    z = _dft_stage1(u_f.reshape(batch, n1, n2, BRANCH_WIDTH), jnp.asarray(w_bd).astype(BF16))
    return _dft_stage2(z, jnp.asarray(c2), jnp.asarray(s2), jnp.asarray(tc), jnp.asarray(ts),
                       cs_bd)


def _mix_out_kernel(x_ref, y_ref, gf_ref, p_ref, mod_ref, gpm_ref, gpf_ref, gqf_ref,
                    wfo_ref, wout_ref, wgate_ref, wup_ref, wdown_ref, o_ref):
    mod = mod_ref[0]
    n_sub = MIX_OUT_TILE // MIX_OUT_SUB
    gate1 = mod[2:3, :] * gpm_ref[...]
    scale2 = gpf_ref[...] * (1.0 + mod[4:5, :])
    shift2 = mod[3:4, :]
    gate2 = mod[5:6, :] * gqf_ref[...]
    k2_sub = MIX_OUT_SUB // K1_PER_TILE
    subs = [pl.ds(s * k2_sub, k2_sub) for s in range(n_sub)]

    def rows_of(ref, k2s):
        return ref[k2s].reshape(MIX_OUT_SUB, ref.shape[-1])

    def up_proj(h2, chunk):
        cols = pl.ds(*chunk)
        return _dot(h2, wgate_ref[:, cols]), _dot(h2, wup_ref[:, cols])

    def down_proj(ab, chunk):
        a, b = ab
        return _dot((a * _sigmoid(a) * b).astype(BF16), wdown_ref[pl.ds(*chunk), :])

    def front(group):
        ys = [jnp.swapaxes(y_ref[:, k2s, :], 0, 1).reshape(MIX_OUT_SUB, BRANCH_WIDTH)
              for k2s in group]
        y_fs = [_dot(y, wfo_ref[...]) for y in ys]
        ms = [_dot((rows_of(gf_ref, k2s).astype(F32) * y_f
                    + rows_of(p_ref, k2s).astype(F32)).astype(BF16), wout_ref[...])
              for k2s, y_f in zip(group, y_fs)]
        x1s = [rows_of(x_ref, k2s) + gate1 * (m * _rms_scale(m))
               for k2s, m in zip(group, ms)]
        h2s = [(x1 * _rms_scale(x1) * scale2 + shift2).astype(BF16) for x1 in x1s]
        return x1s, h2s

    def ffn(h2s, before_last_down):
        zs = [None] * len(h2s)
        pending = [up_proj(h2, FF_CHUNKS[0]) for h2 in h2s]
        result = None
        for c, chunk in enumerate(FF_CHUNKS):
            last = c + 1 == len(FF_CHUNKS)
            nxt = None if last else [up_proj(h2, FF_CHUNKS[c + 1]) for h2 in h2s]
            if last:
                result = before_last_down()
            for s in range(len(h2s)):
                part = down_proj(pending[s], chunk)
                zs[s] = part if zs[s] is None else zs[s] + part
            pending = nxt
        return zs, result

    groups = [subs[i:i + SUBS_PER_GROUP] for i in range(0, n_sub, SUBS_PER_GROUP)]
    x1s, h2s = front(groups[0])
    for gi, group in enumerate(groups):
        following = groups[gi + 1] if gi + 1 < len(groups) else None
        zs, nxt = ffn(h2s, (lambda f=following: front(f)) if following else (lambda: None))
        for k2s, x1, z in zip(group, x1s, zs):
            out = x1 + gate2 * (z * _rms_scale(z))
            o_ref[k2s] = out.reshape(k2_sub, K1_PER_TILE, D_MODEL)
        if nxt is not None:
            x1s, h2s = nxt


def _mix_out(x, y4, g_f, p, mods, g_post_mix, g_pre_ffn, g_post_ffn,
             w_fo, w_out, w_gate, w_up, w_down):
    batch, seq_len, d = x.shape
    n1 = y4.shape[1]
    n2 = SEQ_MINOR
    k1t, k2t = K1_PER_TILE, K2_PER_TILE
    assert n1 % k1t == 0 and n2 % k2t == 0 and k1t * k2t == MIX_OUT_TILE
    by_k = lambda a: a.reshape(batch, n2, n1 // k1t, k1t, a.shape[-1])
    tok_spec = pl.BlockSpec((None, k2t, None, k1t, d), lambda b, a, t: (b, a, t, 0, 0))
    return pl.pallas_call(
        _mix_out_kernel,
        grid=(batch, n2 // k2t, n1 // k1t),
        in_specs=[
            tok_spec,
            pl.BlockSpec((None, k1t, k2t, BRANCH_WIDTH), lambda b, a, t: (b, t, a, 0)),
            tok_spec,
            tok_spec,
            pl.BlockSpec((1, N_ADA, D_MODEL), lambda b, a, t: (b, 0, 0)),
            _const_spec((1, D_MODEL)),
            _const_spec((1, D_MODEL)),
            _const_spec((1, D_MODEL)),
            _const_spec((BRANCH_WIDTH, D_MODEL)),
            _const_spec((D_MODEL, D_MODEL)),
            _const_spec((D_MODEL, D_FF)),
            _const_spec((D_MODEL, D_FF)),
            _const_spec((D_FF, D_MODEL)),
        ],
        out_specs=tok_spec,
        out_shape=jax.ShapeDtypeStruct((batch, n2, n1 // k1t, k1t, d), F32),
        compiler_params=pltpu.CompilerParams(
            dimension_semantics=("arbitrary", "arbitrary", "arbitrary"),
            vmem_limit_bytes=VMEM_LIMIT_BYTES),
        name="mix_out",
    )(by_k(x), y4, by_k(g_f), by_k(p), mods, g_post_mix, g_pre_ffn, g_post_ffn,
      w_fo, w_out, w_gate, w_up, w_down).reshape(batch, seq_len, d)


def _layer(x, mods, early, gains, late):
    g_pre_mix, w_f, w_p, w_g, wpg_bd, pool_scale, w_po = early
    batch, seq_len, d = x.shape
    x2d = x.reshape(batch * seq_len, d)
    to_cast = late if late[0].dtype != BF16 else ()
    u_f, g_f, p, *cast = _mix_in(x2d, mods, seq_len, g_pre_mix, w_f, w_p, w_g, wpg_bd,
                                 pool_scale, w_po, cast_weights=to_cast)
    late = tuple(cast) if to_cast else late
    y4 = _fourier_mix(u_f, batch, seq_len)
    return _mix_out(x, y4, g_f, p, mods, *gains, *late), late


def kernel(x_prompt, x_sample, c_prompt, c_sample, w_ada, b_ada, g_pre_mix, w_in, w_fo, w_pg,
           pool_scale, w_po, w_out, g_post_mix, g_pre_ffn, w_gate, w_up, w_down, g_post_ffn):
    depth = w_ada.shape[0]
    y_prompt, y_sample = x_prompt, x_sample
    bp, bs = c_prompt.shape[0], c_sample.shape[0]
    pad_rows = -(bp + bs) % BF16_ROWS
    c_pad = jnp.concatenate(
        [c_prompt, c_sample, jnp.zeros((pad_rows, D_MODEL), F32)], axis=0)
    for l in range(depth):
        ada = _ada(c_pad, w_ada[l], b_ada[l][None, :])
        mods_p = ada[:bp].reshape(bp, N_ADA, D_MODEL)
        mods_s = ada[bp:bp + bs].reshape(bs, N_ADA, D_MODEL)
        w_in_l = w_in[l].astype(BF16)
        zero_blk = jnp.zeros((GROUP_DIM, GROUP_DIM), F32)
        wpg_bd = jnp.stack([
            jnp.block([[w_pg[l, 2 * pr], zero_blk], [zero_blk, w_pg[l, 2 * pr + 1]]])
            for pr in range(N_GROUPS // 2)]).astype(BF16)
        early = (
            g_pre_mix[l][None, :],
            w_in_l[:, :BRANCH_WIDTH],
            w_in_l[:, BRANCH_WIDTH:2 * BRANCH_WIDTH],
            w_in_l[:, 2 * BRANCH_WIDTH:],
            wpg_bd,
            pool_scale[l][None, :],
            w_po[l].astype(BF16),
        )
        gains = (g_post_mix[l][None, :], g_pre_ffn[l][None, :], g_post_ffn[l][None, :])
        late = (w_fo[l], w_out[l], w_gate[l], w_up[l], w_down[l])
        y_prompt, late = _layer(y_prompt, mods_p, early, gains, late)
        y_sample, _ = _layer(y_sample, mods_s, early, gains, late)
    return (y_prompt, y_sample)
```

```python
import functools

import numpy as np
import jax
import jax.numpy as jnp
from jax import lax
from jax.experimental import pallas as pl
from jax.experimental.pallas import tpu as pltpu

D_MODEL = 1024
N_GROUPS = 4
GROUP_DIM = 128
BRANCH_WIDTH = N_GROUPS * GROUP_DIM
POOL_WINDOWS = (2, 4, 8, 16)
D_FF = 2816
N_ADA = 6
EPS = 1e-6

F32_ROWS = 8
BF16_ROWS = 16
MXU_DIM = 256
VMEM_LIMIT_BYTES = 60 * 1024 * 1024

SEQ_MINOR = 128
HALO = BF16_ROWS
SUB_TILE = 256
MIX_IN_TILE = 1024
MIX_OUT_TILE = 512
MIX_OUT_SUB = 256
SUBS_PER_GROUP = 2
K1_PER_TILE = BF16_ROWS
K2_PER_TILE = MIX_OUT_TILE // K1_PER_TILE
ROW_BLOCKS_PER_STEP = 16
ADA_COLS = 1024
FF_CHUNKS = ((0, 1024), (1024, 1024), (2048, 768))

BF16 = jnp.bfloat16
F32 = jnp.float32


def _dot(a, b):
    return jnp.dot(a, b, preferred_element_type=F32)


def _sigmoid(v):
    return 1.0 / (1.0 + jnp.exp(-v))


def _rms_scale(v):
    return lax.rsqrt(jnp.mean(v * v, axis=-1, keepdims=True) + EPS)


def _ada_kernel(c_ref, w_ref, b_ref, o_ref):
    c = c_ref[...]
    s = (c * _sigmoid(c)).astype(BF16)
    o_ref[...] = _dot(s, w_ref[...].astype(BF16)) + b_ref[...]


def _ada(c_pad, w_ada, b_ada):
    rows, d = c_pad.shape
    n_out = w_ada.shape[1]
    return pl.pallas_call(
        _ada_kernel,
        grid=(n_out // ADA_COLS,),
        in_specs=[
            pl.BlockSpec((rows, d), lambda j: (0, 0)),
            pl.BlockSpec((d, ADA_COLS), lambda j: (0, j)),
            pl.BlockSpec((1, ADA_COLS), lambda j: (0, j)),
        ],
        out_specs=pl.BlockSpec((rows, ADA_COLS), lambda j: (0, j)),
        out_shape=jax.ShapeDtypeStruct((rows, n_out), F32),
        compiler_params=pltpu.CompilerParams(
            dimension_semantics=("arbitrary",), vmem_limit_bytes=VMEM_LIMIT_BYTES),
        name="ada",
    )(c_pad, w_ada, b_ada)


def _window_sum(ext, w):
    n = ext.shape[0]
    s = ext
    k = 1
    while k < w // 2:
        s = s + pltpu.roll(s, n - k, axis=0)
        k *= 2
    return s + pltpu.roll(s, w // 2, axis=0)


def _mix_in_kernel(tiles_per_seq, seq_len, n_cast, has_dft_job,
                   x_ref, xp_ref, xn_ref, mod_ref, g_ref, wf_ref, wp_ref, wg_ref,
                   wpg_ref, ps_ref, wpo_ref, *rest):
    n_extra_in = n_cast + 2 * has_dft_job
    cast_in, dft_in = rest[:n_cast], rest[n_cast:n_extra_in]
    uf_ref, gf_ref, p_ref = rest[n_extra_in:n_extra_in + 3]
    cast_out = rest[n_extra_in + 3:n_extra_in + 3 + n_cast]
    dft = _Stage1Body(*dft_in, rest[-1]) if has_dft_job else None
    if dft:
        dft.load()
    for src, dst in zip(cast_in, cast_out):
        dst[...] = src[...].astype(dst.dtype)
    t = MIX_IN_TILE
    n_sub = t // SUB_TILE
    ti = lax.rem(pl.program_id(0), tiles_per_seq)
    mod = mod_ref[0]
    scale = g_ref[...] * (1.0 + mod[1:2, :])
    shift = mod[0:1, :]

    def normed(v):
        return v * _rms_scale(v) * scale + shift

    ts = SUB_TILE
    subs = [pl.ds(s * ts, ts) for s in range(n_sub)]
    hs = [normed(x_ref[rows, :]).astype(BF16) for rows in subs]
    h_prev = jnp.where(ti == 0, 0.0, normed(xp_ref[...])).astype(BF16)
    h_next = jnp.where(ti == tiles_per_seq - 1, 0.0, normed(xn_ref[...])).astype(BF16)
    befores = [h_prev] + [h[ts - HALO:] for h in hs[:-1]]
    afters = [h[:HALO] for h in hs[1:]] + [h_next]


    slab = F32_ROWS
    row = lax.broadcasted_iota(jnp.int32, (slab, GROUP_DIM), 0)

    def pool_diff(up, first_pos):
        diffs = []
        for g, w in enumerate(POOL_WINDOWS):
            assert w & (w - 1) == 0 and w // 2 <= slab
            left = w // 2
            right = w - 1 - left
            ext = up[:, g * GROUP_DIM:(g + 1) * GROUP_DIM]
            win = _window_sum(ext, w)[HALO:HALO + ts]
            u = ext[HALO:HALO + ts]

            def count(pos):
                return (jnp.minimum(pos + right, seq_len - 1)
                        - jnp.maximum(pos - left, 0) + 1).astype(F32)

            diffs.append(jnp.concatenate([
                win[:slab] / count(first_pos + row) - u[:slab],
                win[slab:ts - slab] * (1.0 / w) - u[slab:ts - slab],
                win[ts - slab:] / count(first_pos + (ts - slab) + row) - u[ts - slab:],
            ], axis=0))
        return diffs

    def project(s):
        up = _dot(jnp.concatenate([befores[s], hs[s], afters[s]], axis=0), wp_ref[...])
        uf_ref[subs[s], :] = _dot(hs[s], wf_ref[...]).astype(uf_ref.dtype)
        return up, _dot(hs[s], wg_ref[...])

    def group_mix(s, up):
        diffs = pool_diff(up, ti * t + s * ts)
        ys = []
        for pair in range(N_GROUPS // 2):
            d2 = jnp.concatenate(diffs[2 * pair:2 * pair + 2], axis=1).astype(BF16)
            ys.append(_dot(d2, wpg_ref[pair]))
        return (jnp.concatenate(ys, axis=1) * ps_ref[...]).astype(BF16)

    def finish(s, y, gate_logits):
        g = _sigmoid(gate_logits)
        gf_ref[subs[s], :] = g[:, :D_MODEL].astype(gf_ref.dtype)
        p_ref[subs[s], :] = (g[:, D_MODEL:] * _dot(y, wpo_ref[...])).astype(p_ref.dtype)

    proj, mixed = {}, {}
    for step in range(n_sub + 2):
        if step < n_sub:
            proj[step] = project(step)
        if dft and step == 0:
            dft.matmul()
            dft.store()
        if 0 <= step - 1 < n_sub:
            mixed[step - 1] = group_mix(step - 1, proj[step - 1][0])
        if 0 <= step - 2 < n_sub:
            finish(step - 2, mixed[step - 2], proj[step - 2][1])


def _const_spec(shape):
    zeros = (0,) * len(shape)
    return pl.BlockSpec(shape, lambda *_: zeros, pipeline_mode=pl.Buffered(1))


def _cast_specs(weights, n_steps):
    specs, shapes = [], []
    for w in weights:
        rows, cols = w.shape
        n_blocks = n_steps
        while rows % n_blocks or (rows // n_blocks) % BF16_ROWS:
            n_blocks //= 2
        assert n_blocks >= 1
        spec = pl.BlockSpec((rows // n_blocks, cols),
                            lambda i, last=n_blocks - 1: (jnp.minimum(i, last), 0))
        specs.append(spec)
        shapes.append(jax.ShapeDtypeStruct(w.shape, BF16))
    return specs, shapes


def _mix_in(x2d, mods, seq_len, g_pre, w_f, w_p, w_g, wpg_bd, pool_scale, w_po,
            cast_weights=(), dft_job=None):
    n_tok = x2d.shape[0]
    t = MIX_IN_TILE
    tiles_per_seq = seq_len // t
    halo_blocks_per_tile = t // HALO
    n_halo_blocks = n_tok // HALO
    row = lambda i: (i, 0)
    cast_specs, cast_shapes = _cast_specs(cast_weights, n_tok // t)
    job = dft_job(n_tok // t, lambda i: i) if dft_job else None
    job_in = job.in_specs if job else []
    job_out = [job.out_spec] if job else []
    job_shape = [job.out_shape] if job else []
    job_args = job.args if job else ()
    return pl.pallas_call(
        functools.partial(_mix_in_kernel, tiles_per_seq, seq_len, len(cast_weights),
                          job is not None),
        grid=(n_tok // t,),
        in_specs=[
            pl.BlockSpec((t, D_MODEL), row),
            pl.BlockSpec((HALO, D_MODEL),
                         lambda i: (jnp.maximum(i * halo_blocks_per_tile - 1, 0), 0)),
            pl.BlockSpec((HALO, D_MODEL),
                         lambda i: (jnp.minimum((i + 1) * halo_blocks_per_tile,
                                                n_halo_blocks - 1), 0)),
            pl.BlockSpec((1, N_ADA, D_MODEL), lambda i: (i // tiles_per_seq, 0, 0)),
            _const_spec((1, D_MODEL)),
            _const_spec((D_MODEL, BRANCH_WIDTH)),
            _const_spec((D_MODEL, BRANCH_WIDTH)),
            _const_spec((D_MODEL, 2 * D_MODEL)),
            _const_spec((N_GROUPS // 2, MXU_DIM, MXU_DIM)),
            _const_spec((1, BRANCH_WIDTH)),
            _const_spec((BRANCH_WIDTH, D_MODEL)),
        ] + cast_specs + job_in,
        out_specs=[
            pl.BlockSpec((t, BRANCH_WIDTH), row),
            pl.BlockSpec((t, D_MODEL), row),
            pl.BlockSpec((t, D_MODEL), row),
        ] + cast_specs + job_out,
        out_shape=[
            jax.ShapeDtypeStruct((n_tok, BRANCH_WIDTH), BF16),
            jax.ShapeDtypeStruct((n_tok, D_MODEL), BF16),
            jax.ShapeDtypeStruct((n_tok, D_MODEL), BF16),
        ] + cast_shapes + job_shape,
        compiler_params=pltpu.CompilerParams(
            dimension_semantics=("arbitrary",), vmem_limit_bytes=VMEM_LIMIT_BYTES),
        name="mix_in",
    )(x2d, x2d, x2d, mods, g_pre, w_f, w_p, w_g, wpg_bd, pool_scale, w_po, *cast_weights,
      *job_args)


class _Stage1Body:
    def __init__(self, x_ref, w_ref, o_ref):
        self.x_ref, self.w_ref, self.o_ref = x_ref, w_ref, o_ref

    def load(self):
        n1, c2, width = self.x_ref.shape
        self.xt = jnp.swapaxes(self.x_ref[...], 0, 1).reshape(
            n1 * c2 // SEQ_MINOR, SEQ_MINOR, width)

    def matmul(self):
        w = self.w_ref[...]
        self.z = jnp.stack([_dot(w, self.xt[g]).astype(BF16)
                            for g in range(self.xt.shape[0])])

    def store(self):
        n1, c2, width = self.x_ref.shape
        self.o_ref[...] = jnp.swapaxes(self.z.reshape(c2, 2 * n1, width), 0, 1)


def _dft_stage1_kernel(x_ref, w_ref, o_ref):
    body = _Stage1Body(x_ref, w_ref, o_ref)
    body.load()
    body.matmul()
    body.store()


def _dft_stage1(job_maker, n_steps):
    job = job_maker(n_steps, lambda i: i)
    return pl.pallas_call(
        _dft_stage1_kernel,
        grid=(n_steps,),
        in_specs=job.in_specs,
        out_specs=job.out_spec,
        out_shape=job.out_shape,
        compiler_params=pltpu.CompilerParams(
            dimension_semantics=("arbitrary",), vmem_limit_bytes=VMEM_LIMIT_BYTES),
        name="dft_stage1",
    )(*job.args)


class _Stage1Job:
    def __init__(self, u4, w_bd, n_steps, lin):
        batch, n1, n2, width = u4.shape
        assert n_steps % batch == 0
        chunks = n_steps // batch
        c2 = n2 // chunks
        assert n2 % chunks == 0 and c2 % BF16_ROWS == 0 and (c2 * n1) % SEQ_MINOR == 0
        index = lambda *g: (lin(*g) // chunks, 0, lin(*g) % chunks, 0)
        self.args = (u4, w_bd)
        self.in_specs = [pl.BlockSpec((None, n1, c2, width), index),
                         _const_spec((2 * SEQ_MINOR, SEQ_MINOR))]
        self.out_spec = pl.BlockSpec((None, 2 * n1, c2, width), index)
        self.out_shape = jax.ShapeDtypeStruct((batch, 2 * n1, n2, width), BF16)


def _dft_stage2_kernel(zr_ref, zi_ref, c2_ref, s2_ref, tc_ref, ts_ref, cs_ref, o_ref):
    c2 = c2_ref[...]
    s2 = s2_ref[...]
    base = pl.program_id(1) * ROW_BLOCKS_PER_STEP
    qs = []
    for j in range(ROW_BLOCKS_PER_STEP):
        tc = tc_ref[pl.ds(base + j, 1), :]
        ts = ts_ref[pl.ds(base + j, 1), :]
        cos_t = c2 * tc - s2 * ts
        sin_t = s2 * tc + c2 * ts
        lhs = jnp.concatenate(
            [jnp.concatenate([cos_t, sin_t], axis=1),
             jnp.concatenate([-sin_t, cos_t], axis=1)], axis=0).astype(BF16)
        z = jnp.concatenate([zr_ref[j], zi_ref[j]], axis=0)
        qs.append(_dot(lhs, z).astype(BF16))
    halves = []
    for pair in range(N_GROUPS // 2):
        rows = []
        for q in qs:
            parts = []
            for g in (2 * pair, 2 * pair + 1):
                lanes = slice(g * GROUP_DIM, (g + 1) * GROUP_DIM)
                parts += [q[:SEQ_MINOR, lanes], q[SEQ_MINOR:, lanes]]
            rows.append(jnp.concatenate(parts, axis=1))
        halves.append(_dot(jnp.concatenate(rows, axis=0), cs_ref[...]))
    y = jnp.concatenate(halves, axis=1).astype(o_ref.dtype)
    o_ref[...] = y.reshape(o_ref.shape)


def _dft_stage2(z4, c2, s2, tc, ts, cs_bd):
    batch, two_n1, n2, width = z4.shape
    n1 = two_n1 // 2
    rb = ROW_BLOCKS_PER_STEP
    assert n1 % rb == 0 and n2 == SEQ_MINOR
    steps = n1 // rb
    return pl.pallas_call(
        _dft_stage2_kernel,
        grid=(batch, steps),
        in_specs=[
            pl.BlockSpec((None, rb, n2, width), lambda b, i: (b, i, 0, 0)),
            pl.BlockSpec((None, rb, n2, width), lambda b, i: (b, steps + i, 0, 0)),
            _const_spec((SEQ_MINOR, SEQ_MINOR)),
            _const_spec((SEQ_MINOR, SEQ_MINOR)),
            _const_spec((n1, SEQ_MINOR)),
            _const_spec((n1, SEQ_MINOR)),
            _const_spec((2 * MXU_DIM, MXU_DIM)),
        ],
        out_specs=pl.BlockSpec((None, rb, n2, width), lambda b, i: (b, i, 0, 0)),
        out_shape=jax.ShapeDtypeStruct((batch, n1, n2, width), BF16),
        compiler_params=pltpu.CompilerParams(
            dimension_semantics=("arbitrary", "arbitrary"),
            vmem_limit_bytes=VMEM_LIMIT_BYTES),
        name="dft_stage2",
    )(z4, z4, c2, s2, tc, ts, cs_bd)


@functools.lru_cache(maxsize=None)
def _dft_tables(seq_len):
    n2 = SEQ_MINOR
    n1 = seq_len // n2
    reps = n2 // n1
    k = np.arange(n1, dtype=np.float64)
    ang1 = 2.0 * np.pi * np.outer(k, k) / n1
    w1 = np.concatenate([np.cos(ang1), -np.sin(ang1)], axis=0)
    w_bd = np.kron(np.eye(reps), w1)
    m = np.arange(n2, dtype=np.float64)
    ang2 = 2.0 * np.pi * np.outer(m, m) / n2
    angt = 2.0 * np.pi * np.outer(k, m) / seq_len
    as32 = lambda a: np.asarray(a, dtype=np.float32)
    return (as32(w_bd), as32(np.cos(ang2)), as32(np.sin(ang2)),
            as32(np.cos(angt)), as32(np.sin(angt)))


@functools.lru_cache(maxsize=None)
def _channel_dft_table(seq_len):
    c = np.arange(GROUP_DIM, dtype=np.float64)
    ang = 2.0 * np.pi * np.outer(c, c) / GROUP_DIM
    norm = 1.0 / np.sqrt(float(seq_len) * GROUP_DIM)
    cs = np.concatenate([np.cos(ang), np.sin(ang)], axis=0) * norm
    return np.asarray(np.kron(np.eye(2), cs), dtype=np.float32)


def _stage1_job(u_f, batch, seq_len):
    n1 = seq_len // SEQ_MINOR
    w_bd = jnp.asarray(_dft_tables(seq_len)[0]).astype(BF16)
    u4 = u_f.reshape(batch, n1, SEQ_MINOR, BRANCH_WIDTH)
    return lambda n_steps, lin: _Stage1Job(u4, w_bd, n_steps, lin)


def _stage2(z, seq_len):
    _, c2, s2, tc, ts = _dft_tables(seq_len)
    cs_bd = jnp.asarray(_channel_dft_table(seq_len)).astype(BF16)
    return _dft_stage2(z, jnp.asarray(c2), jnp.asarray(s2), jnp.asarray(tc), jnp.asarray(ts),
                       cs_bd)


def _mix_out_kernel(x_ref, y_ref, gf_ref, p_ref, mod_ref, gpm_ref, gpf_ref, gqf_ref,
                    wfo_ref, wout_ref, wgate_ref, wup_ref, wdown_ref, o_ref):
    mod = mod_ref[0]
    n_sub = MIX_OUT_TILE // MIX_OUT_SUB
    gate1 = mod[2:3, :] * gpm_ref[...]
    scale2 = gpf_ref[...] * (1.0 + mod[4:5, :])
    shift2 = mod[3:4, :]
    gate2 = mod[5:6, :] * gqf_ref[...]
    k2_sub = MIX_OUT_SUB // K1_PER_TILE
    subs = [pl.ds(s * k2_sub, k2_sub) for s in range(n_sub)]

    def rows_of(ref, k2s):
        return ref[k2s].reshape(MIX_OUT_SUB, ref.shape[-1])

    def up_proj(h2, chunk):
        cols = pl.ds(*chunk)
        return _dot(h2, wgate_ref[:, cols]), _dot(h2, wup_ref[:, cols])

    def down_proj(ab, chunk):
        a, b = ab
        return _dot((a * _sigmoid(a) * b).astype(BF16), wdown_ref[pl.ds(*chunk), :])

    def front(group):
        ys = [jnp.swapaxes(y_ref[:, k2s, :], 0, 1).reshape(MIX_OUT_SUB, BRANCH_WIDTH)
              for k2s in group]
        y_fs = [_dot(y, wfo_ref[...]) for y in ys]
        ms = [_dot((rows_of(gf_ref, k2s).astype(F32) * y_f
                    + rows_of(p_ref, k2s).astype(F32)).astype(BF16), wout_ref[...])
              for k2s, y_f in zip(group, y_fs)]
        x1s = [rows_of(x_ref, k2s) + gate1 * (m * _rms_scale(m))
               for k2s, m in zip(group, ms)]
        h2s = [(x1 * _rms_scale(x1) * scale2 + shift2).astype(BF16) for x1 in x1s]
        return x1s, h2s

    def ffn(h2s, before_last_down):
        zs = [None] * len(h2s)
        pending = [up_proj(h2, FF_CHUNKS[0]) for h2 in h2s]
        result = None
        for c, chunk in enumerate(FF_CHUNKS):
            last = c + 1 == len(FF_CHUNKS)
            nxt = None if last else [up_proj(h2, FF_CHUNKS[c + 1]) for h2 in h2s]
            if last:
                result = before_last_down()
            for s in range(len(h2s)):
                part = down_proj(pending[s], chunk)
                zs[s] = part if zs[s] is None else zs[s] + part
            pending = nxt
        return zs, result

    groups = [subs[i:i + SUBS_PER_GROUP] for i in range(0, n_sub, SUBS_PER_GROUP)]
    x1s, h2s = front(groups[0])
    for gi, group in enumerate(groups):
        following = groups[gi + 1] if gi + 1 < len(groups) else None
        zs, nxt = ffn(h2s, (lambda f=following: front(f)) if following else (lambda: None))
        for k2s, x1, z in zip(group, x1s, zs):
            out = x1 + gate2 * (z * _rms_scale(z))
            o_ref[k2s] = out.reshape(k2_sub, K1_PER_TILE, D_MODEL)
        if nxt is not None:
            x1s, h2s = nxt


def _mix_out(x, y4, g_f, p, mods, g_post_mix, g_pre_ffn, g_post_ffn,
             w_fo, w_out, w_gate, w_up, w_down):
    batch, seq_len, d = x.shape
    n1 = y4.shape[1]
    n2 = SEQ_MINOR
    k1t, k2t = K1_PER_TILE, K2_PER_TILE
    assert n1 % k1t == 0 and n2 % k2t == 0 and k1t * k2t == MIX_OUT_TILE
    by_k = lambda a: a.reshape(batch, n2, n1 // k1t, k1t, a.shape[-1])
    tok_spec = pl.BlockSpec((None, k2t, None, k1t, d), lambda b, a, t: (b, a, t, 0, 0))
    return pl.pallas_call(
        _mix_out_kernel,
        grid=(batch, n2 // k2t, n1 // k1t),
        in_specs=[
            tok_spec,
            pl.BlockSpec((None, k1t, k2t, BRANCH_WIDTH), lambda b, a, t: (b, t, a, 0)),
            tok_spec,
            tok_spec,
            pl.BlockSpec((1, N_ADA, D_MODEL), lambda b, a, t: (b, 0, 0)),
            _const_spec((1, D_MODEL)),
            _const_spec((1, D_MODEL)),
            _const_spec((1, D_MODEL)),
            _const_spec((BRANCH_WIDTH, D_MODEL)),
            _const_spec((D_MODEL, D_MODEL)),
            _const_spec((D_MODEL, D_FF)),
            _const_spec((D_MODEL, D_FF)),
            _const_spec((D_FF, D_MODEL)),
        ],
        out_specs=tok_spec,
        out_shape=jax.ShapeDtypeStruct((batch, n2, n1 // k1t, k1t, d), F32),
        compiler_params=pltpu.CompilerParams(
            dimension_semantics=("arbitrary", "arbitrary", "arbitrary"),
            vmem_limit_bytes=VMEM_LIMIT_BYTES),
        name="mix_out",
    )(by_k(x), y4, by_k(g_f), by_k(p), mods, g_post_mix, g_pre_ffn, g_post_ffn,
      w_fo, w_out, w_gate, w_up, w_down).reshape(batch, seq_len, d)


def _layer_pair(x_a, mods_a, x_b, mods_b, early, gains, late):
    (ba, sa, d), (bb, sb, _) = x_a.shape, x_b.shape
    u_a, gf_a, p_a, *late = _mix_in(x_a.reshape(ba * sa, d), mods_a, sa, *early,
                                    cast_weights=late)
    u_b, gf_b, p_b, z_a = _mix_in(x_b.reshape(bb * sb, d), mods_b, sb, *early,
                                  dft_job=_stage1_job(u_a, ba, sa))
    out_a = _mix_out(x_a, _stage2(z_a, sa), gf_a, p_a, mods_a, *gains, *late)
    z_b = _dft_stage1(_stage1_job(u_b, bb, sb), bb)
    out_b = _mix_out(x_b, _stage2(z_b, sb), gf_b, p_b, mods_b, *gains, *late)
    return out_a, out_b


def kernel(x_prompt, x_sample, c_prompt, c_sample, w_ada, b_ada, g_pre_mix, w_in, w_fo, w_pg,
           pool_scale, w_po, w_out, g_post_mix, g_pre_ffn, w_gate, w_up, w_down, g_post_ffn):
    depth = w_ada.shape[0]
    y_prompt, y_sample = x_prompt, x_sample
    bp, bs = c_prompt.shape[0], c_sample.shape[0]
    pad_rows = -(bp + bs) % BF16_ROWS
    c_pad = jnp.concatenate(
        [c_prompt, c_sample, jnp.zeros((pad_rows, D_MODEL), F32)], axis=0)
    for l in range(depth):
        ada = _ada(c_pad, w_ada[l], b_ada[l][None, :])
        mods_p = ada[:bp].reshape(bp, N_ADA, D_MODEL)
        mods_s = ada[bp:bp + bs].reshape(bs, N_ADA, D_MODEL)
        w_in_l = w_in[l].astype(BF16)
        zero_blk = jnp.zeros((GROUP_DIM, GROUP_DIM), F32)
        wpg_bd = jnp.stack([
            jnp.block([[w_pg[l, 2 * pr], zero_blk], [zero_blk, w_pg[l, 2 * pr + 1]]])
            for pr in range(N_GROUPS // 2)]).astype(BF16)
        early = (
            g_pre_mix[l][None, :],
            w_in_l[:, :BRANCH_WIDTH],
            w_in_l[:, BRANCH_WIDTH:2 * BRANCH_WIDTH],
            w_in_l[:, 2 * BRANCH_WIDTH:],
            wpg_bd,
            pool_scale[l][None, :],
            w_po[l].astype(BF16),
        )
        gains = (g_post_mix[l][None, :], g_pre_ffn[l][None, :], g_post_ffn[l][None, :])
        late = (w_fo[l], w_out[l], w_gate[l], w_up[l], w_down[l])
        y_prompt, y_sample = _layer_pair(y_prompt, mods_p, y_sample, mods_s, early, gains, late)
    return (y_prompt, y_sample)
```

```python
import functools

import numpy as np
import jax
import jax.numpy as jnp
from jax import lax
from jax.experimental import pallas as pl
from jax.experimental.pallas import tpu as pltpu

D_MODEL = 1024
N_GROUPS = 4
GROUP_DIM = 128
BRANCH_WIDTH = N_GROUPS * GROUP_DIM
POOL_WINDOWS = (2, 4, 8, 16)
D_FF = 2816
N_ADA = 6
EPS = 1e-6

F32_ROWS = 8
BF16_ROWS = 16
MXU_DIM = 256
VMEM_LIMIT_BYTES = 60 * 1024 * 1024

SEQ_MINOR = 128
HALO = BF16_ROWS
SUB_TILE = 256
MIX_IN_TILE = 1024
MIX_OUT_TILE = 512
MIX_OUT_SUB = 256
SUBS_PER_GROUP = 2
K1_PER_TILE = BF16_ROWS
K2_PER_TILE = MIX_OUT_TILE // K1_PER_TILE
ROW_BLOCKS_PER_STEP = 16
ADA_COLS = 1024
FF_CHUNKS = ((0, 1024), (1024, 1024), (2048, 768))

BF16 = jnp.bfloat16
F32 = jnp.float32


def _dot(a, b):
    return jnp.dot(a, b, preferred_element_type=F32)


def _sigmoid(v):
    return 1.0 / (1.0 + jnp.exp(-v))


def _rms_scale(v):
    return lax.rsqrt(jnp.mean(v * v, axis=-1, keepdims=True) + EPS)


def _ada_kernel(c_ref, w_ref, b_ref, o_ref):
    c = c_ref[...]
    s = (c * _sigmoid(c)).astype(BF16)
    o_ref[...] = _dot(s, w_ref[...].astype(BF16)) + b_ref[...]


def _ada(c_pad, w_ada, b_ada):
    rows, d = c_pad.shape
    n_out = w_ada.shape[1]
    return pl.pallas_call(
        _ada_kernel,
        grid=(n_out // ADA_COLS,),
        in_specs=[
            pl.BlockSpec((rows, d), lambda j: (0, 0)),
            pl.BlockSpec((d, ADA_COLS), lambda j: (0, j)),
            pl.BlockSpec((1, ADA_COLS), lambda j: (0, j)),
        ],
        out_specs=pl.BlockSpec((rows, ADA_COLS), lambda j: (0, j)),
        out_shape=jax.ShapeDtypeStruct((rows, n_out), F32),
        compiler_params=pltpu.CompilerParams(
            dimension_semantics=("arbitrary",), vmem_limit_bytes=VMEM_LIMIT_BYTES),
        name="ada",
    )(c_pad, w_ada, b_ada)


def _window_sum(ext, w):
    n = ext.shape[0]
    s = ext
    k = 1
    while k < w // 2:
        s = s + pltpu.roll(s, n - k, axis=0)
        k *= 2
    return s + pltpu.roll(s, w // 2, axis=0)


def _mix_in_kernel(tiles_per_seq, seq_len, n_cast, has_dft_job,
                   x_ref, xp_ref, xn_ref, mod_ref, g_ref, wf_ref, wp_ref, wg_ref,
                   wpg_ref, ps_ref, wpo_ref, *rest):
    n_extra_in = n_cast + 2 * has_dft_job
    cast_in, dft_in = rest[:n_cast], rest[n_cast:n_extra_in]
    uf_ref, gf_ref, p_ref = rest[n_extra_in:n_extra_in + 3]
    cast_out = rest[n_extra_in + 3:n_extra_in + 3 + n_cast]
    dft = _Stage1Body(*dft_in, rest[-1]) if has_dft_job else None
    if dft:
        dft.load()
    for src, dst in zip(cast_in, cast_out):
        dst[...] = src[...].astype(dst.dtype)
    t = MIX_IN_TILE
    n_sub = t // SUB_TILE
    ti = lax.rem(pl.program_id(0), tiles_per_seq)
    mod = mod_ref[0]
    scale = g_ref[...] * (1.0 + mod[1:2, :])
    shift = mod[0:1, :]

    def normed(v):
        return v * _rms_scale(v) * scale + shift

    ts = SUB_TILE
    subs = [pl.ds(s * ts, ts) for s in range(n_sub)]
    hs = [normed(x_ref[rows, :]).astype(BF16) for rows in subs]
    h_prev = jnp.where(ti == 0, 0.0, normed(xp_ref[...])).astype(BF16)
    h_next = jnp.where(ti == tiles_per_seq - 1, 0.0, normed(xn_ref[...])).astype(BF16)
    befores = [h_prev] + [h[ts - HALO:] for h in hs[:-1]]
    afters = [h[:HALO] for h in hs[1:]] + [h_next]


    slab = F32_ROWS
    row = lax.broadcasted_iota(jnp.int32, (slab, GROUP_DIM), 0)

    def pool_diff(up, first_pos):
        diffs = []
        for g, w in enumerate(POOL_WINDOWS):
            assert w & (w - 1) == 0 and w // 2 <= slab
            left = w // 2
            right = w - 1 - left
            ext = up[:, g * GROUP_DIM:(g + 1) * GROUP_DIM]
            win = _window_sum(ext, w)[HALO:HALO + ts]
            u = ext[HALO:HALO + ts]

            def count(pos):
                return (jnp.minimum(pos + right, seq_len - 1)
                        - jnp.maximum(pos - left, 0) + 1).astype(F32)

            diffs.append(jnp.concatenate([
                win[:slab] / count(first_pos + row) - u[:slab],
                win[slab:ts - slab] * (1.0 / w) - u[slab:ts - slab],
                win[ts - slab:] / count(first_pos + (ts - slab) + row) - u[ts - slab:],
            ], axis=0))
        return diffs

    def project(s):
        up = _dot(jnp.concatenate([befores[s], hs[s], afters[s]], axis=0), wp_ref[...])
        uf_ref[subs[s], :] = _dot(hs[s], wf_ref[...]).astype(uf_ref.dtype)
        return up, _dot(hs[s], wg_ref[...])

    def group_mix(s, up):
        diffs = pool_diff(up, ti * t + s * ts)
        ys = []
        for pair in range(N_GROUPS // 2):
            d2 = jnp.concatenate(diffs[2 * pair:2 * pair + 2], axis=1).astype(BF16)
            ys.append(_dot(d2, wpg_ref[pair]))
        return (jnp.concatenate(ys, axis=1) * ps_ref[...]).astype(BF16)

    def finish(s, y, gate_logits):
        g = _sigmoid(gate_logits)
        gf_ref[subs[s], :] = g[:, :D_MODEL].astype(gf_ref.dtype)
        p_ref[subs[s], :] = (g[:, D_MODEL:] * _dot(y, wpo_ref[...])).astype(p_ref.dtype)

    proj, mixed = {}, {}
    for step in range(n_sub + 2):
        if step < n_sub:
            proj[step] = project(step)
        if dft and step == 0:
            dft.matmul()
            dft.store()
        if 0 <= step - 1 < n_sub:
            mixed[step - 1] = group_mix(step - 1, proj[step - 1][0])
        if 0 <= step - 2 < n_sub:
            finish(step - 2, mixed[step - 2], proj[step - 2][1])


def _const_spec(shape):
    zeros = (0,) * len(shape)
    return pl.BlockSpec(shape, lambda *_: zeros, pipeline_mode=pl.Buffered(1))


def _cast_specs(weights, n_steps):
    specs, shapes = [], []
    for w in weights:
        rows, cols = w.shape
        n_blocks = n_steps
        while rows % n_blocks or (rows // n_blocks) % BF16_ROWS:
            n_blocks //= 2
        assert n_blocks >= 1
        spec = pl.BlockSpec((rows // n_blocks, cols),
                            lambda i, last=n_blocks - 1: (jnp.minimum(i, last), 0))
        specs.append(spec)
        shapes.append(jax.ShapeDtypeStruct(w.shape, BF16))
    return specs, shapes


def _mix_in(x2d, mods, seq_len, g_pre, w_f, w_p, w_g, wpg_bd, pool_scale, w_po,
            cast_weights=(), dft_job=None):
    n_tok = x2d.shape[0]
    t = MIX_IN_TILE
    tiles_per_seq = seq_len // t
    halo_blocks_per_tile = t // HALO
    n_halo_blocks = n_tok // HALO
    row = lambda i: (i, 0)
    cast_specs, cast_shapes = _cast_specs(cast_weights, n_tok // t)
    job = dft_job(n_tok // t, lambda i: i) if dft_job else None
    job_in = job.in_specs if job else []
    job_out = [job.out_spec] if job else []
    job_shape = [job.out_shape] if job else []
    job_args = job.args if job else ()
    return pl.pallas_call(
        functools.partial(_mix_in_kernel, tiles_per_seq, seq_len, len(cast_weights),
                          job is not None),
        grid=(n_tok // t,),
        in_specs=[
            pl.BlockSpec((t, D_MODEL), row),
            pl.BlockSpec((HALO, D_MODEL),
                         lambda i: (jnp.maximum(i * halo_blocks_per_tile - 1, 0), 0)),
            pl.BlockSpec((HALO, D_MODEL),
                         lambda i: (jnp.minimum((i + 1) * halo_blocks_per_tile,
                                                n_halo_blocks - 1), 0)),
            pl.BlockSpec((1, N_ADA, D_MODEL), lambda i: (i // tiles_per_seq, 0, 0)),
            _const_spec((1, D_MODEL)),
            _const_spec((D_MODEL, BRANCH_WIDTH)),
            _const_spec((D_MODEL, BRANCH_WIDTH)),
            _const_spec((D_MODEL, 2 * D_MODEL)),
            _const_spec((N_GROUPS // 2, MXU_DIM, MXU_DIM)),
            _const_spec((1, BRANCH_WIDTH)),
            _const_spec((BRANCH_WIDTH, D_MODEL)),
        ] + cast_specs + job_in,
        out_specs=[
            pl.BlockSpec((t, BRANCH_WIDTH), row),
            pl.BlockSpec((t, D_MODEL), row),
            pl.BlockSpec((t, D_MODEL), row),
        ] + cast_specs + job_out,
        out_shape=[
            jax.ShapeDtypeStruct((n_tok, BRANCH_WIDTH), BF16),
            jax.ShapeDtypeStruct((n_tok, D_MODEL), BF16),
            jax.ShapeDtypeStruct((n_tok, D_MODEL), BF16),
        ] + cast_shapes + job_shape,
        compiler_params=pltpu.CompilerParams(
            dimension_semantics=("arbitrary",), vmem_limit_bytes=VMEM_LIMIT_BYTES),
        name="mix_in",
    )(x2d, x2d, x2d, mods, g_pre, w_f, w_p, w_g, wpg_bd, pool_scale, w_po, *cast_weights,
      *job_args)


class _Stage1Body:
    def __init__(self, x_ref, w_ref, o_ref):
        self.x_ref, self.w_ref, self.o_ref = x_ref, w_ref, o_ref

    def load(self):
        n1, c2, width = self.x_ref.shape
        self.xt = jnp.swapaxes(self.x_ref[...], 0, 1).reshape(
            n1 * c2 // SEQ_MINOR, SEQ_MINOR, width)

    def matmul(self):
        w = self.w_ref[...]
        self.z = jnp.stack([_dot(w, self.xt[g]).astype(BF16)
                            for g in range(self.xt.shape[0])])

    def store(self):
        n1, c2, width = self.x_ref.shape
        self.o_ref[...] = jnp.swapaxes(self.z.reshape(c2, n1, width), 0, 1)


def _dft_stage1_kernel(x_ref, w_ref, o_ref):
    body = _Stage1Body(x_ref, w_ref, o_ref)
    body.load()
    body.matmul()
    body.store()


def _dft_stage1(job_maker, n_steps):
    job = job_maker(n_steps, lambda i: i)
    return pl.pallas_call(
        _dft_stage1_kernel,
        grid=(n_steps,),
        in_specs=job.in_specs,
        out_specs=job.out_spec,
        out_shape=job.out_shape,
        compiler_params=pltpu.CompilerParams(
            dimension_semantics=("arbitrary",), vmem_limit_bytes=VMEM_LIMIT_BYTES),
        name="dft_stage1",
    )(*job.args)


class _Stage1Job:
    def __init__(self, u4, w_bd, n_steps, lin):
        batch, n1, n2, width = u4.shape
        assert n_steps % batch == 0
        chunks = n_steps // batch
        c2 = n2 // chunks
        assert n2 % chunks == 0 and c2 % BF16_ROWS == 0 and (c2 * n1) % SEQ_MINOR == 0
        index = lambda *g: (lin(*g) // chunks, 0, lin(*g) % chunks, 0)
        self.args = (u4, w_bd)
        self.in_specs = [pl.BlockSpec((None, n1, c2, width), index),
                         _const_spec((SEQ_MINOR, SEQ_MINOR))]
        self.out_spec = pl.BlockSpec((None, n1, c2, width), index)
        self.out_shape = jax.ShapeDtypeStruct((batch, n1, n2, width), BF16)


def _packed_windows(n1, step):
    rb, half = ROW_BLOCKS_PER_STEP, n1 // 2
    upper = rb * step >= max(half, rb)
    re_start = jnp.where(upper, n1 - rb * step - (rb - 1), rb * step)
    im_start = jnp.minimum(jnp.where(upper, re_start + half, half + rb * step), n1 - rb)
    return re_start, im_start


def _dft_stage2_kernel(n1, zr_ref, zi_ref, c2_ref, s2_ref, tc_ref, ts_ref, cs_ref, o_ref):
    c2 = c2_ref[...]
    s2 = s2_ref[...]
    half = n1 // 2
    step = pl.program_id(1)
    re_start, im_start = _packed_windows(n1, step)
    qs = []
    for j in range(ROW_BLOCKS_PER_STEP):
        k1 = step * ROW_BLOCKS_PER_STEP + j
        mirrored = k1 > half
        kp = jnp.where(mirrored, n1 - k1, k1)
        real_only = (kp == 0) | (kp == half)
        sign = jnp.where(real_only, 0.0, jnp.where(mirrored, -1.0, 1.0))
        tc = tc_ref[pl.ds(k1, 1), :]
        ts = ts_ref[pl.ds(k1, 1), :]
        cos_t = c2 * tc - s2 * ts
        sin_t = s2 * tc + c2 * ts
        lhs = jnp.concatenate(
            [jnp.concatenate([cos_t, sign * sin_t], axis=1),
             jnp.concatenate([-sin_t, sign * cos_t], axis=1)], axis=0).astype(BF16)
        z = jnp.concatenate(
            [zr_ref[0, kp - re_start],
             zi_ref[0, jnp.where(real_only, 0, half + kp - im_start)]], axis=0)
        qs.append(_dot(lhs, z).astype(BF16))
    halves = []
    for pair in range(N_GROUPS // 2):
        rows = []
        for q in qs:
            parts = []
            for g in (2 * pair, 2 * pair + 1):
                lanes = slice(g * GROUP_DIM, (g + 1) * GROUP_DIM)
                parts += [q[:SEQ_MINOR, lanes], q[SEQ_MINOR:, lanes]]
            rows.append(jnp.concatenate(parts, axis=1))
        halves.append(_dot(jnp.concatenate(rows, axis=0), cs_ref[...]))
    y = jnp.concatenate(halves, axis=1).astype(o_ref.dtype)
    o_ref[...] = y.reshape(o_ref.shape)


def _dft_stage2(z4, c2, s2, tc, ts, cs_bd):
    batch, n1, n2, width = z4.shape
    rb = ROW_BLOCKS_PER_STEP
    assert n1 % rb == 0 and n2 == SEQ_MINOR
    window = tuple(pl.Element(n) for n in (1, rb, n2, width))
    return pl.pallas_call(
        functools.partial(_dft_stage2_kernel, n1),
        grid=(batch, n1 // rb),
        in_specs=[
            pl.BlockSpec(window, lambda b, i: (b, _packed_windows(n1, i)[0], 0, 0)),
            pl.BlockSpec(window, lambda b, i: (b, _packed_windows(n1, i)[1], 0, 0)),
            _const_spec((SEQ_MINOR, SEQ_MINOR)),
            _const_spec((SEQ_MINOR, SEQ_MINOR)),
            _const_spec((n1, SEQ_MINOR)),
            _const_spec((n1, SEQ_MINOR)),
            _const_spec((2 * MXU_DIM, MXU_DIM)),
        ],
        out_specs=pl.BlockSpec((None, rb, n2, width), lambda b, i: (b, i, 0, 0)),
        out_shape=jax.ShapeDtypeStruct((batch, n1, n2, width), BF16),
        compiler_params=pltpu.CompilerParams(
            dimension_semantics=("arbitrary", "arbitrary"),
            vmem_limit_bytes=VMEM_LIMIT_BYTES),
        name="dft_stage2",
    )(z4, z4, c2, s2, tc, ts, cs_bd)


@functools.lru_cache(maxsize=None)
def _dft_tables(seq_len):
    n2 = SEQ_MINOR
    n1 = seq_len // n2
    reps = n2 // n1
    k = np.arange(n1, dtype=np.float64)
    ang1 = 2.0 * np.pi * np.outer(k, k) / n1
    half = n1 // 2
    w1 = np.concatenate([np.cos(ang1)[:half + 1], -np.sin(ang1)[1:half]], axis=0)
    w_bd = np.kron(np.eye(reps), w1)
    m = np.arange(n2, dtype=np.float64)
    ang2 = 2.0 * np.pi * np.outer(m, m) / n2
    angt = 2.0 * np.pi * np.outer(k, m) / seq_len
    as32 = lambda a: np.asarray(a, dtype=np.float32)
    return (as32(w_bd), as32(np.cos(ang2)), as32(np.sin(ang2)),
            as32(np.cos(angt)), as32(np.sin(angt)))


@functools.lru_cache(maxsize=None)
def _channel_dft_table(seq_len):
    c = np.arange(GROUP_DIM, dtype=np.float64)
    ang = 2.0 * np.pi * np.outer(c, c) / GROUP_DIM
    norm = 1.0 / np.sqrt(float(seq_len) * GROUP_DIM)
    cs = np.concatenate([np.cos(ang), np.sin(ang)], axis=0) * norm
    return np.asarray(np.kron(np.eye(2), cs), dtype=np.float32)


def _stage1_job(u_f, batch, seq_len):
    n1 = seq_len // SEQ_MINOR
    w_bd = jnp.asarray(_dft_tables(seq_len)[0]).astype(BF16)
    u4 = u_f.reshape(batch, n1, SEQ_MINOR, BRANCH_WIDTH)
    return lambda n_steps, lin: _Stage1Job(u4, w_bd, n_steps, lin)


def _stage2(z, seq_len):
    _, c2, s2, tc, ts = _dft_tables(seq_len)
    cs_bd = jnp.asarray(_channel_dft_table(seq_len)).astype(BF16)
    return _dft_stage2(z, jnp.asarray(c2), jnp.asarray(s2), jnp.asarray(tc), jnp.asarray(ts),
                       cs_bd)


def _mix_out_kernel(x_ref, y_ref, gf_ref, p_ref, mod_ref, gpm_ref, gpf_ref, gqf_ref,
                    wfo_ref, wout_ref, wgate_ref, wup_ref, wdown_ref, o_ref):
    mod = mod_ref[0]
    n_sub = MIX_OUT_TILE // MIX_OUT_SUB
    gate1 = mod[2:3, :] * gpm_ref[...]
    scale2 = gpf_ref[...] * (1.0 + mod[4:5, :])
    shift2 = mod[3:4, :]
    gate2 = mod[5:6, :] * gqf_ref[...]
    k2_sub = MIX_OUT_SUB // K1_PER_TILE
    subs = [pl.ds(s * k2_sub, k2_sub) for s in range(n_sub)]

    def rows_of(ref, k2s):
        return ref[k2s].reshape(MIX_OUT_SUB, ref.shape[-1])

    def up_proj(h2, chunk):
        cols = pl.ds(*chunk)
        return _dot(h2, wgate_ref[:, cols]), _dot(h2, wup_ref[:, cols])

    def down_proj(ab, chunk):
        a, b = ab
        return _dot((a * _sigmoid(a) * b).astype(BF16), wdown_ref[pl.ds(*chunk), :])

    def front(group):
        ys = [jnp.swapaxes(y_ref[:, k2s, :], 0, 1).reshape(MIX_OUT_SUB, BRANCH_WIDTH)
              for k2s in group]
        y_fs = [_dot(y, wfo_ref[...]) for y in ys]
        ms = [_dot((rows_of(gf_ref, k2s).astype(F32) * y_f
                    + rows_of(p_ref, k2s).astype(F32)).astype(BF16), wout_ref[...])
              for k2s, y_f in zip(group, y_fs)]
        x1s = [rows_of(x_ref, k2s) + gate1 * (m * _rms_scale(m))
               for k2s, m in zip(group, ms)]
        h2s = [(x1 * _rms_scale(x1) * scale2 + shift2).astype(BF16) for x1 in x1s]
        return x1s, h2s

    def ffn(h2s, before_last_down):
        zs = [None] * len(h2s)
        pending = [up_proj(h2, FF_CHUNKS[0]) for h2 in h2s]
        result = None
        for c, chunk in enumerate(FF_CHUNKS):
            last = c + 1 == len(FF_CHUNKS)
            nxt = None if last else [up_proj(h2, FF_CHUNKS[c + 1]) for h2 in h2s]
            if last:
                result = before_last_down()
            for s in range(len(h2s)):
                part = down_proj(pending[s], chunk)
                zs[s] = part if zs[s] is None else zs[s] + part
            pending = nxt
        return zs, result

    groups = [subs[i:i + SUBS_PER_GROUP] for i in range(0, n_sub, SUBS_PER_GROUP)]
    x1s, h2s = front(groups[0])
    for gi, group in enumerate(groups):
        following = groups[gi + 1] if gi + 1 < len(groups) else None
        zs, nxt = ffn(h2s, (lambda f=following: front(f)) if following else (lambda: None))
        for k2s, x1, z in zip(group, x1s, zs):
            out = x1 + gate2 * (z * _rms_scale(z))
            o_ref[k2s] = out.reshape(k2_sub, K1_PER_TILE, D_MODEL)
        if nxt is not None:
            x1s, h2s = nxt


def _mix_out(x, y4, g_f, p, mods, g_post_mix, g_pre_ffn, g_post_ffn,
             w_fo, w_out, w_gate, w_up, w_down):
    batch, seq_len, d = x.shape
    n1 = y4.shape[1]
    n2 = SEQ_MINOR
    k1t, k2t = K1_PER_TILE, K2_PER_TILE
    assert n1 % k1t == 0 and n2 % k2t == 0 and k1t * k2t == MIX_OUT_TILE
    by_k = lambda a: a.reshape(batch, n2, n1 // k1t, k1t, a.shape[-1])
    tok_spec = pl.BlockSpec((None, k2t, None, k1t, d), lambda b, a, t: (b, a, t, 0, 0))
    return pl.pallas_call(
        _mix_out_kernel,
        grid=(batch, n2 // k2t, n1 // k1t),
        in_specs=[
            tok_spec,
            pl.BlockSpec((None, k1t, k2t, BRANCH_WIDTH), lambda b, a, t: (b, t, a, 0)),
            tok_spec,
            tok_spec,
            pl.BlockSpec((1, N_ADA, D_MODEL), lambda b, a, t: (b, 0, 0)),
            _const_spec((1, D_MODEL)),
            _const_spec((1, D_MODEL)),
            _const_spec((1, D_MODEL)),
            _const_spec((BRANCH_WIDTH, D_MODEL)),
            _const_spec((D_MODEL, D_MODEL)),
            _const_spec((D_MODEL, D_FF)),
            _const_spec((D_MODEL, D_FF)),
            _const_spec((D_FF, D_MODEL)),
        ],
        out_specs=tok_spec,
        out_shape=jax.ShapeDtypeStruct((batch, n2, n1 // k1t, k1t, d), F32),
        compiler_params=pltpu.CompilerParams(
            dimension_semantics=("arbitrary", "arbitrary", "arbitrary"),
            vmem_limit_bytes=VMEM_LIMIT_BYTES),
        name="mix_out",
    )(by_k(x), y4, by_k(g_f), by_k(p), mods, g_post_mix, g_pre_ffn, g_post_ffn,
      w_fo, w_out, w_gate, w_up, w_down).reshape(batch, seq_len, d)


def _layer_pair(x_a, mods_a, x_b, mods_b, early, gains, late):
    (ba, sa, d), (bb, sb, _) = x_a.shape, x_b.shape
    u_a, gf_a, p_a, *late = _mix_in(x_a.reshape(ba * sa, d), mods_a, sa, *early,
                                    cast_weights=late)
    u_b, gf_b, p_b, z_a = _mix_in(x_b.reshape(bb * sb, d), mods_b, sb, *early,
                                  dft_job=_stage1_job(u_a, ba, sa))
    out_a = _mix_out(x_a, _stage2(z_a, sa), gf_a, p_a, mods_a, *gains, *late)
    z_b = _dft_stage1(_stage1_job(u_b, bb, sb), bb)
    out_b = _mix_out(x_b, _stage2(z_b, sb), gf_b, p_b, mods_b, *gains, *late)
    return out_a, out_b


def kernel(x_prompt, x_sample, c_prompt, c_sample, w_ada, b_ada, g_pre_mix, w_in, w_fo, w_pg,
           pool_scale, w_po, w_out, g_post_mix, g_pre_ffn, w_gate, w_up, w_down, g_post_ffn):
    depth = w_ada.shape[0]
    y_prompt, y_sample = x_prompt, x_sample
    bp, bs = c_prompt.shape[0], c_sample.shape[0]
    pad_rows = -(bp + bs) % BF16_ROWS
    c_pad = jnp.concatenate(
        [c_prompt, c_sample, jnp.zeros((pad_rows, D_MODEL), F32)], axis=0)
    for l in range(depth):
        ada = _ada(c_pad, w_ada[l], b_ada[l][None, :])
        mods_p = ada[:bp].reshape(bp, N_ADA, D_MODEL)
        mods_s = ada[bp:bp + bs].reshape(bs, N_ADA, D_MODEL)
        w_in_l = w_in[l].astype(BF16)
        zero_blk = jnp.zeros((GROUP_DIM, GROUP_DIM), F32)
        wpg_bd = jnp.stack([
            jnp.block([[w_pg[l, 2 * pr], zero_blk], [zero_blk, w_pg[l, 2 * pr + 1]]])
            for pr in range(N_GROUPS // 2)]).astype(BF16)
        early = (
            g_pre_mix[l][None, :],
            w_in_l[:, :BRANCH_WIDTH],
            w_in_l[:, BRANCH_WIDTH:2 * BRANCH_WIDTH],
            w_in_l[:, 2 * BRANCH_WIDTH:],
            wpg_bd,
            pool_scale[l][None, :],
            w_po[l].astype(BF16),
        )
        gains = (g_post_mix[l][None, :], g_pre_ffn[l][None, :], g_post_ffn[l][None, :])
        late = (w_fo[l], w_out[l], w_gate[l], w_up[l], w_down[l])
        y_prompt, y_sample = _layer_pair(y_prompt, mods_p, y_sample, mods_s, early, gains, late)
    return (y_prompt, y_sample)
```

```python
import functools

import numpy as np
import jax
import jax.numpy as jnp
from jax import lax
from jax.experimental import pallas as pl
from jax.experimental.pallas import tpu as pltpu

D_MODEL = 1024
N_GROUPS = 4
GROUP_DIM = 128
BRANCH_WIDTH = N_GROUPS * GROUP_DIM
POOL_WINDOWS = (2, 4, 8, 16)
D_FF = 2816
N_ADA = 6
EPS = 1e-6

F32_ROWS = 8
BF16_ROWS = 16
MXU_DIM = 256
VMEM_LIMIT_BYTES = 60 * 1024 * 1024

SEQ_MINOR = 128
HALO = BF16_ROWS
SUB_TILE = 256
MIX_IN_TILE = 1024
MIX_OUT_TILE = 512
MIX_OUT_SUB = 256
SUBS_PER_GROUP = 2
K1_PER_TILE = BF16_ROWS
K2_PER_TILE = MIX_OUT_TILE // K1_PER_TILE
ROW_BLOCKS_PER_STEP = 16
ADA_COLS = 1024
FF_CHUNKS = ((0, 1024), (1024, 1024), (2048, 768))

BF16 = jnp.bfloat16
F32 = jnp.float32


def _dot(a, b):
    return jnp.dot(a, b, preferred_element_type=F32)


def _sigmoid(v):
    return 1.0 / (1.0 + jnp.exp(-v))


def _rms_scale(v):
    return lax.rsqrt(jnp.mean(v * v, axis=-1, keepdims=True) + EPS)


def _ada_kernel(c_ref, w_ref, b_ref, o_ref):
    c = c_ref[...]
    s = (c * _sigmoid(c)).astype(BF16)
    o_ref[...] = _dot(s, w_ref[...].astype(BF16)) + b_ref[...]


def _ada(c_pad, w_ada, b_ada):
    rows, d = c_pad.shape
    n_out = w_ada.shape[1]
    return pl.pallas_call(
        _ada_kernel,
        grid=(n_out // ADA_COLS,),
        in_specs=[
            pl.BlockSpec((rows, d), lambda j: (0, 0)),
            pl.BlockSpec((d, ADA_COLS), lambda j: (0, j)),
            pl.BlockSpec((1, ADA_COLS), lambda j: (0, j)),
        ],
        out_specs=pl.BlockSpec((rows, ADA_COLS), lambda j: (0, j)),
        out_shape=jax.ShapeDtypeStruct((rows, n_out), F32),
        compiler_params=pltpu.CompilerParams(
            dimension_semantics=("arbitrary",), vmem_limit_bytes=VMEM_LIMIT_BYTES),
        name="ada",
    )(c_pad, w_ada, b_ada)


def _fold_pool_out_kernel(wpg_ref, ps_ref, wpo_ref, o_ref):
    for g in range(N_GROUPS):
        rows = pl.ds(g * GROUP_DIM, GROUP_DIM)
        scaled = wpg_ref[g] * ps_ref[:, rows]
        o_ref[rows, :] = jnp.dot(scaled, wpo_ref[rows, :], preferred_element_type=F32,
                                 precision=lax.Precision.HIGHEST).astype(o_ref.dtype)


def _fold_pool_out(w_pg, pool_scale, w_po):
    return pl.pallas_call(
        _fold_pool_out_kernel,
        out_shape=jax.ShapeDtypeStruct(w_po.shape, BF16),
        compiler_params=pltpu.CompilerParams(vmem_limit_bytes=VMEM_LIMIT_BYTES),
        name="fold_pool_out",
    )(w_pg, pool_scale, w_po)


def _window_sum(ext, w):
    n = ext.shape[0]
    s = ext
    k = 1
    while k < w // 2:
        s = s + pltpu.roll(s, n - k, axis=0)
        k *= 2
    return s + pltpu.roll(s, w // 2, axis=0)


def _mix_in_kernel(tiles_per_seq, seq_len, n_cast, has_dft_job,
                   x_ref, xp_ref, xn_ref, mod_ref, g_ref, win_ref, wpool_ref, *rest):
    wf_ref = win_ref.at[:, pl.ds(0, BRANCH_WIDTH)]
    wp_ref = win_ref.at[:, pl.ds(BRANCH_WIDTH, BRANCH_WIDTH)]
    wg_ref = win_ref.at[:, pl.ds(2 * BRANCH_WIDTH, 2 * D_MODEL)]
    n_extra_in = n_cast + 2 * has_dft_job
    cast_in, dft_in = rest[:n_cast], rest[n_cast:n_extra_in]
    uf_ref, gf_ref, p_ref = rest[n_extra_in:n_extra_in + 3]
    cast_out = rest[n_extra_in + 3:n_extra_in + 3 + n_cast]
    dft = _Stage1Body(*dft_in, rest[-1]) if has_dft_job else None
    if dft:
        dft.load()
    for src, dst in zip(cast_in, cast_out):
        dst[...] = src[...].astype(dst.dtype)
    t = MIX_IN_TILE
    n_sub = t // SUB_TILE
    ti = lax.rem(pl.program_id(0), tiles_per_seq)
    mod = mod_ref[0]
    scale = g_ref[...] * (1.0 + mod[1:2, :])
    shift = mod[0:1, :]

    def normed(v):
        return v * _rms_scale(v) * scale + shift

    ts = SUB_TILE
    subs = [pl.ds(s * ts, ts) for s in range(n_sub)]
    hs = [normed(x_ref[rows, :]).astype(BF16) for rows in subs]
    h_prev = jnp.where(ti == 0, 0.0, normed(xp_ref[...])).astype(BF16)
    h_next = jnp.where(ti == tiles_per_seq - 1, 0.0, normed(xn_ref[...])).astype(BF16)
    afters = [h[:HALO] for h in hs[1:]] + [h_next]

    slab = F32_ROWS
    row = lax.broadcasted_iota(jnp.int32, (slab, GROUP_DIM), 0)

    def pool_diff(up, first_pos):
        diffs = []
        for g, w in enumerate(POOL_WINDOWS):
            assert w & (w - 1) == 0 and w // 2 <= slab
            left = w // 2
            right = w - 1 - left
            ext = up[:, g * GROUP_DIM:(g + 1) * GROUP_DIM]
            win = _window_sum(ext, w)[HALO:HALO + ts]
            u = ext[HALO:HALO + ts]

            def count(pos):
                return (jnp.minimum(pos + right, seq_len - 1)
                        - jnp.maximum(pos - left, 0) + 1).astype(F32)

            diffs.append(jnp.concatenate([
                win[:slab] / count(first_pos + row) - u[:slab],
                win[slab:ts - slab] * (1.0 / w) - u[slab:ts - slab],
                win[ts - slab:] / count(first_pos + (ts - slab) + row) - u[ts - slab:],
            ], axis=0))
        return diffs

    def project(s):
        head = [h_prev, hs[0][:HALO]] if s == 0 else []
        up = _dot(jnp.concatenate(head + [hs[s][HALO:], afters[s]], axis=0), wp_ref[...])
        uf_ref[subs[s], :] = _dot(hs[s], wf_ref[...]).astype(uf_ref.dtype)
        return up, _dot(hs[s], wg_ref[...])

    def pool_rows(s):
        if s == 0:
            return proj[0][0]
        return jnp.concatenate([proj[s - 1][0][-2 * HALO:], proj[s][0]], axis=0)

    def finish(s, up, gate_logits):
        d = jnp.concatenate(pool_diff(up, ti * t + s * ts), axis=1).astype(BF16)
        g = _sigmoid(gate_logits)
        gf_ref[subs[s], :] = g[:, :D_MODEL].astype(gf_ref.dtype)
        p_ref[subs[s], :] = (g[:, D_MODEL:] * _dot(d, wpool_ref[...])).astype(p_ref.dtype)

    proj = {}
    for step in range(n_sub + 1):
        if step < n_sub:
            proj[step] = project(step)
        if dft and step == 0:
            dft.matmul()
            dft.store()
        if step >= 1:
            finish(step - 1, pool_rows(step - 1), proj[step - 1][1])


def _const_spec(shape):
    zeros = (0,) * len(shape)
    return pl.BlockSpec(shape, lambda *_: zeros, pipeline_mode=pl.Buffered(1))


def _cast_specs(weights, n_steps):
    specs, shapes = [], []
    for w in weights:
        rows, cols = w.shape
        n_blocks = n_steps
        while rows % n_blocks or (rows // n_blocks) % BF16_ROWS:
            n_blocks //= 2
        assert n_blocks >= 1
        spec = pl.BlockSpec((rows // n_blocks, cols),
                            lambda i, last=n_blocks - 1: (jnp.minimum(i, last), 0))
        specs.append(spec)
        shapes.append(jax.ShapeDtypeStruct(w.shape, BF16))
    return specs, shapes


def _mix_in(x2d, mods, seq_len, g_pre, w_in, w_pool, cast_weights=(), dft_job=None):
    n_tok = x2d.shape[0]
    t = MIX_IN_TILE
    tiles_per_seq = seq_len // t
    halo_blocks_per_tile = t // HALO
    n_halo_blocks = n_tok // HALO
    row = lambda i: (i, 0)
    cast_specs, cast_shapes = _cast_specs(cast_weights, n_tok // t)
    job = dft_job(n_tok // t, lambda i: i) if dft_job else None
    job_in = job.in_specs if job else []
    job_out = [job.out_spec] if job else []
    job_shape = [job.out_shape] if job else []
    job_args = job.args if job else ()
    return pl.pallas_call(
        functools.partial(_mix_in_kernel, tiles_per_seq, seq_len, len(cast_weights),
                          job is not None),
        grid=(n_tok // t,),
        in_specs=[
            pl.BlockSpec((t, D_MODEL), row),
            pl.BlockSpec((HALO, D_MODEL),
                         lambda i: (jnp.maximum(i * halo_blocks_per_tile - 1, 0), 0)),
            pl.BlockSpec((HALO, D_MODEL),
                         lambda i: (jnp.minimum((i + 1) * halo_blocks_per_tile,
                                                n_halo_blocks - 1), 0)),
            pl.BlockSpec((1, N_ADA, D_MODEL), lambda i: (i // tiles_per_seq, 0, 0)),
            _const_spec((1, D_MODEL)),
            _const_spec((D_MODEL, 2 * BRANCH_WIDTH + 2 * D_MODEL)),
            _const_spec((BRANCH_WIDTH, D_MODEL)),
        ] + cast_specs + job_in,
        out_specs=[
            pl.BlockSpec((t, BRANCH_WIDTH), row),
            pl.BlockSpec((t, D_MODEL), row),
            pl.BlockSpec((t, D_MODEL), row),
        ] + cast_specs + job_out,
        out_shape=[
            jax.ShapeDtypeStruct((n_tok, BRANCH_WIDTH), BF16),
            jax.ShapeDtypeStruct((n_tok, D_MODEL), BF16),
            jax.ShapeDtypeStruct((n_tok, D_MODEL), BF16),
        ] + cast_shapes + job_shape,
        compiler_params=pltpu.CompilerParams(
            dimension_semantics=("arbitrary",), vmem_limit_bytes=VMEM_LIMIT_BYTES),
        name="mix_in",
    )(x2d, x2d, x2d, mods, g_pre, w_in, w_pool, *cast_weights, *job_args)


class _Stage1Body:
    def __init__(self, x_ref, w_ref, o_ref):
        self.x_ref, self.w_ref, self.o_ref = x_ref, w_ref, o_ref

    def load(self):
        n1, c2, width = self.x_ref.shape
        self.xt = jnp.swapaxes(self.x_ref[...], 0, 1).reshape(
            n1 * c2 // SEQ_MINOR, SEQ_MINOR, width)

    def matmul(self):
        w = self.w_ref[...]
        self.z = jnp.stack([_dot(w, self.xt[g]).astype(BF16)
                            for g in range(self.xt.shape[0])])

    def store(self):
        n1, c2, width = self.x_ref.shape
        self.o_ref[...] = jnp.swapaxes(self.z.reshape(c2, n1, width), 0, 1)


def _dft_stage1_kernel(x_ref, w_ref, o_ref):
    body = _Stage1Body(x_ref, w_ref, o_ref)
    body.load()
    body.matmul()
    body.store()


def _dft_stage1(job_maker, n_steps):
    job = job_maker(n_steps, lambda i: i)
    return pl.pallas_call(
        _dft_stage1_kernel,
        grid=(n_steps,),
        in_specs=job.in_specs,
        out_specs=job.out_spec,
        out_shape=job.out_shape,
        compiler_params=pltpu.CompilerParams(
            dimension_semantics=("arbitrary",), vmem_limit_bytes=VMEM_LIMIT_BYTES),
        name="dft_stage1",
    )(*job.args)


class _Stage1Job:
    def __init__(self, u4, w_bd, n_steps, lin):
        batch, n1, n2, width = u4.shape
        assert n_steps % batch == 0
        chunks = n_steps // batch
        c2 = n2 // chunks
        assert n2 % chunks == 0 and c2 % BF16_ROWS == 0 and (c2 * n1) % SEQ_MINOR == 0
        index = lambda *g: (lin(*g) // chunks, 0, lin(*g) % chunks, 0)
        self.args = (u4, w_bd)
        self.in_specs = [pl.BlockSpec((None, n1, c2, width), index),
                         _const_spec((SEQ_MINOR, SEQ_MINOR))]
        self.out_spec = pl.BlockSpec((None, n1, c2, width), index)
        self.out_shape = jax.ShapeDtypeStruct((batch, n1, n2, width), BF16)


def _packed_windows(n1, step):
    rb, half = ROW_BLOCKS_PER_STEP, n1 // 2
    upper = rb * step >= max(half, rb)
    re_start = jnp.where(upper, n1 - rb * step - (rb - 1), rb * step)
    im_start = jnp.minimum(jnp.where(upper, re_start + half, half + rb * step), n1 - rb)
    return re_start, im_start


def _dft_stage2_kernel(n1, zr_ref, zi_ref, c2_ref, s2_ref, tc_ref, ts_ref, cs_ref, o_ref):
    c2 = c2_ref[...]
    s2 = s2_ref[...]
    half = n1 // 2
    step = pl.program_id(1)
    re_start, im_start = _packed_windows(n1, step)
    qs = []
    for j in range(ROW_BLOCKS_PER_STEP):
        k1 = step * ROW_BLOCKS_PER_STEP + j
        mirrored = k1 > half
        kp = jnp.where(mirrored, n1 - k1, k1)
        real_only = (kp == 0) | (kp == half)
        sign = jnp.where(real_only, 0.0, jnp.where(mirrored, -1.0, 1.0))
        tc = tc_ref[pl.ds(k1, 1), :]
        ts = ts_ref[pl.ds(k1, 1), :]
        cos_t = c2 * tc - s2 * ts
        sin_t = s2 * tc + c2 * ts
        lhs = jnp.concatenate(
            [jnp.concatenate([cos_t, sign * sin_t], axis=1),
             jnp.concatenate([-sin_t, sign * cos_t], axis=1)], axis=0).astype(BF16)
        z = jnp.concatenate(
            [zr_ref[0, kp - re_start],
             zi_ref[0, jnp.where(real_only, 0, half + kp - im_start)]], axis=0)
        qs.append(_dot(lhs, z).astype(BF16))
    halves = []
    for pair in range(N_GROUPS // 2):
        rows = []
        for q in qs:
            parts = []
            for g in (2 * pair, 2 * pair + 1):
                lanes = slice(g * GROUP_DIM, (g + 1) * GROUP_DIM)
                parts += [q[:SEQ_MINOR, lanes], q[SEQ_MINOR:, lanes]]
            rows.append(jnp.concatenate(parts, axis=1))
        halves.append(_dot(jnp.concatenate(rows, axis=0), cs_ref[...]))
    y = jnp.concatenate(halves, axis=1).astype(o_ref.dtype)
    o_ref[...] = y.reshape(o_ref.shape)


def _dft_stage2(z4, c2, s2, tc, ts, cs_bd):
    batch, n1, n2, width = z4.shape
    rb = ROW_BLOCKS_PER_STEP
    assert n1 % rb == 0 and n2 == SEQ_MINOR
    window = tuple(pl.Element(n) for n in (1, rb, n2, width))
    return pl.pallas_call(
        functools.partial(_dft_stage2_kernel, n1),
        grid=(batch, n1 // rb),
        in_specs=[
            pl.BlockSpec(window, lambda b, i: (b, _packed_windows(n1, i)[0], 0, 0)),
            pl.BlockSpec(window, lambda b, i: (b, _packed_windows(n1, i)[1], 0, 0)),
            _const_spec((SEQ_MINOR, SEQ_MINOR)),
            _const_spec((SEQ_MINOR, SEQ_MINOR)),
            _const_spec((n1, SEQ_MINOR)),
            _const_spec((n1, SEQ_MINOR)),
            _const_spec((2 * MXU_DIM, MXU_DIM)),
        ],
        out_specs=pl.BlockSpec((None, rb, n2, width), lambda b, i: (b, i, 0, 0)),
        out_shape=jax.ShapeDtypeStruct((batch, n1, n2, width), BF16),
        compiler_params=pltpu.CompilerParams(
            dimension_semantics=("arbitrary", "arbitrary"),
            vmem_limit_bytes=VMEM_LIMIT_BYTES),
        name="dft_stage2",
    )(z4, z4, c2, s2, tc, ts, cs_bd)


@functools.lru_cache(maxsize=None)
def _dft_tables(seq_len):
    n2 = SEQ_MINOR
    n1 = seq_len // n2
    reps = n2 // n1
    k = np.arange(n1, dtype=np.float64)
    ang1 = 2.0 * np.pi * np.outer(k, k) / n1
    half = n1 // 2
    w1 = np.concatenate([np.cos(ang1)[:half + 1], -np.sin(ang1)[1:half]], axis=0)
    w_bd = np.kron(np.eye(reps), w1)
    m = np.arange(n2, dtype=np.float64)
    ang2 = 2.0 * np.pi * np.outer(m, m) / n2
    angt = 2.0 * np.pi * np.outer(k, m) / seq_len
    as32 = lambda a: np.asarray(a, dtype=np.float32)
    return (as32(w_bd), as32(np.cos(ang2)), as32(np.sin(ang2)),
            as32(np.cos(angt)), as32(np.sin(angt)))


@functools.lru_cache(maxsize=None)
def _channel_dft_table(seq_len):
    c = np.arange(GROUP_DIM, dtype=np.float64)
    ang = 2.0 * np.pi * np.outer(c, c) / GROUP_DIM
    norm = 1.0 / np.sqrt(float(seq_len) * GROUP_DIM)
    cs = np.concatenate([np.cos(ang), np.sin(ang)], axis=0) * norm
    return np.asarray(np.kron(np.eye(2), cs), dtype=np.float32)


def _stage1_job(u_f, batch, seq_len):
    n1 = seq_len // SEQ_MINOR
    w_bd = jnp.asarray(_dft_tables(seq_len)[0]).astype(BF16)
    u4 = u_f.reshape(batch, n1, SEQ_MINOR, BRANCH_WIDTH)
    return lambda n_steps, lin: _Stage1Job(u4, w_bd, n_steps, lin)


def _stage2(z, seq_len):
    _, c2, s2, tc, ts = _dft_tables(seq_len)
    cs_bd = jnp.asarray(_channel_dft_table(seq_len)).astype(BF16)
    return _dft_stage2(z, jnp.asarray(c2), jnp.asarray(s2), jnp.asarray(tc), jnp.asarray(ts),
                       cs_bd)


def _mix_out_kernel(x_ref, y_ref, gf_ref, p_ref, mod_ref, gpm_ref, gpf_ref, gqf_ref,
                    wfo_ref, wout_ref, wgate_ref, wup_ref, wdown_ref, o_ref):
    mod = mod_ref[0]
    n_sub = MIX_OUT_TILE // MIX_OUT_SUB
    gate1 = mod[2:3, :] * gpm_ref[...]
    scale2 = gpf_ref[...] * (1.0 + mod[4:5, :])
    shift2 = mod[3:4, :]
    gate2 = mod[5:6, :] * gqf_ref[...]
    k2_sub = MIX_OUT_SUB // K1_PER_TILE
    subs = [pl.ds(s * k2_sub, k2_sub) for s in range(n_sub)]

    def rows_of(ref, k2s):
        return ref[k2s].reshape(MIX_OUT_SUB, ref.shape[-1])

    def up_proj(h2, chunk):
        cols = pl.ds(*chunk)
        return _dot(h2, wgate_ref[:, cols]), _dot(h2, wup_ref[:, cols])

    def down_proj(ab, chunk):
        a, b = ab
        return _dot((a * _sigmoid(a) * b).astype(BF16), wdown_ref[pl.ds(*chunk), :])

    def front(group):
        ys = [jnp.swapaxes(y_ref[:, k2s, :], 0, 1).reshape(MIX_OUT_SUB, BRANCH_WIDTH)
              for k2s in group]
        y_fs = [_dot(y, wfo_ref[...]) for y in ys]
        ms = [_dot((rows_of(gf_ref, k2s).astype(F32) * y_f
                    + rows_of(p_ref, k2s).astype(F32)).astype(BF16), wout_ref[...])
              for k2s, y_f in zip(group, y_fs)]
        x1s = [rows_of(x_ref, k2s) + gate1 * (m * _rms_scale(m))
               for k2s, m in zip(group, ms)]
        h2s = [(x1 * _rms_scale(x1) * scale2 + shift2).astype(BF16) for x1 in x1s]
        return x1s, h2s

    def ffn(h2s, before_last_down):
        zs = [None] * len(h2s)
        pending = [up_proj(h2, FF_CHUNKS[0]) for h2 in h2s]
        result = None
        for c, chunk in enumerate(FF_CHUNKS):
            last = c + 1 == len(FF_CHUNKS)
            nxt = None if last else [up_proj(h2, FF_CHUNKS[c + 1]) for h2 in h2s]
            if last:
                result = before_last_down()
            for s in range(len(h2s)):
                part = down_proj(pending[s], chunk)
                zs[s] = part if zs[s] is None else zs[s] + part
            pending = nxt
        return zs, result

    groups = [subs[i:i + SUBS_PER_GROUP] for i in range(0, n_sub, SUBS_PER_GROUP)]
    x1s, h2s = front(groups[0])
    for gi, group in enumerate(groups):
        following = groups[gi + 1] if gi + 1 < len(groups) else None
        zs, nxt = ffn(h2s, (lambda f=following: front(f)) if following else (lambda: None))
        for k2s, x1, z in zip(group, x1s, zs):
            out = x1 + gate2 * (z * _rms_scale(z))
            o_ref[k2s] = out.reshape(k2_sub, K1_PER_TILE, D_MODEL)
        if nxt is not None:
            x1s, h2s = nxt


def _mix_out(x, y4, g_f, p, mods, g_post_mix, g_pre_ffn, g_post_ffn,
             w_fo, w_out, w_gate, w_up, w_down):
    batch, seq_len, d = x.shape
    n1 = y4.shape[1]
    n2 = SEQ_MINOR
    k1t, k2t = K1_PER_TILE, K2_PER_TILE
    assert n1 % k1t == 0 and n2 % k2t == 0 and k1t * k2t == MIX_OUT_TILE
    by_k = lambda a: a.reshape(batch, n2, n1 // k1t, k1t, a.shape[-1])
    tok_spec = pl.BlockSpec((None, k2t, None, k1t, d), lambda b, a, t: (b, a, t, 0, 0))
    return pl.pallas_call(
        _mix_out_kernel,
        grid=(batch, n2 // k2t, n1 // k1t),
        in_specs=[
            tok_spec,
            pl.BlockSpec((None, k1t, k2t, BRANCH_WIDTH), lambda b, a, t: (b, t, a, 0)),
            tok_spec,
            tok_spec,
            pl.BlockSpec((1, N_ADA, D_MODEL), lambda b, a, t: (b, 0, 0)),
            _const_spec((1, D_MODEL)),
            _const_spec((1, D_MODEL)),
            _const_spec((1, D_MODEL)),
            _const_spec((BRANCH_WIDTH, D_MODEL)),
            _const_spec((D_MODEL, D_MODEL)),
            _const_spec((D_MODEL, D_FF)),
            _const_spec((D_MODEL, D_FF)),
            _const_spec((D_FF, D_MODEL)),
        ],
        out_specs=tok_spec,
        out_shape=jax.ShapeDtypeStruct((batch, n2, n1 // k1t, k1t, d), F32),
        compiler_params=pltpu.CompilerParams(
            dimension_semantics=("arbitrary", "arbitrary", "arbitrary"),
            vmem_limit_bytes=VMEM_LIMIT_BYTES),
        name="mix_out",
    )(by_k(x), y4, by_k(g_f), by_k(p), mods, g_post_mix, g_pre_ffn, g_post_ffn,
      w_fo, w_out, w_gate, w_up, w_down).reshape(batch, seq_len, d)


def _layer_pair(x_a, mods_a, x_b, mods_b, early, gains, late):
    (ba, sa, d), (bb, sb, _) = x_a.shape, x_b.shape
    u_a, gf_a, p_a, *late = _mix_in(x_a.reshape(ba * sa, d), mods_a, sa, *early,
                                    cast_weights=late)
    u_b, gf_b, p_b, z_a = _mix_in(x_b.reshape(bb * sb, d), mods_b, sb, *early,
                                  dft_job=_stage1_job(u_a, ba, sa))
    out_a = _mix_out(x_a, _stage2(z_a, sa), gf_a, p_a, mods_a, *gains, *late)
    z_b = _dft_stage1(_stage1_job(u_b, bb, sb), bb)
    out_b = _mix_out(x_b, _stage2(z_b, sb), gf_b, p_b, mods_b, *gains, *late)
    return out_a, out_b


def kernel(x_prompt, x_sample, c_prompt, c_sample, w_ada, b_ada, g_pre_mix, w_in, w_fo, w_pg,
           pool_scale, w_po, w_out, g_post_mix, g_pre_ffn, w_gate, w_up, w_down, g_post_ffn):
    depth = w_ada.shape[0]
    y_prompt, y_sample = x_prompt, x_sample
    bp, bs = c_prompt.shape[0], c_sample.shape[0]
    pad_rows = -(bp + bs) % BF16_ROWS
    c_pad = jnp.concatenate(
        [c_prompt, c_sample, jnp.zeros((pad_rows, D_MODEL), F32)], axis=0)
    for l in range(depth):
        ada = _ada(c_pad, w_ada[l], b_ada[l][None, :])
        mods_p = ada[:bp].reshape(bp, N_ADA, D_MODEL)
        mods_s = ada[bp:bp + bs].reshape(bs, N_ADA, D_MODEL)
        early = (g_pre_mix[l][None, :], w_in[l].astype(BF16),
                 _fold_pool_out(w_pg[l], pool_scale[l][None, :], w_po[l]))
        gains = (g_post_mix[l][None, :], g_pre_ffn[l][None, :], g_post_ffn[l][None, :])
        late = (w_fo[l], w_out[l], w_gate[l], w_up[l], w_down[l])
        y_prompt, y_sample = _layer_pair(y_prompt, mods_p, y_sample, mods_s, early, gains, late)
    return (y_prompt, y_sample)
```

```python
import functools

import numpy as np
import jax
import jax.numpy as jnp
from jax import lax
from jax.experimental import pallas as pl
from jax.experimental.pallas import tpu as pltpu

D_MODEL = 1024
N_GROUPS = 4
GROUP_DIM = 128
BRANCH_WIDTH = N_GROUPS * GROUP_DIM
POOL_WINDOWS = (2, 4, 8, 16)
D_FF = 2816
N_ADA = 6
EPS = 1e-6

F32_ROWS = 8
BF16_ROWS = 16
MXU_DIM = 256
VMEM_LIMIT_BYTES = 60 * 1024 * 1024

SEQ_MINOR = 128
HALO = BF16_ROWS
SUB_TILE = 256
MIX_IN_TILE = 1024
MIX_OUT_TILE = 1024
MIX_OUT_SUB = 256
SUBS_PER_GROUP = 2
K1_PER_TILE = BF16_ROWS
K2_PER_TILE = MIX_OUT_TILE // K1_PER_TILE
ROW_BLOCKS_PER_STEP = 16
ADA_COLS = 1024
FF_CHUNKS = ((0, 1024), (1024, 1024), (2048, 768))

BF16 = jnp.bfloat16
F32 = jnp.float32


def _dot(a, b):
    return jnp.dot(a, b, preferred_element_type=F32)


def _sigmoid(v):
    return 1.0 / (1.0 + jnp.exp(-v))


def _rms_scale(v):
    return lax.rsqrt(jnp.mean(v * v, axis=-1, keepdims=True) + EPS)


def _ada_kernel(c_ref, w_ref, b_ref, o_ref):
    c = c_ref[...]
    s = (c * _sigmoid(c)).astype(BF16)
    o_ref[...] = _dot(s, w_ref[...].astype(BF16)) + b_ref[...]


def _ada(c_pad, w_ada, b_ada):
    rows, d = c_pad.shape
    n_out = w_ada.shape[1]
    return pl.pallas_call(
        _ada_kernel,
        grid=(n_out // ADA_COLS,),
        in_specs=[
            pl.BlockSpec((rows, d), lambda j: (0, 0)),
            pl.BlockSpec((d, ADA_COLS), lambda j: (0, j)),
            pl.BlockSpec((1, ADA_COLS), lambda j: (0, j)),
        ],
        out_specs=pl.BlockSpec((rows, ADA_COLS), lambda j: (0, j)),
        out_shape=jax.ShapeDtypeStruct((rows, n_out), F32),
        compiler_params=pltpu.CompilerParams(
            dimension_semantics=("arbitrary",), vmem_limit_bytes=VMEM_LIMIT_BYTES),
        name="ada",
    )(c_pad, w_ada, b_ada)


def _fold_pool_out_kernel(wpg_ref, ps_ref, wpo_ref, o_ref):
    for g in range(N_GROUPS):
        rows = pl.ds(g * GROUP_DIM, GROUP_DIM)
        scaled = wpg_ref[g] * ps_ref[:, rows]
        o_ref[rows, :] = jnp.dot(scaled, wpo_ref[rows, :], preferred_element_type=F32,
                                 precision=lax.Precision.HIGHEST).astype(o_ref.dtype)


def _fold_pool_out(w_pg, pool_scale, w_po):
    return pl.pallas_call(
        _fold_pool_out_kernel,
        out_shape=jax.ShapeDtypeStruct(w_po.shape, BF16),
        compiler_params=pltpu.CompilerParams(vmem_limit_bytes=VMEM_LIMIT_BYTES),
        name="fold_pool_out",
    )(w_pg, pool_scale, w_po)


def _window_sum(ext, w):
    n = ext.shape[0]
    s = ext
    k = 1
    while k < w // 2:
        s = s + pltpu.roll(s, n - k, axis=0)
        k *= 2
    return s + pltpu.roll(s, w // 2, axis=0)


def _mix_in_kernel(tiles_per_seq, seq_len, n_cast, has_dft_job,
                   x_ref, xp_ref, xn_ref, mod_ref, g_ref, win_ref, wpool_ref, *rest):
    wf_ref = win_ref.at[:, pl.ds(0, BRANCH_WIDTH)]
    wp_ref = win_ref.at[:, pl.ds(BRANCH_WIDTH, BRANCH_WIDTH)]
    wg_ref = win_ref.at[:, pl.ds(2 * BRANCH_WIDTH, 2 * D_MODEL)]
    n_extra_in = n_cast + 2 * has_dft_job
    cast_in, dft_in = rest[:n_cast], rest[n_cast:n_extra_in]
    uf_ref, gf_ref, p_ref = rest[n_extra_in:n_extra_in + 3]
    cast_out = rest[n_extra_in + 3:n_extra_in + 3 + n_cast]
    dft = _Stage1Body(*dft_in, rest[-1]) if has_dft_job else None
    if dft:
        dft.load()
    for src, dst in zip(cast_in, cast_out):
        dst[...] = src[...].astype(dst.dtype)
    t = MIX_IN_TILE
    n_sub = t // SUB_TILE
    ti = lax.rem(pl.program_id(0), tiles_per_seq)
    mod = mod_ref[0]
    scale = g_ref[...] * (1.0 + mod[1:2, :])
    shift = mod[0:1, :]

    def normed(v):
        return v * _rms_scale(v) * scale + shift

    ts = SUB_TILE
    subs = [pl.ds(s * ts, ts) for s in range(n_sub)]
    hs = [normed(x_ref[rows, :]).astype(BF16) for rows in subs]
    h_prev = jnp.where(ti == 0, 0.0, normed(xp_ref[...])).astype(BF16)
    h_next = jnp.where(ti == tiles_per_seq - 1, 0.0, normed(xn_ref[...])).astype(BF16)
    afters = [h[:HALO] for h in hs[1:]] + [h_next]

    slab = F32_ROWS
    row = lax.broadcasted_iota(jnp.int32, (slab, GROUP_DIM), 0)

    def pool_diff(up, first_pos):
        diffs = []
        for g, w in enumerate(POOL_WINDOWS):
            assert w & (w - 1) == 0 and w // 2 <= slab
            left = w // 2
            right = w - 1 - left
            ext = up[:, g * GROUP_DIM:(g + 1) * GROUP_DIM]
            win = _window_sum(ext, w)[HALO:HALO + ts]
            u = ext[HALO:HALO + ts]

            def count(pos):
                return (jnp.minimum(pos + right, seq_len - 1)
                        - jnp.maximum(pos - left, 0) + 1).astype(F32)

            diffs.append(jnp.concatenate([
                win[:slab] / count(first_pos + row) - u[:slab],
                win[slab:ts - slab] * (1.0 / w) - u[slab:ts - slab],
                win[ts - slab:] / count(first_pos + (ts - slab) + row) - u[ts - slab:],
            ], axis=0))
        return diffs

    def project(s):
        head = [h_prev, hs[0][:HALO]] if s == 0 else []
        up = _dot(jnp.concatenate(head + [hs[s][HALO:], afters[s]], axis=0), wp_ref[...])
        uf_ref[subs[s], :] = _dot(hs[s], wf_ref[...]).astype(uf_ref.dtype)
        return up, _dot(hs[s], wg_ref[...])

    def pool_rows(s):
        if s == 0:
            return proj[0][0]
        return jnp.concatenate([proj[s - 1][0][-2 * HALO:], proj[s][0]], axis=0)

    def finish(s, up, gate_logits):
        d = jnp.concatenate(pool_diff(up, ti * t + s * ts), axis=1).astype(BF16)
        g = _sigmoid(gate_logits)
        gf_ref[subs[s], :] = g[:, :D_MODEL].astype(gf_ref.dtype)
        p_ref[subs[s], :] = (g[:, D_MODEL:] * _dot(d, wpool_ref[...])).astype(p_ref.dtype)

    proj = {}
    for step in range(n_sub + 1):
        if step < n_sub:
            proj[step] = project(step)
        if dft and step == 0:
            dft.matmul()
            dft.store()
        if step >= 1:
            finish(step - 1, pool_rows(step - 1), proj[step - 1][1])


def _const_spec(shape):
    zeros = (0,) * len(shape)
    return pl.BlockSpec(shape, lambda *_: zeros, pipeline_mode=pl.Buffered(1))


def _cast_specs(weights, n_steps):
    specs, shapes = [], []
    for w in weights:
        rows, cols = w.shape
        n_blocks = n_steps
        while rows % n_blocks or (rows // n_blocks) % BF16_ROWS:
            n_blocks //= 2
        assert n_blocks >= 1
        spec = pl.BlockSpec((rows // n_blocks, cols),
                            lambda i, last=n_blocks - 1: (jnp.minimum(i, last), 0))
        specs.append(spec)
        shapes.append(jax.ShapeDtypeStruct(w.shape, BF16))
    return specs, shapes


def _mix_in(x2d, mods, seq_len, g_pre, w_in, w_pool, cast_weights=(), dft_job=None):
    n_tok = x2d.shape[0]
    t = MIX_IN_TILE
    tiles_per_seq = seq_len // t
    halo_blocks_per_tile = t // HALO
    n_halo_blocks = n_tok // HALO
    row = lambda i: (i, 0)
    cast_specs, cast_shapes = _cast_specs(cast_weights, n_tok // t)
    job = dft_job(n_tok // t, lambda i: i) if dft_job else None
    job_in = job.in_specs if job else []
    job_out = [job.out_spec] if job else []
    job_shape = [job.out_shape] if job else []
    job_args = job.args if job else ()
    return pl.pallas_call(
        functools.partial(_mix_in_kernel, tiles_per_seq, seq_len, len(cast_weights),
                          job is not None),
        grid=(n_tok // t,),
        in_specs=[
            pl.BlockSpec((t, D_MODEL), row),
            pl.BlockSpec((HALO, D_MODEL),
                         lambda i: (jnp.maximum(i * halo_blocks_per_tile - 1, 0), 0)),
            pl.BlockSpec((HALO, D_MODEL),
                         lambda i: (jnp.minimum((i + 1) * halo_blocks_per_tile,
                                                n_halo_blocks - 1), 0)),
            pl.BlockSpec((1, N_ADA, D_MODEL), lambda i: (i // tiles_per_seq, 0, 0)),
            _const_spec((1, D_MODEL)),
            _const_spec((D_MODEL, 2 * BRANCH_WIDTH + 2 * D_MODEL)),
            _const_spec((BRANCH_WIDTH, D_MODEL)),
        ] + cast_specs + job_in,
        out_specs=[
            pl.BlockSpec((t, BRANCH_WIDTH), row),
            pl.BlockSpec((t, D_MODEL), row),
            pl.BlockSpec((t, D_MODEL), row),
        ] + cast_specs + job_out,
        out_shape=[
            jax.ShapeDtypeStruct((n_tok, BRANCH_WIDTH), BF16),
            jax.ShapeDtypeStruct((n_tok, D_MODEL), BF16),
            jax.ShapeDtypeStruct((n_tok, D_MODEL), BF16),
        ] + cast_shapes + job_shape,
        compiler_params=pltpu.CompilerParams(
            dimension_semantics=("arbitrary",), vmem_limit_bytes=VMEM_LIMIT_BYTES),
        name="mix_in",
    )(x2d, x2d, x2d, mods, g_pre, w_in, w_pool, *cast_weights, *job_args)


class _Stage1Body:
    def __init__(self, x_ref, w_ref, o_ref):
        self.x_ref, self.w_ref, self.o_ref = x_ref, w_ref, o_ref

    def load(self):
        n1, c2, width = self.x_ref.shape
        self.xt = jnp.swapaxes(self.x_ref[...], 0, 1).reshape(
            n1 * c2 // SEQ_MINOR, SEQ_MINOR, width)

    def matmul(self):
        w = self.w_ref[...]
        self.z = jnp.stack([_dot(w, self.xt[g]).astype(BF16)
                            for g in range(self.xt.shape[0])])

    def store(self):
        n1, c2, width = self.x_ref.shape
        self.o_ref[...] = jnp.swapaxes(self.z.reshape(c2, n1, width), 0, 1)


def _dft_stage1_kernel(x_ref, w_ref, o_ref):
    body = _Stage1Body(x_ref, w_ref, o_ref)
    body.load()
    body.matmul()
    body.store()


def _dft_stage1(job_maker, n_steps):
    job = job_maker(n_steps, lambda i: i)
    return pl.pallas_call(
        _dft_stage1_kernel,
        grid=(n_steps,),
        in_specs=job.in_specs,
        out_specs=job.out_spec,
        out_shape=job.out_shape,
        compiler_params=pltpu.CompilerParams(
            dimension_semantics=("arbitrary",), vmem_limit_bytes=VMEM_LIMIT_BYTES),
        name="dft_stage1",
    )(*job.args)


class _Stage1Job:
    def __init__(self, u4, w_bd, n_steps, lin):
        batch, n1, n2, width = u4.shape
        assert n_steps % batch == 0
        chunks = n_steps // batch
        c2 = n2 // chunks
        assert n2 % chunks == 0 and c2 % BF16_ROWS == 0 and (c2 * n1) % SEQ_MINOR == 0
        index = lambda *g: (lin(*g) // chunks, 0, lin(*g) % chunks, 0)
        self.args = (u4, w_bd)
        self.in_specs = [pl.BlockSpec((None, n1, c2, width), index),
                         _const_spec((SEQ_MINOR, SEQ_MINOR))]
        self.out_spec = pl.BlockSpec((None, n1, c2, width), index)
        self.out_shape = jax.ShapeDtypeStruct((batch, n1, n2, width), BF16)


def _packed_windows(n1, step):
    rb, half = ROW_BLOCKS_PER_STEP, n1 // 2
    upper = rb * step >= max(half, rb)
    re_start = jnp.where(upper, n1 - rb * step - (rb - 1), rb * step)
    im_start = jnp.minimum(jnp.where(upper, re_start + half, half + rb * step), n1 - rb)
    return re_start, im_start


def _dft_stage2_kernel(n1, zr_ref, zi_ref, c2_ref, s2_ref, tc_ref, ts_ref, cs_ref, o_ref):
    c2 = c2_ref[...]
    s2 = s2_ref[...]
    half = n1 // 2
    step = pl.program_id(1)
    re_start, im_start = _packed_windows(n1, step)
    qs = []
    for j in range(ROW_BLOCKS_PER_STEP):
        k1 = step * ROW_BLOCKS_PER_STEP + j
        mirrored = k1 > half
        kp = jnp.where(mirrored, n1 - k1, k1)
        real_only = (kp == 0) | (kp == half)
        sign = jnp.where(real_only, 0.0, jnp.where(mirrored, -1.0, 1.0))
        tc = tc_ref[pl.ds(k1, 1), :]
        ts = ts_ref[pl.ds(k1, 1), :]
        cos_t = c2 * tc - s2 * ts
        sin_t = s2 * tc + c2 * ts
        lhs = jnp.concatenate(
            [jnp.concatenate([cos_t, sign * sin_t], axis=1),
             jnp.concatenate([-sin_t, sign * cos_t], axis=1)], axis=0).astype(BF16)
        z = jnp.concatenate(
            [zr_ref[0, kp - re_start],
             zi_ref[0, jnp.where(real_only, 0, half + kp - im_start)]], axis=0)
        qs.append(_dot(lhs, z).astype(BF16))
    halves = []
    for pair in range(N_GROUPS // 2):
        rows = []
        for q in qs:
            parts = []
            for g in (2 * pair, 2 * pair + 1):
                lanes = slice(g * GROUP_DIM, (g + 1) * GROUP_DIM)
                parts += [q[:SEQ_MINOR, lanes], q[SEQ_MINOR:, lanes]]
            rows.append(jnp.concatenate(parts, axis=1))
        halves.append(_dot(jnp.concatenate(rows, axis=0), cs_ref[...]))
    y = jnp.concatenate(halves, axis=1).astype(o_ref.dtype)
    o_ref[...] = y.reshape(o_ref.shape)


def _dft_stage2(z4, c2, s2, tc, ts, cs_bd):
    batch, n1, n2, width = z4.shape
    rb = ROW_BLOCKS_PER_STEP
    assert n1 % rb == 0 and n2 == SEQ_MINOR
    window = tuple(pl.Element(n) for n in (1, rb, n2, width))
    return pl.pallas_call(
        functools.partial(_dft_stage2_kernel, n1),
        grid=(batch, n1 // rb),
        in_specs=[
            pl.BlockSpec(window, lambda b, i: (b, _packed_windows(n1, i)[0], 0, 0)),
            pl.BlockSpec(window, lambda b, i: (b, _packed_windows(n1, i)[1], 0, 0)),
            _const_spec((SEQ_MINOR, SEQ_MINOR)),
            _const_spec((SEQ_MINOR, SEQ_MINOR)),
            _const_spec((n1, SEQ_MINOR)),
            _const_spec((n1, SEQ_MINOR)),
            _const_spec((2 * MXU_DIM, MXU_DIM)),
        ],
        out_specs=pl.BlockSpec((None, rb, n2, width), lambda b, i: (b, i, 0, 0)),
        out_shape=jax.ShapeDtypeStruct((batch, n1, n2, width), BF16),
        compiler_params=pltpu.CompilerParams(
            dimension_semantics=("arbitrary", "arbitrary"),
            vmem_limit_bytes=VMEM_LIMIT_BYTES),
        name="dft_stage2",
    )(z4, z4, c2, s2, tc, ts, cs_bd)


@functools.lru_cache(maxsize=None)
def _dft_tables(seq_len):
    n2 = SEQ_MINOR
    n1 = seq_len // n2
    reps = n2 // n1
    k = np.arange(n1, dtype=np.float64)
    ang1 = 2.0 * np.pi * np.outer(k, k) / n1
    half = n1 // 2
    w1 = np.concatenate([np.cos(ang1)[:half + 1], -np.sin(ang1)[1:half]], axis=0)
    w_bd = np.kron(np.eye(reps), w1)
    m = np.arange(n2, dtype=np.float64)
    ang2 = 2.0 * np.pi * np.outer(m, m) / n2
    angt = 2.0 * np.pi * np.outer(k, m) / seq_len
    as32 = lambda a: np.asarray(a, dtype=np.float32)
    return (as32(w_bd), as32(np.cos(ang2)), as32(np.sin(ang2)),
            as32(np.cos(angt)), as32(np.sin(angt)))


@functools.lru_cache(maxsize=None)
def _channel_dft_table(seq_len):
    c = np.arange(GROUP_DIM, dtype=np.float64)
    ang = 2.0 * np.pi * np.outer(c, c) / GROUP_DIM
    norm = 1.0 / np.sqrt(float(seq_len) * GROUP_DIM)
    cs = np.concatenate([np.cos(ang), np.sin(ang)], axis=0) * norm
    return np.asarray(np.kron(np.eye(2), cs), dtype=np.float32)


def _stage1_job(u_f, batch, seq_len):
    n1 = seq_len // SEQ_MINOR
    w_bd = jnp.asarray(_dft_tables(seq_len)[0]).astype(BF16)
    u4 = u_f.reshape(batch, n1, SEQ_MINOR, BRANCH_WIDTH)
    return lambda n_steps, lin: _Stage1Job(u4, w_bd, n_steps, lin)


def _stage2(z, seq_len):
    _, c2, s2, tc, ts = _dft_tables(seq_len)
    cs_bd = jnp.asarray(_channel_dft_table(seq_len)).astype(BF16)
    return _dft_stage2(z, jnp.asarray(c2), jnp.asarray(s2), jnp.asarray(tc), jnp.asarray(ts),
                       cs_bd)


def _mix_out_kernel(x_ref, y_ref, gf_ref, p_ref, mod_ref, gpm_ref, gpf_ref, gqf_ref,
                    wfo_ref, wout_ref, wgate_ref, wup_ref, wdown_ref, o_ref):
    mod = mod_ref[0]
    n_sub = MIX_OUT_TILE // MIX_OUT_SUB
    gate1 = mod[2:3, :] * gpm_ref[...]
    scale2 = gpf_ref[...] * (1.0 + mod[4:5, :])
    shift2 = mod[3:4, :]
    gate2 = mod[5:6, :] * gqf_ref[...]
    k2_sub = MIX_OUT_SUB // K1_PER_TILE

    def rows_of(ref, k2s):
        return ref[k2s].reshape(MIX_OUT_SUB, ref.shape[-1])

    def up_proj(h2, chunk):
        cols = pl.ds(*chunk)
        return _dot(h2, wgate_ref[:, cols]), _dot(h2, wup_ref[:, cols])

    def down_proj(ab, chunk):
        a, b = ab
        return _dot((a * _sigmoid(a) * b).astype(BF16), wdown_ref[pl.ds(*chunk), :])

    def front(group):
        ys = [jnp.swapaxes(y_ref[:, k2s, :], 0, 1).reshape(MIX_OUT_SUB, BRANCH_WIDTH)
              for k2s in group]
        y_fs = [_dot(y, wfo_ref[...]) for y in ys]
        ms = [_dot((rows_of(gf_ref, k2s).astype(F32) * y_f
                    + rows_of(p_ref, k2s).astype(F32)).astype(BF16), wout_ref[...])
              for k2s, y_f in zip(group, y_fs)]
        x1s = [rows_of(x_ref, k2s) + gate1 * (m * _rms_scale(m))
               for k2s, m in zip(group, ms)]
        h2s = [(x1 * _rms_scale(x1) * scale2 + shift2).astype(BF16) for x1 in x1s]
        return x1s, h2s

    def ffn(h2s):
        zs = [None] * len(h2s)
        pending = [up_proj(h2, FF_CHUNKS[0]) for h2 in h2s]
        for c, chunk in enumerate(FF_CHUNKS):
            last = c + 1 == len(FF_CHUNKS)
            nxt = None if last else [up_proj(h2, FF_CHUNKS[c + 1]) for h2 in h2s]
            for s in range(len(h2s)):
                part = down_proj(pending[s], chunk)
                zs[s] = part if zs[s] is None else zs[s] + part
            pending = nxt
        return zs

    def run_group(g):
        base = pl.multiple_of(g * (SUBS_PER_GROUP * k2_sub), SUBS_PER_GROUP * k2_sub)
        group = [pl.ds(base + s * k2_sub, k2_sub) for s in range(SUBS_PER_GROUP)]
        x1s, h2s = front(group)
        for k2s, x1, z in zip(group, x1s, ffn(h2s)):
            out = x1 + gate2 * (z * _rms_scale(z))
            o_ref[k2s] = out.reshape(k2_sub, K1_PER_TILE, D_MODEL)

    n_groups = n_sub // SUBS_PER_GROUP
    if n_groups == 1:
        run_group(0)
    else:
        lax.fori_loop(0, n_groups, lambda g, carry: (run_group(g), carry)[1], 0)


def _mix_out(x, y4, g_f, p, mods, g_post_mix, g_pre_ffn, g_post_ffn,
             w_fo, w_out, w_gate, w_up, w_down):
    batch, seq_len, d = x.shape
    n1 = y4.shape[1]
    n2 = SEQ_MINOR
    k1t, k2t = K1_PER_TILE, K2_PER_TILE
    assert n1 % k1t == 0 and n2 % k2t == 0 and k1t * k2t == MIX_OUT_TILE
    by_k = lambda a: a.reshape(batch, n2, n1 // k1t, k1t, a.shape[-1])
    tok_spec = pl.BlockSpec((None, k2t, None, k1t, d), lambda b, a, t: (b, a, t, 0, 0))
    return pl.pallas_call(
        _mix_out_kernel,
        grid=(batch, n2 // k2t, n1 // k1t),
        in_specs=[
            tok_spec,
            pl.BlockSpec((None, k1t, k2t, BRANCH_WIDTH), lambda b, a, t: (b, t, a, 0)),
            tok_spec,
            tok_spec,
            pl.BlockSpec((1, N_ADA, D_MODEL), lambda b, a, t: (b, 0, 0)),
            _const_spec((1, D_MODEL)),
            _const_spec((1, D_MODEL)),
            _const_spec((1, D_MODEL)),
            _const_spec((BRANCH_WIDTH, D_MODEL)),
            _const_spec((D_MODEL, D_MODEL)),
            _const_spec((D_MODEL, D_FF)),
            _const_spec((D_MODEL, D_FF)),
            _const_spec((D_FF, D_MODEL)),
        ],
        out_specs=tok_spec,
        out_shape=jax.ShapeDtypeStruct((batch, n2, n1 // k1t, k1t, d), F32),
        compiler_params=pltpu.CompilerParams(
            dimension_semantics=("arbitrary", "arbitrary", "arbitrary"),
            vmem_limit_bytes=VMEM_LIMIT_BYTES),
        name="mix_out",
    )(by_k(x), y4, by_k(g_f), by_k(p), mods, g_post_mix, g_pre_ffn, g_post_ffn,
      w_fo, w_out, w_gate, w_up, w_down).reshape(batch, seq_len, d)


def _layer_pair(x_a, mods_a, x_b, mods_b, early, gains, late):
    (ba, sa, d), (bb, sb, _) = x_a.shape, x_b.shape
    u_a, gf_a, p_a, *late = _mix_in(x_a.reshape(ba * sa, d), mods_a, sa, *early,
                                    cast_weights=late)
    u_b, gf_b, p_b, z_a = _mix_in(x_b.reshape(bb * sb, d), mods_b, sb, *early,
                                  dft_job=_stage1_job(u_a, ba, sa))
    out_a = _mix_out(x_a, _stage2(z_a, sa), gf_a, p_a, mods_a, *gains, *late)
    z_b = _dft_stage1(_stage1_job(u_b, bb, sb), bb)
    out_b = _mix_out(x_b, _stage2(z_b, sb), gf_b, p_b, mods_b, *gains, *late)
    return out_a, out_b


def kernel(x_prompt, x_sample, c_prompt, c_sample, w_ada, b_ada, g_pre_mix, w_in, w_fo, w_pg,
           pool_scale, w_po, w_out, g_post_mix, g_pre_ffn, w_gate, w_up, w_down, g_post_ffn):
    depth = w_ada.shape[0]
    y_prompt, y_sample = x_prompt, x_sample
    bp, bs = c_prompt.shape[0], c_sample.shape[0]
    pad_rows = -(bp + bs) % BF16_ROWS
    c_pad = jnp.concatenate(
        [c_prompt, c_sample, jnp.zeros((pad_rows, D_MODEL), F32)], axis=0)
    for l in range(depth):
        ada = _ada(c_pad, w_ada[l], b_ada[l][None, :])
        mods_p = ada[:bp].reshape(bp, N_ADA, D_MODEL)
        mods_s = ada[bp:bp + bs].reshape(bs, N_ADA, D_MODEL)
        early = (g_pre_mix[l][None, :], w_in[l].astype(BF16),
                 _fold_pool_out(w_pg[l], pool_scale[l][None, :], w_po[l]))
        gains = (g_post_mix[l][None, :], g_pre_ffn[l][None, :], g_post_ffn[l][None, :])
        late = (w_fo[l], w_out[l], w_gate[l], w_up[l], w_down[l])
        y_prompt, y_sample = _layer_pair(y_prompt, mods_p, y_sample, mods_s, early, gains, late)
    return (y_prompt, y_sample)
```

```python
import functools

import numpy as np
import jax
import jax.numpy as jnp
from jax import lax
from jax.experimental import pallas as pl
from jax.experimental.pallas import tpu as pltpu

D_MODEL = 1024
N_GROUPS = 4
GROUP_DIM = 128
BRANCH_WIDTH = N_GROUPS * GROUP_DIM
POOL_WINDOWS = (2, 4, 8, 16)
D_FF = 2816
N_ADA = 6
EPS = 1e-6

F32_ROWS = 8
BF16_ROWS = 16
MXU_DIM = 256
VMEM_LIMIT_BYTES = 60 * 1024 * 1024

SEQ_MINOR = 128
HALO = BF16_ROWS
SUB_TILE = 256
MIX_IN_TILE = 1024
MIX_OUT_TILE = 1024
MIX_OUT_SUB = 256
SUBS_PER_GROUP = 2
K1_PER_TILE = BF16_ROWS
K2_PER_TILE = MIX_OUT_TILE // K1_PER_TILE
ROW_BLOCKS_PER_STEP = 16
ADA_COLS = 1024
FF_CHUNKS = ((0, 1024), (1024, 1024), (2048, 768))

BF16 = jnp.bfloat16
F32 = jnp.float32


def _dot(a, b):
    return jnp.dot(a, b, preferred_element_type=F32)


def _sigmoid(v):
    return 1.0 / (1.0 + jnp.exp(-v))


def _rms_scale(v):
    return lax.rsqrt(jnp.mean(v * v, axis=-1, keepdims=True) + EPS)


def _ada_kernel(c_ref, w_ref, b_ref, o_ref):
    c = c_ref[...]
    s = (c * _sigmoid(c)).astype(BF16)
    o_ref[...] = _dot(s, w_ref[...].astype(BF16)) + b_ref[...]


def _ada(c_pad, w_ada, b_ada):
    rows, d = c_pad.shape
    n_out = w_ada.shape[1]
    return pl.pallas_call(
        _ada_kernel,
        grid=(n_out // ADA_COLS,),
        in_specs=[
            pl.BlockSpec((rows, d), lambda j: (0, 0)),
            pl.BlockSpec((d, ADA_COLS), lambda j: (0, j)),
            pl.BlockSpec((1, ADA_COLS), lambda j: (0, j)),
        ],
        out_specs=pl.BlockSpec((rows, ADA_COLS), lambda j: (0, j)),
        out_shape=jax.ShapeDtypeStruct((rows, n_out), F32),
        compiler_params=pltpu.CompilerParams(
            dimension_semantics=("arbitrary",), vmem_limit_bytes=VMEM_LIMIT_BYTES),
        name="ada",
    )(c_pad, w_ada, b_ada)


def _fold_pool_out_kernel(wpg_ref, ps_ref, wpo_ref, o_ref):
    for g in range(N_GROUPS):
        rows = pl.ds(g * GROUP_DIM, GROUP_DIM)
        scaled = wpg_ref[g] * ps_ref[:, rows]
        o_ref[rows, :] = jnp.dot(scaled, wpo_ref[rows, :], preferred_element_type=F32,
                                 precision=lax.Precision.HIGHEST).astype(o_ref.dtype)


def _fold_pool_out(w_pg, pool_scale, w_po):
    return pl.pallas_call(
        _fold_pool_out_kernel,
        out_shape=jax.ShapeDtypeStruct(w_po.shape, BF16),
        compiler_params=pltpu.CompilerParams(vmem_limit_bytes=VMEM_LIMIT_BYTES),
        name="fold_pool_out",
    )(w_pg, pool_scale, w_po)


def _window_sum(ext, w):
    n = ext.shape[0]
    s = ext
    k = 1
    while k < w // 2:
        s = s + pltpu.roll(s, n - k, axis=0)
        k *= 2
    return s + pltpu.roll(s, w // 2, axis=0)


def _mix_in_kernel(tiles_per_seq, seq_len, n_cast, has_dft_job,
                   x_ref, xp_ref, xn_ref, mod_ref, g_ref, win_ref, wpool_ref, *rest):
    wf_ref = win_ref.at[:, pl.ds(0, BRANCH_WIDTH)]
    wp_ref = win_ref.at[:, pl.ds(BRANCH_WIDTH, BRANCH_WIDTH)]
    wg_ref = win_ref.at[:, pl.ds(2 * BRANCH_WIDTH, 2 * D_MODEL)]
    n_extra_in = n_cast + 2 * has_dft_job
    cast_in, dft_in = rest[:n_cast], rest[n_cast:n_extra_in]
    uf_ref, gf_ref, p_ref = rest[n_extra_in:n_extra_in + 3]
    cast_out = rest[n_extra_in + 3:n_extra_in + 3 + n_cast]
    dft = _Stage1Body(*dft_in, rest[-1]) if has_dft_job else None
    if dft:
        dft.load()
    for src, dst in zip(cast_in, cast_out):
        dst[...] = src[...].astype(dst.dtype)
    t = MIX_IN_TILE
    n_sub = t // SUB_TILE
    ti = lax.rem(pl.program_id(0), tiles_per_seq)
    mod = mod_ref[0]
    scale = g_ref[...] * (1.0 + mod[1:2, :])
    shift = mod[0:1, :]

    def normed(v):
        return v * _rms_scale(v) * scale + shift

    ts = SUB_TILE
    subs = [pl.ds(s * ts, ts) for s in range(n_sub)]
    hs = [normed(x_ref[rows, :]).astype(BF16) for rows in subs]
    h_prev = jnp.where(ti == 0, 0.0, normed(xp_ref[...])).astype(BF16)
    h_next = jnp.where(ti == tiles_per_seq - 1, 0.0, normed(xn_ref[...])).astype(BF16)
    afters = [h[:HALO] for h in hs[1:]] + [h_next]

    slab = F32_ROWS
    row = lax.broadcasted_iota(jnp.int32, (slab, GROUP_DIM), 0)

    def pool_diff(up, first_pos):
        diffs = []
        for g, w in enumerate(POOL_WINDOWS):
            assert w & (w - 1) == 0 and w // 2 <= slab
            left = w // 2
            right = w - 1 - left
            ext = up[:, g * GROUP_DIM:(g + 1) * GROUP_DIM]
            win = _window_sum(ext, w)[HALO:HALO + ts]
            u = ext[HALO:HALO + ts]

            def count(pos):
                return (jnp.minimum(pos + right, seq_len - 1)
                        - jnp.maximum(pos - left, 0) + 1).astype(F32)

            diffs.append(jnp.concatenate([
                win[:slab] / count(first_pos + row) - u[:slab],
                win[slab:ts - slab] * (1.0 / w) - u[slab:ts - slab],
                win[ts - slab:] / count(first_pos + (ts - slab) + row) - u[ts - slab:],
            ], axis=0))
        return diffs

    def project(s):
        head = [h_prev, hs[0][:HALO]] if s == 0 else []
        up = _dot(jnp.concatenate(head + [hs[s][HALO:], afters[s]], axis=0), wp_ref[...])
        uf_ref[subs[s], :] = _dot(hs[s], wf_ref[...]).astype(uf_ref.dtype)
        return up, _dot(hs[s], wg_ref[...])

    def pool_rows(s):
        if s == 0:
            return proj[0][0]
        return jnp.concatenate([proj[s - 1][0][-2 * HALO:], proj[s][0]], axis=0)

    def finish(s, up, gate_logits):
        d = jnp.concatenate(pool_diff(up, ti * t + s * ts), axis=1).astype(BF16)
        g = _sigmoid(gate_logits)
        gf_ref[subs[s], :] = g[:, :D_MODEL].astype(gf_ref.dtype)
        p_ref[subs[s], :] = (g[:, D_MODEL:] * _dot(d, wpool_ref[...])).astype(p_ref.dtype)

    proj = {}
    for step in range(n_sub + 1):
        if step < n_sub:
            proj[step] = project(step)
        if dft and step == 0:
            dft.matmul()
            dft.store()
        if step >= 1:
            finish(step - 1, pool_rows(step - 1), proj[step - 1][1])


def _const_spec(shape):
    zeros = (0,) * len(shape)
    return pl.BlockSpec(shape, lambda *_: zeros, pipeline_mode=pl.Buffered(1))


def _cast_specs(weights, n_steps):
    specs, shapes = [], []
    for w in weights:
        rows, cols = w.shape
        n_blocks = n_steps
        while rows % n_blocks or (rows // n_blocks) % BF16_ROWS:
            n_blocks //= 2
        assert n_blocks >= 1
        spec = pl.BlockSpec((rows // n_blocks, cols),
                            lambda i, last=n_blocks - 1: (jnp.minimum(i, last), 0))
        specs.append(spec)
        shapes.append(jax.ShapeDtypeStruct(w.shape, BF16))
    return specs, shapes


def _mix_in(x2d, mods, seq_len, g_pre, w_in, w_pool, cast_weights=(), dft_job=None):
    n_tok = x2d.shape[0]
    t = MIX_IN_TILE
    tiles_per_seq = seq_len // t
    halo_blocks_per_tile = t // HALO
    n_halo_blocks = n_tok // HALO
    row = lambda i: (i, 0)
    cast_specs, cast_shapes = _cast_specs(cast_weights, n_tok // t)
    job = dft_job(n_tok // t, lambda i: i) if dft_job else None
    job_in = job.in_specs if job else []
    job_out = [job.out_spec] if job else []
    job_shape = [job.out_shape] if job else []
    job_args = job.args if job else ()
    return pl.pallas_call(
        functools.partial(_mix_in_kernel, tiles_per_seq, seq_len, len(cast_weights),
                          job is not None),
        grid=(n_tok // t,),
        in_specs=[
            pl.BlockSpec((t, D_MODEL), row),
            pl.BlockSpec((HALO, D_MODEL),
                         lambda i: (jnp.maximum(i * halo_blocks_per_tile - 1, 0), 0)),
            pl.BlockSpec((HALO, D_MODEL),
                         lambda i: (jnp.minimum((i + 1) * halo_blocks_per_tile,
                                                n_halo_blocks - 1), 0)),
            pl.BlockSpec((1, N_ADA, D_MODEL), lambda i: (i // tiles_per_seq, 0, 0)),
            _const_spec((1, D_MODEL)),
            _const_spec((D_MODEL, 2 * BRANCH_WIDTH + 2 * D_MODEL)),
            _const_spec((BRANCH_WIDTH, D_MODEL)),
        ] + cast_specs + job_in,
        out_specs=[
            pl.BlockSpec((t, BRANCH_WIDTH), row),
            pl.BlockSpec((t, D_MODEL), row),
            pl.BlockSpec((t, D_MODEL), row),
        ] + cast_specs + job_out,
        out_shape=[
            jax.ShapeDtypeStruct((n_tok, BRANCH_WIDTH), BF16),
            jax.ShapeDtypeStruct((n_tok, D_MODEL), BF16),
            jax.ShapeDtypeStruct((n_tok, D_MODEL), BF16),
        ] + cast_shapes + job_shape,
        compiler_params=pltpu.CompilerParams(
            dimension_semantics=("arbitrary",), vmem_limit_bytes=VMEM_LIMIT_BYTES),
        name="mix_in",
    )(x2d, x2d, x2d, mods, g_pre, w_in, w_pool, *cast_weights, *job_args)


class _Stage1Body:
    def __init__(self, x_ref, w_ref, o_ref):
        self.x_ref, self.w_ref, self.o_ref = x_ref, w_ref, o_ref

    def load(self):
        n1, c2, width = self.x_ref.shape
        self.xt = jnp.swapaxes(self.x_ref[...], 0, 1).reshape(
            n1 * c2 // SEQ_MINOR, SEQ_MINOR, width)

    def matmul(self):
        w = self.w_ref[...]
        self.z = jnp.stack([_dot(w, self.xt[g]).astype(BF16)
                            for g in range(self.xt.shape[0])])

    def store(self):
        n1, c2, width = self.x_ref.shape
        self.o_ref[...] = jnp.swapaxes(self.z.reshape(c2, n1, width), 0, 1)

    def run(self):
        self.load()
        self.matmul()
        self.store()


class _Stage1Job:
    def __init__(self, u4, w_bd, n_steps, lin):
        batch, n1, n2, width = u4.shape
        assert n_steps % batch == 0
        chunks = n_steps // batch
        c2 = n2 // chunks
        assert n2 % chunks == 0 and c2 % BF16_ROWS == 0 and (c2 * n1) % SEQ_MINOR == 0
        index = lambda *g: (lin(*g) // chunks, 0, lin(*g) % chunks, 0)
        self.args = (u4, w_bd)
        self.in_specs = [pl.BlockSpec((None, n1, c2, width), index),
                         _const_spec((SEQ_MINOR, SEQ_MINOR))]
        self.out_spec = pl.BlockSpec((None, n1, c2, width), index)
        self.out_shape = jax.ShapeDtypeStruct((batch, n1, n2, width), BF16)


def _packed_windows(n1, step):
    rb, half = ROW_BLOCKS_PER_STEP, n1 // 2
    upper = rb * step >= max(half, rb)
    re_start = jnp.where(upper, n1 - rb * step - (rb - 1), rb * step)
    im_start = jnp.minimum(jnp.where(upper, re_start + half, half + rb * step), n1 - rb)
    return re_start, im_start


def _stage2_body(n1, step, re_row, im_row, c2_ref, s2_ref, tc_ref, ts_ref, cs_ref, o_ref):
    c2 = c2_ref[...]
    s2 = s2_ref[...]
    half = n1 // 2
    qs = []
    for j in range(ROW_BLOCKS_PER_STEP):
        k1 = step * ROW_BLOCKS_PER_STEP + j
        mirrored = k1 > half
        kp = jnp.where(mirrored, n1 - k1, k1)
        real_only = (kp == 0) | (kp == half)
        sign = jnp.where(real_only, 0.0, jnp.where(mirrored, -1.0, 1.0))
        tc = tc_ref[pl.ds(k1, 1), :]
        ts = ts_ref[pl.ds(k1, 1), :]
        cos_t = c2 * tc - s2 * ts
        sin_t = s2 * tc + c2 * ts
        lhs = jnp.concatenate(
            [jnp.concatenate([cos_t, sign * sin_t], axis=1),
             jnp.concatenate([-sin_t, sign * cos_t], axis=1)], axis=0).astype(BF16)
        z = jnp.concatenate(
            [re_row(kp), im_row(jnp.where(real_only, half + 1, half + kp))], axis=0)
        qs.append(_dot(lhs, z).astype(BF16))
    halves = []
    for pair in range(N_GROUPS // 2):
        rows = []
        for q in qs:
            parts = []
            for g in (2 * pair, 2 * pair + 1):
                lanes = slice(g * GROUP_DIM, (g + 1) * GROUP_DIM)
                parts += [q[:SEQ_MINOR, lanes], q[SEQ_MINOR:, lanes]]
            rows.append(jnp.concatenate(parts, axis=1))
        halves.append(_dot(jnp.concatenate(rows, axis=0), cs_ref[...]))
    y = jnp.concatenate(halves, axis=1).astype(o_ref.dtype)
    o_ref[...] = y.reshape(o_ref.shape)


def _dft_stage2_kernel(n1, zr_ref, zi_ref, *table_and_out_refs):
    step = pl.program_id(1)
    re_start, im_start = _packed_windows(n1, step)
    _stage2_body(n1, step, lambda r: zr_ref[0, r - re_start],
                 lambda r: zi_ref[0, jnp.maximum(r - im_start, 0)], *table_and_out_refs)


def _dft_stage2(z4, tables):
    batch, n1, n2, width = z4.shape
    rb = ROW_BLOCKS_PER_STEP
    assert n1 % rb == 0 and n2 == SEQ_MINOR
    window = tuple(pl.Element(n) for n in (1, rb, n2, width))
    return pl.pallas_call(
        functools.partial(_dft_stage2_kernel, n1),
        grid=(batch, n1 // rb),
        in_specs=[
            pl.BlockSpec(window, lambda b, i: (b, _packed_windows(n1, i)[0], 0, 0)),
            pl.BlockSpec(window, lambda b, i: (b, _packed_windows(n1, i)[1], 0, 0)),
        ] + _stage2_table_specs(n1),
        out_specs=pl.BlockSpec((None, rb, n2, width), lambda b, i: (b, i, 0, 0)),
        out_shape=jax.ShapeDtypeStruct((batch, n1, n2, width), BF16),
        compiler_params=pltpu.CompilerParams(
            dimension_semantics=("arbitrary", "arbitrary"),
            vmem_limit_bytes=VMEM_LIMIT_BYTES),
        name="dft_stage2",
    )(z4, z4, *tables)


def _stage2_table_specs(n1):
    return [_const_spec((SEQ_MINOR, SEQ_MINOR)), _const_spec((SEQ_MINOR, SEQ_MINOR)),
            _const_spec((n1, SEQ_MINOR)), _const_spec((n1, SEQ_MINOR)),
            _const_spec((2 * MXU_DIM, MXU_DIM))]


def _dft_fused_kernel(x_ref, w_ref, c2_ref, s2_ref, tc_ref, ts_ref, cs_ref, o_ref, z_ref):
    _Stage1Body(x_ref, w_ref, z_ref).run()
    packed_row = lambda r: z_ref[r]
    _stage2_body(x_ref.shape[0], 0, packed_row, packed_row,
                 c2_ref, s2_ref, tc_ref, ts_ref, cs_ref, o_ref)


def _dft_fused(u4, w_bd, tables):
    batch, n1, n2, width = u4.shape
    seq_block = pl.BlockSpec((None, n1, n2, width), lambda b: (b, 0, 0, 0))
    return pl.pallas_call(
        _dft_fused_kernel,
        grid=(batch,),
        in_specs=[seq_block, _const_spec((SEQ_MINOR, SEQ_MINOR))] + _stage2_table_specs(n1),
        out_specs=seq_block,
        out_shape=jax.ShapeDtypeStruct((batch, n1, n2, width), BF16),
        scratch_shapes=[pltpu.VMEM((n1, n2, width), BF16)],
        compiler_params=pltpu.CompilerParams(
            dimension_semantics=("arbitrary",), vmem_limit_bytes=VMEM_LIMIT_BYTES),
        name="dft_fused",
    )(u4, w_bd, *tables)


@functools.lru_cache(maxsize=None)
def _dft_tables(seq_len):
    n2 = SEQ_MINOR
    n1 = seq_len // n2
    reps = n2 // n1
    k = np.arange(n1, dtype=np.float64)
    ang1 = 2.0 * np.pi * np.outer(k, k) / n1
    half = n1 // 2
    w1 = np.concatenate([np.cos(ang1)[:half + 1], -np.sin(ang1)[1:half]], axis=0)
    w_bd = np.kron(np.eye(reps), w1)
    m = np.arange(n2, dtype=np.float64)
    ang2 = 2.0 * np.pi * np.outer(m, m) / n2
    angt = 2.0 * np.pi * np.outer(k, m) / seq_len
    as32 = lambda a: np.asarray(a, dtype=np.float32)
    return (as32(w_bd), as32(np.cos(ang2)), as32(np.sin(ang2)),
            as32(np.cos(angt)), as32(np.sin(angt)))


@functools.lru_cache(maxsize=None)
def _channel_dft_table(seq_len):
    c = np.arange(GROUP_DIM, dtype=np.float64)
    ang = 2.0 * np.pi * np.outer(c, c) / GROUP_DIM
    norm = 1.0 / np.sqrt(float(seq_len) * GROUP_DIM)
    cs = np.concatenate([np.cos(ang), np.sin(ang)], axis=0) * norm
    return np.asarray(np.kron(np.eye(2), cs), dtype=np.float32)


def _stage1_weights(seq_len):
    return jnp.asarray(_dft_tables(seq_len)[0]).astype(BF16)


def _stage2_tables(seq_len):
    _, c2, s2, tc, ts = _dft_tables(seq_len)
    return (jnp.asarray(c2), jnp.asarray(s2), jnp.asarray(tc), jnp.asarray(ts),
            jnp.asarray(_channel_dft_table(seq_len)).astype(BF16))


def _as_blocks(u_f, batch, seq_len):
    return u_f.reshape(batch, seq_len // SEQ_MINOR, SEQ_MINOR, BRANCH_WIDTH)


def _mix_out_kernel(x_ref, y_ref, gf_ref, p_ref, mod_ref, gpm_ref, gpf_ref, gqf_ref,
                    wfo_ref, wout_ref, wgate_ref, wup_ref, wdown_ref, o_ref):
    mod = mod_ref[0]
    n_sub = MIX_OUT_TILE // MIX_OUT_SUB
    gate1 = mod[2:3, :] * gpm_ref[...]
    scale2 = gpf_ref[...] * (1.0 + mod[4:5, :])
    shift2 = mod[3:4, :]
    gate2 = mod[5:6, :] * gqf_ref[...]
    k2_sub = MIX_OUT_SUB // K1_PER_TILE

    def rows_of(ref, k2s):
        return ref[k2s].reshape(MIX_OUT_SUB, ref.shape[-1])

    def up_proj(h2, chunk):
        cols = pl.ds(*chunk)
        return _dot(h2, wgate_ref[:, cols]), _dot(h2, wup_ref[:, cols])

    def down_proj(ab, chunk):
        a, b = ab
        return _dot((a * _sigmoid(a) * b).astype(BF16), wdown_ref[pl.ds(*chunk), :])

    def front(group):
        ys = [jnp.swapaxes(y_ref[:, k2s, :], 0, 1).reshape(MIX_OUT_SUB, BRANCH_WIDTH)
              for k2s in group]
        y_fs = [_dot(y, wfo_ref[...]) for y in ys]
        ms = [_dot((rows_of(gf_ref, k2s).astype(F32) * y_f
                    + rows_of(p_ref, k2s).astype(F32)).astype(BF16), wout_ref[...])
              for k2s, y_f in zip(group, y_fs)]
        x1s = [rows_of(x_ref, k2s) + gate1 * (m * _rms_scale(m))
               for k2s, m in zip(group, ms)]
        h2s = [(x1 * _rms_scale(x1) * scale2 + shift2).astype(BF16) for x1 in x1s]
        return x1s, h2s

    def ffn(h2s):
        zs = [None] * len(h2s)
        pending = [up_proj(h2, FF_CHUNKS[0]) for h2 in h2s]
        for c, chunk in enumerate(FF_CHUNKS):
            last = c + 1 == len(FF_CHUNKS)
            nxt = None if last else [up_proj(h2, FF_CHUNKS[c + 1]) for h2 in h2s]
            for s in range(len(h2s)):
                part = down_proj(pending[s], chunk)
                zs[s] = part if zs[s] is None else zs[s] + part
            pending = nxt
        return zs

    def run_group(g):
        base = pl.multiple_of(g * (SUBS_PER_GROUP * k2_sub), SUBS_PER_GROUP * k2_sub)
        group = [pl.ds(base + s * k2_sub, k2_sub) for s in range(SUBS_PER_GROUP)]
        x1s, h2s = front(group)
        for k2s, x1, z in zip(group, x1s, ffn(h2s)):
            out = x1 + gate2 * (z * _rms_scale(z))
            o_ref[k2s] = out.reshape(k2_sub, K1_PER_TILE, D_MODEL)

    n_groups = n_sub // SUBS_PER_GROUP
    if n_groups == 1:
        run_group(0)
    else:
        lax.fori_loop(0, n_groups, lambda g, carry: (run_group(g), carry)[1], 0)


def _mix_out(x, y4, g_f, p, mods, g_post_mix, g_pre_ffn, g_post_ffn,
             w_fo, w_out, w_gate, w_up, w_down):
    batch, seq_len, d = x.shape
    n1 = y4.shape[1]
    n2 = SEQ_MINOR
    k1t, k2t = K1_PER_TILE, K2_PER_TILE
    assert n1 % k1t == 0 and n2 % k2t == 0 and k1t * k2t == MIX_OUT_TILE
    by_k = lambda a: a.reshape(batch, n2, n1 // k1t, k1t, a.shape[-1])
    tok_spec = pl.BlockSpec((None, k2t, None, k1t, d), lambda b, a, t: (b, a, t, 0, 0))
    return pl.pallas_call(
        _mix_out_kernel,
        grid=(batch, n2 // k2t, n1 // k1t),
        in_specs=[
            tok_spec,
            pl.BlockSpec((None, k1t, k2t, BRANCH_WIDTH), lambda b, a, t: (b, t, a, 0)),
            tok_spec,
            tok_spec,
            pl.BlockSpec((1, N_ADA, D_MODEL), lambda b, a, t: (b, 0, 0)),
            _const_spec((1, D_MODEL)),
            _const_spec((1, D_MODEL)),
            _const_spec((1, D_MODEL)),
            _const_spec((BRANCH_WIDTH, D_MODEL)),
            _const_spec((D_MODEL, D_MODEL)),
            _const_spec((D_MODEL, D_FF)),
            _const_spec((D_MODEL, D_FF)),
            _const_spec((D_FF, D_MODEL)),
        ],
        out_specs=tok_spec,
        out_shape=jax.ShapeDtypeStruct((batch, n2, n1 // k1t, k1t, d), F32),
        compiler_params=pltpu.CompilerParams(
            dimension_semantics=("arbitrary", "arbitrary", "arbitrary"),
            vmem_limit_bytes=VMEM_LIMIT_BYTES),
        name="mix_out",
    )(by_k(x), y4, by_k(g_f), by_k(p), mods, g_post_mix, g_pre_ffn, g_post_ffn,
      w_fo, w_out, w_gate, w_up, w_down).reshape(batch, seq_len, d)


def _layer_pair(x_a, mods_a, x_b, mods_b, early, gains, late):
    (ba, sa, d), (bb, sb, _) = x_a.shape, x_b.shape
    assert sb == ROW_BLOCKS_PER_STEP * SEQ_MINOR
    u_a, gf_a, p_a, *late = _mix_in(x_a.reshape(ba * sa, d), mods_a, sa, *early,
                                    cast_weights=late)
    u4_a, w_a = _as_blocks(u_a, ba, sa), _stage1_weights(sa)
    u_b, gf_b, p_b, z_a = _mix_in(
        x_b.reshape(bb * sb, d), mods_b, sb, *early,
        dft_job=lambda n_steps, lin: _Stage1Job(u4_a, w_a, n_steps, lin))
    y_a = _dft_stage2(z_a, _stage2_tables(sa))
    out_a = _mix_out(x_a, y_a, gf_a, p_a, mods_a, *gains, *late)
    y_b = _dft_fused(_as_blocks(u_b, bb, sb), _stage1_weights(sb), _stage2_tables(sb))
    out_b = _mix_out(x_b, y_b, gf_b, p_b, mods_b, *gains, *late)
    return out_a, out_b


def kernel(x_prompt, x_sample, c_prompt, c_sample, w_ada, b_ada, g_pre_mix, w_in, w_fo, w_pg,
           pool_scale, w_po, w_out, g_post_mix, g_pre_ffn, w_gate, w_up, w_down, g_post_ffn):
    depth = w_ada.shape[0]
    y_prompt, y_sample = x_prompt, x_sample
    bp, bs = c_prompt.shape[0], c_sample.shape[0]
    pad_rows = -(bp + bs) % BF16_ROWS
    c_pad = jnp.concatenate(
        [c_prompt, c_sample, jnp.zeros((pad_rows, D_MODEL), F32)], axis=0)
    for l in range(depth):
        ada = _ada(c_pad, w_ada[l], b_ada[l][None, :])
        mods_p = ada[:bp].reshape(bp, N_ADA, D_MODEL)
        mods_s = ada[bp:bp + bs].reshape(bs, N_ADA, D_MODEL)
        early = (g_pre_mix[l][None, :], w_in[l].astype(BF16),
                 _fold_pool_out(w_pg[l], pool_scale[l][None, :], w_po[l]))
        gains = (g_post_mix[l][None, :], g_pre_ffn[l][None, :], g_post_ffn[l][None, :])
        late = (w_fo[l], w_out[l], w_gate[l], w_up[l], w_down[l])
        y_prompt, y_sample = _layer_pair(y_prompt, mods_p, y_sample, mods_s, early, gains, late)
    return (y_prompt, y_sample)
```

```python
import functools

import numpy as np
import jax
import jax.numpy as jnp
from jax import lax
from jax.experimental import pallas as pl
from jax.experimental.pallas import tpu as pltpu

D_MODEL = 1024
N_GROUPS = 4
GROUP_DIM = 128
BRANCH_WIDTH = N_GROUPS * GROUP_DIM
POOL_WINDOWS = (2, 4, 8, 16)
D_FF = 2816
N_ADA = 6
EPS = 1e-6

F32_ROWS = 8
BF16_ROWS = 16
MXU_DIM = 256
VMEM_LIMIT_BYTES = 60 * 1024 * 1024

SEQ_MINOR = 128
HALO = BF16_ROWS
SUB_TILE = 256
MIX_IN_TILE = 1024
MIX_OUT_TILE = 1024
MIX_OUT_SUB = 256
SUBS_PER_GROUP = 2
K1_PER_TILE = BF16_ROWS
K2_PER_TILE = MIX_OUT_TILE // K1_PER_TILE
ROW_BLOCKS_PER_STEP = 16
PREP_ROWS = 128
FF_CHUNKS = ((0, 1024), (1024, 1024), (2048, 768))

BF16 = jnp.bfloat16
F32 = jnp.float32


def _dot(a, b):
    return jnp.dot(a, b, preferred_element_type=F32)


def _sigmoid(v):
    return 1.0 / (1.0 + jnp.exp(-v))


def _rms_scale(v):
    return lax.rsqrt(jnp.mean(v * v, axis=-1, keepdims=True) + EPS)


def _prep_kernel(c_ref, wada_ref, b_ref, win_ref, wpg_ref, ps_ref, wpo_ref,
                 ada_ref, winb_ref, wpool_ref):
    k = pl.program_id(0)

    @pl.when(k == 0)
    def _():
        ada_ref[...] = jnp.broadcast_to(b_ref[...], ada_ref.shape)
        for g in range(N_GROUPS):
            rows = pl.ds(g * GROUP_DIM, GROUP_DIM)
            scaled = wpg_ref[g] * ps_ref[:, rows]
            wpool_ref[rows, :] = jnp.dot(
                scaled, wpo_ref[rows, :], preferred_element_type=F32,
                precision=lax.Precision.HIGHEST).astype(wpool_ref.dtype)

    c = c_ref[:, pl.ds(pl.multiple_of(k * PREP_ROWS, PREP_ROWS), PREP_ROWS)]
    s = (c * _sigmoid(c)).astype(BF16)
    ada_ref[...] += _dot(s, wada_ref[...].astype(BF16))
    winb_ref[...] = win_ref[...].astype(winb_ref.dtype)


def _prep(c_pad, w_ada, b_ada, w_in, w_pg, pool_scale, w_po):
    rows, d = c_pad.shape
    n_ada, n_in = w_ada.shape[1], w_in.shape[1]
    block = lambda cols: pl.BlockSpec((PREP_ROWS, cols), lambda k: (k, 0))
    whole = lambda a: pl.BlockSpec(a.shape, lambda k: (0,) * a.ndim)
    return pl.pallas_call(
        _prep_kernel,
        grid=(d // PREP_ROWS,),
        in_specs=[whole(c_pad), block(n_ada), whole(b_ada), block(n_in),
                  whole(w_pg), whole(pool_scale), whole(w_po)],
        out_specs=[pl.BlockSpec((rows, n_ada), lambda k: (0, 0)), block(n_in),
                   pl.BlockSpec(w_po.shape, lambda k: (0, 0))],
        out_shape=[jax.ShapeDtypeStruct((rows, n_ada), F32),
                   jax.ShapeDtypeStruct(w_in.shape, BF16),
                   jax.ShapeDtypeStruct(w_po.shape, BF16)],
        compiler_params=pltpu.CompilerParams(
            dimension_semantics=("arbitrary",), vmem_limit_bytes=VMEM_LIMIT_BYTES),
        name="prep",
    )(c_pad, w_ada, b_ada, w_in, w_pg, pool_scale, w_po)


def _window_sum(ext, w):
    n = ext.shape[0]
    s = ext
    k = 1
    while k < w // 2:
        s = s + pltpu.roll(s, n - k, axis=0)
        k *= 2
    return s + pltpu.roll(s, w // 2, axis=0)


def _mix_in_kernel(tiles_per_seq, seq_len, n_cast, has_dft_job,
                   x_ref, xp_ref, xn_ref, mod_ref, g_ref, win_ref, wpool_ref, *rest):
    wf_ref = win_ref.at[:, pl.ds(0, BRANCH_WIDTH)]
    wp_ref = win_ref.at[:, pl.ds(BRANCH_WIDTH, BRANCH_WIDTH)]
    wg_ref = win_ref.at[:, pl.ds(2 * BRANCH_WIDTH, 2 * D_MODEL)]
    n_extra_in = n_cast + 2 * has_dft_job
    cast_in, dft_in = rest[:n_cast], rest[n_cast:n_extra_in]
    uf_ref, gf_ref, p_ref = rest[n_extra_in:n_extra_in + 3]
    cast_out = rest[n_extra_in + 3:n_extra_in + 3 + n_cast]
    dft = _Stage1Body(*dft_in, rest[-1]) if has_dft_job else None
    if dft:
        dft.load()
    for src, dst in zip(cast_in, cast_out):
        dst[...] = src[...].astype(dst.dtype)
    t = MIX_IN_TILE
    n_sub = t // SUB_TILE
    ti = lax.rem(pl.program_id(0), tiles_per_seq)
    mod = mod_ref[0]
    scale = g_ref[...] * (1.0 + mod[1:2, :])
    shift = mod[0:1, :]

    def normed(v):
        return v * _rms_scale(v) * scale + shift

    ts = SUB_TILE
    subs = [pl.ds(s * ts, ts) for s in range(n_sub)]
    hs = [normed(x_ref[rows, :]).astype(BF16) for rows in subs]
    h_prev = jnp.where(ti == 0, 0.0, normed(xp_ref[...])).astype(BF16)
    h_next = jnp.where(ti == tiles_per_seq - 1, 0.0, normed(xn_ref[...])).astype(BF16)
    afters = [h[:HALO] for h in hs[1:]] + [h_next]

    slab = F32_ROWS
    row = lax.broadcasted_iota(jnp.int32, (slab, GROUP_DIM), 0)

    def pool_diff(up, first_pos):
        diffs = []
        for g, w in enumerate(POOL_WINDOWS):
            assert w & (w - 1) == 0 and w // 2 <= slab
            left = w // 2
            right = w - 1 - left
            ext = up[:, g * GROUP_DIM:(g + 1) * GROUP_DIM]
            win = _window_sum(ext, w)[HALO:HALO + ts]
            u = ext[HALO:HALO + ts]

            def count(pos):
                return (jnp.minimum(pos + right, seq_len - 1)
                        - jnp.maximum(pos - left, 0) + 1).astype(F32)

            diffs.append(jnp.concatenate([
                win[:slab] / count(first_pos + row) - u[:slab],
                win[slab:ts - slab] * (1.0 / w) - u[slab:ts - slab],
                win[ts - slab:] / count(first_pos + (ts - slab) + row) - u[ts - slab:],
            ], axis=0))
        return diffs

    def project(s):
        head = [h_prev, hs[0][:HALO]] if s == 0 else []
        up = _dot(jnp.concatenate(head + [hs[s][HALO:], afters[s]], axis=0), wp_ref[...])
        uf_ref[subs[s], :] = _dot(hs[s], wf_ref[...]).astype(uf_ref.dtype)
        return up, _dot(hs[s], wg_ref[...])

    def pool_rows(s):
        if s == 0:
            return proj[0][0]
        return jnp.concatenate([proj[s - 1][0][-2 * HALO:], proj[s][0]], axis=0)

    def finish(s, up, gate_logits):
        d = jnp.concatenate(pool_diff(up, ti * t + s * ts), axis=1).astype(BF16)
        g = _sigmoid(gate_logits)
        gf_ref[subs[s], :] = g[:, :D_MODEL].astype(gf_ref.dtype)
        p_ref[subs[s], :] = (g[:, D_MODEL:] * _dot(d, wpool_ref[...])).astype(p_ref.dtype)

    proj = {}
    for step in range(n_sub + 1):
        if step < n_sub:
            proj[step] = project(step)
        if dft and step == 0:
            dft.matmul()
            dft.store()
        if step >= 1:
            finish(step - 1, pool_rows(step - 1), proj[step - 1][1])


def _const_spec(shape):
    zeros = (0,) * len(shape)
    return pl.BlockSpec(shape, lambda *_: zeros, pipeline_mode=pl.Buffered(1))


def _cast_specs(weights, n_steps):
    specs, shapes = [], []
    for w in weights:
        rows, cols = w.shape
        n_blocks = n_steps
        while rows % n_blocks or (rows // n_blocks) % BF16_ROWS:
            n_blocks //= 2
        assert n_blocks >= 1
        spec = pl.BlockSpec((rows // n_blocks, cols),
                            lambda i, last=n_blocks - 1: (jnp.minimum(i, last), 0))
        specs.append(spec)
        shapes.append(jax.ShapeDtypeStruct(w.shape, BF16))
    return specs, shapes


def _mix_in(x2d, mods, seq_len, g_pre, w_in, w_pool, cast_weights=(), dft_job=None):
    n_tok = x2d.shape[0]
    t = MIX_IN_TILE
    tiles_per_seq = seq_len // t
    halo_blocks_per_tile = t // HALO
    n_halo_blocks = n_tok // HALO
    row = lambda i: (i, 0)
    cast_specs, cast_shapes = _cast_specs(cast_weights, n_tok // t)
    job = dft_job(n_tok // t, lambda i: i) if dft_job else None
    job_in = job.in_specs if job else []
    job_out = [job.out_spec] if job else []
    job_shape = [job.out_shape] if job else []
    job_args = job.args if job else ()
    return pl.pallas_call(
        functools.partial(_mix_in_kernel, tiles_per_seq, seq_len, len(cast_weights),
                          job is not None),
        grid=(n_tok // t,),
        in_specs=[
            pl.BlockSpec((t, D_MODEL), row),
            pl.BlockSpec((HALO, D_MODEL),
                         lambda i: (jnp.maximum(i * halo_blocks_per_tile - 1, 0), 0)),
            pl.BlockSpec((HALO, D_MODEL),
                         lambda i: (jnp.minimum((i + 1) * halo_blocks_per_tile,
                                                n_halo_blocks - 1), 0)),
            pl.BlockSpec((1, N_ADA, D_MODEL), lambda i: (i // tiles_per_seq, 0, 0)),
            _const_spec((1, D_MODEL)),
            _const_spec((D_MODEL, 2 * BRANCH_WIDTH + 2 * D_MODEL)),
            _const_spec((BRANCH_WIDTH, D_MODEL)),
        ] + cast_specs + job_in,
        out_specs=[
            pl.BlockSpec((t, BRANCH_WIDTH), row),
            pl.BlockSpec((t, D_MODEL), row),
            pl.BlockSpec((t, D_MODEL), row),
        ] + cast_specs + job_out,
        out_shape=[
            jax.ShapeDtypeStruct((n_tok, BRANCH_WIDTH), BF16),
            jax.ShapeDtypeStruct((n_tok, D_MODEL), BF16),
            jax.ShapeDtypeStruct((n_tok, D_MODEL), BF16),
        ] + cast_shapes + job_shape,
        compiler_params=pltpu.CompilerParams(
            dimension_semantics=("arbitrary",), vmem_limit_bytes=VMEM_LIMIT_BYTES),
        name="mix_in",
    )(x2d, x2d, x2d, mods, g_pre, w_in, w_pool, *cast_weights, *job_args)


class _Stage1Body:
    def __init__(self, x_ref, w_ref, o_ref):
        self.x_ref, self.w_ref, self.o_ref = x_ref, w_ref, o_ref

    def load(self):
        n1, c2, width = self.x_ref.shape
        self.xt = jnp.swapaxes(self.x_ref[...], 0, 1).reshape(
            n1 * c2 // SEQ_MINOR, SEQ_MINOR, width)

    def matmul(self):
        w = self.w_ref[...]
        self.z = jnp.stack([_dot(w, self.xt[g]).astype(BF16)
                            for g in range(self.xt.shape[0])])

    def store(self):
        n1, c2, width = self.x_ref.shape
        self.o_ref[...] = jnp.swapaxes(self.z.reshape(c2, n1, width), 0, 1)

    def run(self):
        self.load()
        self.matmul()
        self.store()


class _Stage1Job:
    def __init__(self, u4, w_bd, n_steps, lin):
        batch, n1, n2, width = u4.shape
        assert n_steps % batch == 0
        chunks = n_steps // batch
        c2 = n2 // chunks
        assert n2 % chunks == 0 and c2 % BF16_ROWS == 0 and (c2 * n1) % SEQ_MINOR == 0
        index = lambda *g: (lin(*g) // chunks, 0, lin(*g) % chunks, 0)
        self.args = (u4, w_bd)
        self.in_specs = [pl.BlockSpec((None, n1, c2, width), index),
                         _const_spec((SEQ_MINOR, SEQ_MINOR))]
        self.out_spec = pl.BlockSpec((None, n1, c2, width), index)
        self.out_shape = jax.ShapeDtypeStruct((batch, n1, n2, width), BF16)


def _packed_windows(n1, step):
    rb, half = ROW_BLOCKS_PER_STEP, n1 // 2
    upper = rb * step >= max(half, rb)
    re_start = jnp.where(upper, n1 - rb * step - (rb - 1), rb * step)
    im_start = jnp.minimum(jnp.where(upper, re_start + half, half + rb * step), n1 - rb)
    return re_start, im_start


def _stage2_body(n1, step, re_row, im_row, c2_ref, s2_ref, tc_ref, ts_ref, cs_ref, o_ref):
    c2 = c2_ref[...]
    s2 = s2_ref[...]
    half = n1 // 2
    qs = []
    for j in range(ROW_BLOCKS_PER_STEP):
        k1 = step * ROW_BLOCKS_PER_STEP + j
        mirrored = k1 > half
        kp = jnp.where(mirrored, n1 - k1, k1)
        real_only = (kp == 0) | (kp == half)
        sign = jnp.where(real_only, 0.0, jnp.where(mirrored, -1.0, 1.0))
        tc = tc_ref[pl.ds(k1, 1), :]
        ts = ts_ref[pl.ds(k1, 1), :]
        cos_t = c2 * tc - s2 * ts
        sin_t = s2 * tc + c2 * ts
        lhs = jnp.concatenate(
            [jnp.concatenate([cos_t, sign * sin_t], axis=1),
             jnp.concatenate([-sin_t, sign * cos_t], axis=1)], axis=0).astype(BF16)
        z = jnp.concatenate(
            [re_row(kp), im_row(jnp.where(real_only, half + 1, half + kp))], axis=0)
        qs.append(_dot(lhs, z).astype(BF16))
    halves = []
    for pair in range(N_GROUPS // 2):
        rows = []
        for q in qs:
            parts = []
            for g in (2 * pair, 2 * pair + 1):
                lanes = slice(g * GROUP_DIM, (g + 1) * GROUP_DIM)
                parts += [q[:SEQ_MINOR, lanes], q[SEQ_MINOR:, lanes]]
            rows.append(jnp.concatenate(parts, axis=1))
        halves.append(_dot(jnp.concatenate(rows, axis=0), cs_ref[...]))
    y = jnp.concatenate(halves, axis=1).astype(o_ref.dtype)
    o_ref[...] = y.reshape(o_ref.shape)


def _dft_stage2_kernel(n1, zr_ref, zi_ref, *table_and_out_refs):
    step = pl.program_id(1)
    re_start, im_start = _packed_windows(n1, step)
    _stage2_body(n1, step, lambda r: zr_ref[0, r - re_start],
                 lambda r: zi_ref[0, jnp.maximum(r - im_start, 0)], *table_and_out_refs)


def _dft_stage2(z4, tables):
    batch, n1, n2, width = z4.shape
    rb = ROW_BLOCKS_PER_STEP
    assert n1 % rb == 0 and n2 == SEQ_MINOR
    window = tuple(pl.Element(n) for n in (1, rb, n2, width))
    return pl.pallas_call(
        functools.partial(_dft_stage2_kernel, n1),
        grid=(batch, n1 // rb),
        in_specs=[
            pl.BlockSpec(window, lambda b, i: (b, _packed_windows(n1, i)[0], 0, 0)),
            pl.BlockSpec(window, lambda b, i: (b, _packed_windows(n1, i)[1], 0, 0)),
        ] + _stage2_table_specs(n1),
        out_specs=pl.BlockSpec((None, rb, n2, width), lambda b, i: (b, i, 0, 0)),
        out_shape=jax.ShapeDtypeStruct((batch, n1, n2, width), BF16),
        compiler_params=pltpu.CompilerParams(
            dimension_semantics=("arbitrary", "arbitrary"),
            vmem_limit_bytes=VMEM_LIMIT_BYTES),
        name="dft_stage2",
    )(z4, z4, *tables)


def _stage2_table_specs(n1):
    return [_const_spec((SEQ_MINOR, SEQ_MINOR)), _const_spec((SEQ_MINOR, SEQ_MINOR)),
            _const_spec((n1, SEQ_MINOR)), _const_spec((n1, SEQ_MINOR)),
            _const_spec((2 * MXU_DIM, MXU_DIM))]


def _dft_fused_kernel(x_ref, w_ref, c2_ref, s2_ref, tc_ref, ts_ref, cs_ref, o_ref, z_ref):
    _Stage1Body(x_ref, w_ref, z_ref).run()
    packed_row = lambda r: z_ref[r]
    _stage2_body(x_ref.shape[0], 0, packed_row, packed_row,
                 c2_ref, s2_ref, tc_ref, ts_ref, cs_ref, o_ref)


def _dft_fused(u4, w_bd, tables):
    batch, n1, n2, width = u4.shape
    seq_block = pl.BlockSpec((None, n1, n2, width), lambda b: (b, 0, 0, 0))
    return pl.pallas_call(
        _dft_fused_kernel,
        grid=(batch,),
        in_specs=[seq_block, _const_spec((SEQ_MINOR, SEQ_MINOR))] + _stage2_table_specs(n1),
        out_specs=seq_block,
        out_shape=jax.ShapeDtypeStruct((batch, n1, n2, width), BF16),
        scratch_shapes=[pltpu.VMEM((n1, n2, width), BF16)],
        compiler_params=pltpu.CompilerParams(
            dimension_semantics=("arbitrary",), vmem_limit_bytes=VMEM_LIMIT_BYTES),
        name="dft_fused",
    )(u4, w_bd, *tables)


@functools.lru_cache(maxsize=None)
def _dft_tables(seq_len):
    n2 = SEQ_MINOR
    n1 = seq_len // n2
    reps = n2 // n1
    k = np.arange(n1, dtype=np.float64)
    ang1 = 2.0 * np.pi * np.outer(k, k) / n1
    half = n1 // 2
    w1 = np.concatenate([np.cos(ang1)[:half + 1], -np.sin(ang1)[1:half]], axis=0)
    w_bd = np.kron(np.eye(reps), w1)
    m = np.arange(n2, dtype=np.float64)
    ang2 = 2.0 * np.pi * np.outer(m, m) / n2
    angt = 2.0 * np.pi * np.outer(k, m) / seq_len
    as32 = lambda a: np.asarray(a, dtype=np.float32)
    return (as32(w_bd), as32(np.cos(ang2)), as32(np.sin(ang2)),
            as32(np.cos(angt)), as32(np.sin(angt)))


@functools.lru_cache(maxsize=None)
def _channel_dft_table(seq_len):
    c = np.arange(GROUP_DIM, dtype=np.float64)
    ang = 2.0 * np.pi * np.outer(c, c) / GROUP_DIM
    norm = 1.0 / np.sqrt(float(seq_len) * GROUP_DIM)
    cs = np.concatenate([np.cos(ang), np.sin(ang)], axis=0) * norm
    return np.asarray(np.kron(np.eye(2), cs), dtype=np.float32)


def _stage1_weights(seq_len):
    return jnp.asarray(_dft_tables(seq_len)[0]).astype(BF16)


def _stage2_tables(seq_len):
    _, c2, s2, tc, ts = _dft_tables(seq_len)
    return (jnp.asarray(c2), jnp.asarray(s2), jnp.asarray(tc), jnp.asarray(ts),
            jnp.asarray(_channel_dft_table(seq_len)).astype(BF16))


def _as_blocks(u_f, batch, seq_len):
    return u_f.reshape(batch, seq_len // SEQ_MINOR, SEQ_MINOR, BRANCH_WIDTH)


def _mix_out_kernel(x_ref, y_ref, gf_ref, p_ref, mod_ref, gpm_ref, gpf_ref, gqf_ref,
                    wfo_ref, wout_ref, wgate_ref, wup_ref, wdown_ref, o_ref):
    mod = mod_ref[0]
    n_sub = MIX_OUT_TILE // MIX_OUT_SUB
    gate1 = mod[2:3, :] * gpm_ref[...]
    scale2 = gpf_ref[...] * (1.0 + mod[4:5, :])
    shift2 = mod[3:4, :]
    gate2 = mod[5:6, :] * gqf_ref[...]
    k2_sub = MIX_OUT_SUB // K1_PER_TILE

    def rows_of(ref, k2s):
        return ref[k2s].reshape(MIX_OUT_SUB, ref.shape[-1])

    def up_proj(h2, chunk):
        cols = pl.ds(*chunk)
        return _dot(h2, wgate_ref[:, cols]), _dot(h2, wup_ref[:, cols])

    def down_proj(ab, chunk):
        a, b = ab
        return _dot((a * _sigmoid(a) * b).astype(BF16), wdown_ref[pl.ds(*chunk), :])

    def front(group):
        ys = [jnp.swapaxes(y_ref[:, k2s, :], 0, 1).reshape(MIX_OUT_SUB, BRANCH_WIDTH)
              for k2s in group]
        y_fs = [_dot(y, wfo_ref[...]) for y in ys]
        ms = [_dot((rows_of(gf_ref, k2s).astype(F32) * y_f
                    + rows_of(p_ref, k2s).astype(F32)).astype(BF16), wout_ref[...])
              for k2s, y_f in zip(group, y_fs)]
        x1s = [rows_of(x_ref, k2s) + gate1 * (m * _rms_scale(m))
               for k2s, m in zip(group, ms)]
        h2s = [(x1 * _rms_scale(x1) * scale2 + shift2).astype(BF16) for x1 in x1s]
        return x1s, h2s

    def ffn(h2s):
        zs = [None] * len(h2s)
        pending = [up_proj(h2, FF_CHUNKS[0]) for h2 in h2s]
        for c, chunk in enumerate(FF_CHUNKS):
            last = c + 1 == len(FF_CHUNKS)
            nxt = None if last else [up_proj(h2, FF_CHUNKS[c + 1]) for h2 in h2s]
            for s in range(len(h2s)):
                part = down_proj(pending[s], chunk)
                zs[s] = part if zs[s] is None else zs[s] + part
            pending = nxt
        return zs

    def run_group(g):
        base = pl.multiple_of(g * (SUBS_PER_GROUP * k2_sub), SUBS_PER_GROUP * k2_sub)
        group = [pl.ds(base + s * k2_sub, k2_sub) for s in range(SUBS_PER_GROUP)]
        x1s, h2s = front(group)
        for k2s, x1, z in zip(group, x1s, ffn(h2s)):
            out = x1 + gate2 * (z * _rms_scale(z))
            o_ref[k2s] = out.reshape(k2_sub, K1_PER_TILE, D_MODEL)

    n_groups = n_sub // SUBS_PER_GROUP
    if n_groups == 1:
        run_group(0)
    else:
        lax.fori_loop(0, n_groups, lambda g, carry: (run_group(g), carry)[1], 0)


def _mix_out(x, y4, g_f, p, mods, g_post_mix, g_pre_ffn, g_post_ffn,
             w_fo, w_out, w_gate, w_up, w_down):
    batch, seq_len, d = x.shape
    n1 = y4.shape[1]
    n2 = SEQ_MINOR
    k1t, k2t = K1_PER_TILE, K2_PER_TILE
    assert n1 % k1t == 0 and n2 % k2t == 0 and k1t * k2t == MIX_OUT_TILE
    by_k = lambda a: a.reshape(batch, n2, n1 // k1t, k1t, a.shape[-1])
    tok_spec = pl.BlockSpec((None, k2t, None, k1t, d), lambda b, a, t: (b, a, t, 0, 0))
    return pl.pallas_call(
        _mix_out_kernel,
        grid=(batch, n2 // k2t, n1 // k1t),
        in_specs=[
            tok_spec,
            pl.BlockSpec((None, k1t, k2t, BRANCH_WIDTH), lambda b, a, t: (b, t, a, 0)),
            tok_spec,
            tok_spec,
            pl.BlockSpec((1, N_ADA, D_MODEL), lambda b, a, t: (b, 0, 0)),
            _const_spec((1, D_MODEL)),
            _const_spec((1, D_MODEL)),
            _const_spec((1, D_MODEL)),
            _const_spec((BRANCH_WIDTH, D_MODEL)),
            _const_spec((D_MODEL, D_MODEL)),
            _const_spec((D_MODEL, D_FF)),
            _const_spec((D_MODEL, D_FF)),
            _const_spec((D_FF, D_MODEL)),
        ],
        out_specs=tok_spec,
        out_shape=jax.ShapeDtypeStruct((batch, n2, n1 // k1t, k1t, d), F32),
        compiler_params=pltpu.CompilerParams(
            dimension_semantics=("arbitrary", "arbitrary", "arbitrary"),
            vmem_limit_bytes=VMEM_LIMIT_BYTES),
        name="mix_out",
    )(by_k(x), y4, by_k(g_f), by_k(p), mods, g_post_mix, g_pre_ffn, g_post_ffn,
      w_fo, w_out, w_gate, w_up, w_down).reshape(batch, seq_len, d)


def _layer_pair(x_a, mods_a, x_b, mods_b, early, gains, late):
    (ba, sa, d), (bb, sb, _) = x_a.shape, x_b.shape
    assert sb == ROW_BLOCKS_PER_STEP * SEQ_MINOR
    u_a, gf_a, p_a, *late = _mix_in(x_a.reshape(ba * sa, d), mods_a, sa, *early,
                                    cast_weights=late)
    u4_a, w_a = _as_blocks(u_a, ba, sa), _stage1_weights(sa)
    u_b, gf_b, p_b, z_a = _mix_in(
        x_b.reshape(bb * sb, d), mods_b, sb, *early,
        dft_job=lambda n_steps, lin: _Stage1Job(u4_a, w_a, n_steps, lin))
    y_a = _dft_stage2(z_a, _stage2_tables(sa))
    out_a = _mix_out(x_a, y_a, gf_a, p_a, mods_a, *gains, *late)
    y_b = _dft_fused(_as_blocks(u_b, bb, sb), _stage1_weights(sb), _stage2_tables(sb))
    out_b = _mix_out(x_b, y_b, gf_b, p_b, mods_b, *gains, *late)
    return out_a, out_b


def kernel(x_prompt, x_sample, c_prompt, c_sample, w_ada, b_ada, g_pre_mix, w_in, w_fo, w_pg,
           pool_scale, w_po, w_out, g_post_mix, g_pre_ffn, w_gate, w_up, w_down, g_post_ffn):
    depth = w_ada.shape[0]
    y_prompt, y_sample = x_prompt, x_sample
    bp, bs = c_prompt.shape[0], c_sample.shape[0]
    pad_rows = -(bp + bs) % BF16_ROWS
    c_pad = jnp.concatenate(
        [c_prompt, c_sample, jnp.zeros((pad_rows, D_MODEL), F32)], axis=0)
    for l in range(depth):
        ada, w_in_bf16, w_pool = _prep(c_pad, w_ada[l], b_ada[l][None, :], w_in[l], w_pg[l],
                                       pool_scale[l][None, :], w_po[l])
        mods_p = ada[:bp].reshape(bp, N_ADA, D_MODEL)
        mods_s = ada[bp:bp + bs].reshape(bs, N_ADA, D_MODEL)
        early = (g_pre_mix[l][None, :], w_in_bf16, w_pool)
        gains = (g_post_mix[l][None, :], g_pre_ffn[l][None, :], g_post_ffn[l][None, :])
        late = (w_fo[l], w_out[l], w_gate[l], w_up[l], w_down[l])
        y_prompt, y_sample = _layer_pair(y_prompt, mods_p, y_sample, mods_s, early, gains, late)
    return (y_prompt, y_sample)
```

```python
import functools

import numpy as np
import jax
import jax.numpy as jnp
from jax import lax
from jax.experimental import pallas as pl
from jax.experimental.pallas import tpu as pltpu

D_MODEL = 1024
N_GROUPS = 4
GROUP_DIM = 128
BRANCH_WIDTH = N_GROUPS * GROUP_DIM
POOL_WINDOWS = (2, 4, 8, 16)
D_FF = 2816
N_ADA = 6
EPS = 1e-6

F32_ROWS = 8
BF16_ROWS = 16
MXU_DIM = 256
VMEM_LIMIT_BYTES = 60 * 1024 * 1024

SEQ_MINOR = 128
HALO = BF16_ROWS
SUB_TILE = 256
MIX_IN_TILE = 1024
MIX_OUT_TILE = 1024
MIX_OUT_SUB = 256
SUBS_PER_GROUP = 2
K1_PER_TILE = BF16_ROWS
K2_PER_TILE = MIX_OUT_TILE // K1_PER_TILE
ROW_BLOCKS_PER_STEP = 16
PREP_ROWS = 128
FF_CHUNKS = ((0, 1024), (1024, 1024), (2048, 768))

BF16 = jnp.bfloat16
F32 = jnp.float32


def _dot(a, b):
    return jnp.dot(a, b, preferred_element_type=F32)


def _sigmoid(v):
    return 1.0 / (1.0 + jnp.exp(-v))


def _rms_scale(v):
    return lax.rsqrt(jnp.mean(v * v, axis=-1, keepdims=True) + EPS)


def _prep_kernel(c_ref, wada_ref, b_ref, win_ref, wpg_ref, ps_ref, wpo_ref,
                 ada_ref, winb_ref, wpool_ref):
    k = pl.program_id(0)

    @pl.when(k == 0)
    def _():
        ada_ref[...] = jnp.broadcast_to(b_ref[...], ada_ref.shape)
        for g in range(N_GROUPS):
            rows = pl.ds(g * GROUP_DIM, GROUP_DIM)
            scaled = wpg_ref[g] * ps_ref[:, rows]
            wpool_ref[rows, :] = jnp.dot(
                scaled, wpo_ref[rows, :], preferred_element_type=F32,
                precision=lax.Precision.HIGHEST).astype(wpool_ref.dtype)

    c = c_ref[:, pl.ds(pl.multiple_of(k * PREP_ROWS, PREP_ROWS), PREP_ROWS)]
    s = (c * _sigmoid(c)).astype(BF16)
    ada_ref[...] += _dot(s, wada_ref[...].astype(BF16))
    winb_ref[...] = win_ref[...].astype(winb_ref.dtype)


def _prep(c_pad, w_ada, b_ada, w_in, w_pg, pool_scale, w_po):
    rows, d = c_pad.shape
    n_ada, n_in = w_ada.shape[1], w_in.shape[1]
    block = lambda cols: pl.BlockSpec((PREP_ROWS, cols), lambda k: (k, 0))
    whole = lambda a: pl.BlockSpec(a.shape, lambda k: (0,) * a.ndim)
    return pl.pallas_call(
        _prep_kernel,
        grid=(d // PREP_ROWS,),
        in_specs=[whole(c_pad), block(n_ada), whole(b_ada), block(n_in),
                  whole(w_pg), whole(pool_scale), whole(w_po)],
        out_specs=[pl.BlockSpec((rows, n_ada), lambda k: (0, 0)), block(n_in),
                   pl.BlockSpec(w_po.shape, lambda k: (0, 0))],
        out_shape=[jax.ShapeDtypeStruct((rows, n_ada), F32),
                   jax.ShapeDtypeStruct(w_in.shape, BF16),
                   jax.ShapeDtypeStruct(w_po.shape, BF16)],
        compiler_params=pltpu.CompilerParams(
            dimension_semantics=("arbitrary",), vmem_limit_bytes=VMEM_LIMIT_BYTES),
        name="prep",
    )(c_pad, w_ada, b_ada, w_in, w_pg, pool_scale, w_po)


def _window_sum(ext, w):
    n = ext.shape[0]
    s = ext
    k = 1
    while k < w // 2:
        s = s + pltpu.roll(s, n - k, axis=0)
        k *= 2
    return s + pltpu.roll(s, w // 2, axis=0)


def _mix_in_kernel(tiles_per_seq, seq_len, n_cast,
                   x_ref, xp_ref, xn_ref, mod_ref, g_ref, win_ref, wpool_ref, *rest):
    wf_ref = win_ref.at[:, pl.ds(0, BRANCH_WIDTH)]
    wp_ref = win_ref.at[:, pl.ds(BRANCH_WIDTH, BRANCH_WIDTH)]
    wg_ref = win_ref.at[:, pl.ds(2 * BRANCH_WIDTH, 2 * D_MODEL)]
    cast_in, (uf_ref, gf_ref, p_ref), cast_out = (
        rest[:n_cast], rest[n_cast:n_cast + 3], rest[n_cast + 3:])
    for src, dst in zip(cast_in, cast_out):
        dst[...] = src[...].astype(dst.dtype)
    t = MIX_IN_TILE
    n_sub = t // SUB_TILE
    ti = lax.rem(pl.program_id(0), tiles_per_seq)
    mod = mod_ref[0]
    scale = g_ref[...] * (1.0 + mod[1:2, :])
    shift = mod[0:1, :]

    def normed(v):
        return v * _rms_scale(v) * scale + shift

    ts = SUB_TILE
    subs = [pl.ds(s * ts, ts) for s in range(n_sub)]
    hs = [normed(x_ref[rows, :]).astype(BF16) for rows in subs]
    h_prev = jnp.where(ti == 0, 0.0, normed(xp_ref[...])).astype(BF16)
    h_next = jnp.where(ti == tiles_per_seq - 1, 0.0, normed(xn_ref[...])).astype(BF16)
    afters = [h[:HALO] for h in hs[1:]] + [h_next]

    slab = F32_ROWS
    row = lax.broadcasted_iota(jnp.int32, (slab, GROUP_DIM), 0)

    def pool_diff(up, first_pos):
        diffs = []
        for g, w in enumerate(POOL_WINDOWS):
            assert w & (w - 1) == 0 and w // 2 <= slab
            left = w // 2
            right = w - 1 - left
            ext = up[:, g * GROUP_DIM:(g + 1) * GROUP_DIM]
            win = _window_sum(ext, w)[HALO:HALO + ts]
            u = ext[HALO:HALO + ts]

            def count(pos):
                return (jnp.minimum(pos + right, seq_len - 1)
                        - jnp.maximum(pos - left, 0) + 1).astype(F32)

            diffs.append(jnp.concatenate([
                win[:slab] / count(first_pos + row) - u[:slab],
                win[slab:ts - slab] * (1.0 / w) - u[slab:ts - slab],
                win[ts - slab:] / count(first_pos + (ts - slab) + row) - u[ts - slab:],
            ], axis=0))
        return diffs

    def project(s):
        head = [h_prev, hs[0][:HALO]] if s == 0 else []
        up = _dot(jnp.concatenate(head + [hs[s][HALO:], afters[s]], axis=0), wp_ref[...])
        uf_ref[subs[s], :] = _dot(hs[s], wf_ref[...]).astype(uf_ref.dtype)
        return up, _dot(hs[s], wg_ref[...])

    def pool_rows(s):
        if s == 0:
            return proj[0][0]
        return jnp.concatenate([proj[s - 1][0][-2 * HALO:], proj[s][0]], axis=0)

    def finish(s, up, gate_logits):
        d = jnp.concatenate(pool_diff(up, ti * t + s * ts), axis=1).astype(BF16)
        g = _sigmoid(gate_logits)
        gf_ref[subs[s], :] = g[:, :D_MODEL].astype(gf_ref.dtype)
        p_ref[subs[s], :] = (g[:, D_MODEL:] * _dot(d, wpool_ref[...])).astype(p_ref.dtype)

    proj = {}
    for step in range(n_sub + 1):
        if step < n_sub:
            proj[step] = project(step)
        if step >= 1:
            finish(step - 1, pool_rows(step - 1), proj[step - 1][1])


def _const_spec(shape):
    zeros = (0,) * len(shape)
    return pl.BlockSpec(shape, lambda *_: zeros, pipeline_mode=pl.Buffered(1))


def _cast_specs(weights, n_steps):
    specs, shapes = [], []
    for w in weights:
        rows, cols = w.shape
        n_blocks = n_steps
        while rows % n_blocks or (rows // n_blocks) % BF16_ROWS:
            n_blocks //= 2
        assert n_blocks >= 1
        spec = pl.BlockSpec((rows // n_blocks, cols),
                            lambda i, last=n_blocks - 1: (jnp.minimum(i, last), 0))
        specs.append(spec)
        shapes.append(jax.ShapeDtypeStruct(w.shape, BF16))
    return specs, shapes


def _mix_in(x2d, mods, seq_len, g_pre, w_in, w_pool, cast_weights=()):
    n_tok = x2d.shape[0]
    t = MIX_IN_TILE
    tiles_per_seq = seq_len // t
    halo_blocks_per_tile = t // HALO
    n_halo_blocks = n_tok // HALO
    row = lambda i: (i, 0)
    cast_specs, cast_shapes = _cast_specs(cast_weights, n_tok // t)
    return pl.pallas_call(
        functools.partial(_mix_in_kernel, tiles_per_seq, seq_len, len(cast_weights)),
        grid=(n_tok // t,),
        in_specs=[
            pl.BlockSpec((t, D_MODEL), row),
            pl.BlockSpec((HALO, D_MODEL),
                         lambda i: (jnp.maximum(i * halo_blocks_per_tile - 1, 0), 0)),
            pl.BlockSpec((HALO, D_MODEL),
                         lambda i: (jnp.minimum((i + 1) * halo_blocks_per_tile,
                                                n_halo_blocks - 1), 0)),
            pl.BlockSpec((1, N_ADA, D_MODEL), lambda i: (i // tiles_per_seq, 0, 0)),
            _const_spec((1, D_MODEL)),
            _const_spec((D_MODEL, 2 * BRANCH_WIDTH + 2 * D_MODEL)),
            _const_spec((BRANCH_WIDTH, D_MODEL)),
        ] + cast_specs,
        out_specs=[
            pl.BlockSpec((t, BRANCH_WIDTH), row),
            pl.BlockSpec((t, D_MODEL), row),
            pl.BlockSpec((t, D_MODEL), row),
        ] + cast_specs,
        out_shape=[
            jax.ShapeDtypeStruct((n_tok, BRANCH_WIDTH), BF16),
            jax.ShapeDtypeStruct((n_tok, D_MODEL), BF16),
            jax.ShapeDtypeStruct((n_tok, D_MODEL), BF16),
        ] + cast_shapes,
        compiler_params=pltpu.CompilerParams(
            dimension_semantics=("arbitrary",), vmem_limit_bytes=VMEM_LIMIT_BYTES),
        name="mix_in",
    )(x2d, x2d, x2d, mods, g_pre, w_in, w_pool, *cast_weights)


def _stage1_body(x_ref, w_ref, o_ref):
    n1, c2, width = x_ref.shape
    groups = n1 * c2 // SEQ_MINOR
    w = w_ref[...]
    xt = jnp.swapaxes(x_ref[...], 0, 1).reshape(groups, SEQ_MINOR, width)
    z = jnp.stack([_dot(w, xt[g]).astype(BF16) for g in range(groups)])
    o_ref[...] = jnp.swapaxes(z.reshape(c2, n1, width), 0, 1)


def _stage2_body(n1, step, z_ref, c2_ref, s2_ref, tc_ref, ts_ref, cs_ref, o_ref):
    c2 = c2_ref[...]
    s2 = s2_ref[...]
    half = n1 // 2
    qs = []
    for j in range(ROW_BLOCKS_PER_STEP):
        k1 = step * ROW_BLOCKS_PER_STEP + j
        mirrored = k1 > half
        kp = jnp.where(mirrored, n1 - k1, k1)
        real_only = (kp == 0) | (kp == half)
        sign = jnp.where(real_only, 0.0, jnp.where(mirrored, -1.0, 1.0))
        tc = tc_ref[pl.ds(k1, 1), :]
        ts = ts_ref[pl.ds(k1, 1), :]
        cos_t = c2 * tc - s2 * ts
        sin_t = s2 * tc + c2 * ts
        lhs = jnp.concatenate(
            [jnp.concatenate([cos_t, sign * sin_t], axis=1),
             jnp.concatenate([-sin_t, sign * cos_t], axis=1)], axis=0).astype(BF16)
        z = jnp.concatenate(
            [z_ref[kp], z_ref[jnp.where(real_only, half + 1, half + kp)]], axis=0)
        qs.append(_dot(lhs, z).astype(BF16))
    halves = []
    for pair in range(N_GROUPS // 2):
        rows = []
        for q in qs:
            parts = []
            for g in (2 * pair, 2 * pair + 1):
                lanes = slice(g * GROUP_DIM, (g + 1) * GROUP_DIM)
                parts += [q[:SEQ_MINOR, lanes], q[SEQ_MINOR:, lanes]]
            rows.append(jnp.concatenate(parts, axis=1))
        halves.append(_dot(jnp.concatenate(rows, axis=0), cs_ref[...]))
    y = jnp.concatenate(halves, axis=1).astype(o_ref.dtype)
    o_ref[...] = y.reshape(o_ref.shape)


def _stage2_table_specs(n1):
    return [_const_spec((SEQ_MINOR, SEQ_MINOR)), _const_spec((SEQ_MINOR, SEQ_MINOR)),
            _const_spec((n1, SEQ_MINOR)), _const_spec((n1, SEQ_MINOR)),
            _const_spec((2 * MXU_DIM, MXU_DIM))]


def _dft_kernel(steps, x_ref, w_ref, c2_ref, s2_ref, tc_ref, ts_ref, cs_ref, o_ref, z_ref):
    n1, c2, _ = x_ref.shape
    step = pl.program_id(1)

    @pl.when(step < steps)
    def _():
        cols = pl.ds(pl.multiple_of(step * c2, c2), c2)
        _stage1_body(x_ref, w_ref, z_ref.at[:, cols, :])

    @pl.when(step >= steps)
    def _():
        _stage2_body(n1, step - steps, z_ref, c2_ref, s2_ref, tc_ref, ts_ref, cs_ref, o_ref)


def _dft(u4, w_bd, tables):
    batch, n1, n2, width = u4.shape
    rb = ROW_BLOCKS_PER_STEP
    steps = n1 // rb
    c2 = n2 // steps
    assert n1 % rb == 0 and n2 == SEQ_MINOR and c2 % BF16_ROWS == 0
    return pl.pallas_call(
        functools.partial(_dft_kernel, steps),
        grid=(batch, 2 * steps),
        in_specs=[
            pl.BlockSpec((None, n1, c2, width),
                         lambda b, s: (b, 0, jnp.minimum(s, steps - 1), 0)),
            _const_spec((SEQ_MINOR, SEQ_MINOR)),
        ] + _stage2_table_specs(n1),
        out_specs=pl.BlockSpec((None, rb, n2, width),
                               lambda b, s: (b, jnp.maximum(s - steps, 0), 0, 0)),
        out_shape=jax.ShapeDtypeStruct((batch, n1, n2, width), BF16),
        scratch_shapes=[pltpu.VMEM((n1, n2, width), BF16)],
        compiler_params=pltpu.CompilerParams(
            dimension_semantics=("arbitrary", "arbitrary"),
            vmem_limit_bytes=VMEM_LIMIT_BYTES),
        name="dft",
    )(u4, w_bd, *tables)


@functools.lru_cache(maxsize=None)
def _dft_tables(seq_len):
    n2 = SEQ_MINOR
    n1 = seq_len // n2
    reps = n2 // n1
    k = np.arange(n1, dtype=np.float64)
    ang1 = 2.0 * np.pi * np.outer(k, k) / n1
    half = n1 // 2
    w1 = np.concatenate([np.cos(ang1)[:half + 1], -np.sin(ang1)[1:half]], axis=0)
    w_bd = np.kron(np.eye(reps), w1)
    m = np.arange(n2, dtype=np.float64)
    ang2 = 2.0 * np.pi * np.outer(m, m) / n2
    angt = 2.0 * np.pi * np.outer(k, m) / seq_len
    as32 = lambda a: np.asarray(a, dtype=np.float32)
    return (as32(w_bd), as32(np.cos(ang2)), as32(np.sin(ang2)),
            as32(np.cos(angt)), as32(np.sin(angt)))


@functools.lru_cache(maxsize=None)
def _channel_dft_table(seq_len):
    c = np.arange(GROUP_DIM, dtype=np.float64)
    ang = 2.0 * np.pi * np.outer(c, c) / GROUP_DIM
    norm = 1.0 / np.sqrt(float(seq_len) * GROUP_DIM)
    cs = np.concatenate([np.cos(ang), np.sin(ang)], axis=0) * norm
    return np.asarray(np.kron(np.eye(2), cs), dtype=np.float32)


def _stage2_tables(seq_len):
    _, c2, s2, tc, ts = _dft_tables(seq_len)
    return (jnp.asarray(c2), jnp.asarray(s2), jnp.asarray(tc), jnp.asarray(ts),
            jnp.asarray(_channel_dft_table(seq_len)).astype(BF16))


def _mix_out_kernel(x_ref, y_ref, gf_ref, p_ref, mod_ref, gpm_ref, gpf_ref, gqf_ref,
                    wfo_ref, wout_ref, wgate_ref, wup_ref, wdown_ref, o_ref):
    mod = mod_ref[0]
    n_sub = MIX_OUT_TILE // MIX_OUT_SUB
    gate1 = mod[2:3, :] * gpm_ref[...]
    scale2 = gpf_ref[...] * (1.0 + mod[4:5, :])
    shift2 = mod[3:4, :]
    gate2 = mod[5:6, :] * gqf_ref[...]
    k2_sub = MIX_OUT_SUB // K1_PER_TILE

    def rows_of(ref, k2s):
        return ref[k2s].reshape(MIX_OUT_SUB, ref.shape[-1])

    def up_proj(h2, chunk):
        cols = pl.ds(*chunk)
        return _dot(h2, wgate_ref[:, cols]), _dot(h2, wup_ref[:, cols])

    def down_proj(ab, chunk):
        a, b = ab
        return _dot((a * _sigmoid(a) * b).astype(BF16), wdown_ref[pl.ds(*chunk), :])

    def front(group):
        ys = [jnp.swapaxes(y_ref[:, k2s, :], 0, 1).reshape(MIX_OUT_SUB, BRANCH_WIDTH)
              for k2s in group]
        y_fs = [_dot(y, wfo_ref[...]) for y in ys]
        ms = [_dot((rows_of(gf_ref, k2s).astype(F32) * y_f
                    + rows_of(p_ref, k2s).astype(F32)).astype(BF16), wout_ref[...])
              for k2s, y_f in zip(group, y_fs)]
        x1s = [rows_of(x_ref, k2s) + gate1 * (m * _rms_scale(m))
               for k2s, m in zip(group, ms)]
        h2s = [(x1 * _rms_scale(x1) * scale2 + shift2).astype(BF16) for x1 in x1s]
        return x1s, h2s

    def ffn(h2s):
        zs = [None] * len(h2s)
        pending = [up_proj(h2, FF_CHUNKS[0]) for h2 in h2s]
        for c, chunk in enumerate(FF_CHUNKS):
            last = c + 1 == len(FF_CHUNKS)
            nxt = None if last else [up_proj(h2, FF_CHUNKS[c + 1]) for h2 in h2s]
            for s in range(len(h2s)):
                part = down_proj(pending[s], chunk)
                zs[s] = part if zs[s] is None else zs[s] + part
            pending = nxt
        return zs

    def run_group(g):
        base = pl.multiple_of(g * (SUBS_PER_GROUP * k2_sub), SUBS_PER_GROUP * k2_sub)
        group = [pl.ds(base + s * k2_sub, k2_sub) for s in range(SUBS_PER_GROUP)]
        x1s, h2s = front(group)
        for k2s, x1, z in zip(group, x1s, ffn(h2s)):
            out = x1 + gate2 * (z * _rms_scale(z))
            o_ref[k2s] = out.reshape(k2_sub, K1_PER_TILE, D_MODEL)

    n_groups = n_sub // SUBS_PER_GROUP
    if n_groups == 1:
        run_group(0)
    else:
        lax.fori_loop(0, n_groups, lambda g, carry: (run_group(g), carry)[1], 0)


def _mix_out(x, y4, g_f, p, mods, g_post_mix, g_pre_ffn, g_post_ffn,
             w_fo, w_out, w_gate, w_up, w_down):
    batch, seq_len, d = x.shape
    n1 = y4.shape[1]
    n2 = SEQ_MINOR
    k1t, k2t = K1_PER_TILE, K2_PER_TILE
    assert n1 % k1t == 0 and n2 % k2t == 0 and k1t * k2t == MIX_OUT_TILE
    by_k = lambda a: a.reshape(batch, n2, n1 // k1t, k1t, a.shape[-1])
    tok_spec = pl.BlockSpec((None, k2t, None, k1t, d), lambda b, a, t: (b, a, t, 0, 0))
    return pl.pallas_call(
        _mix_out_kernel,
        grid=(batch, n2 // k2t, n1 // k1t),
        in_specs=[
            tok_spec,
            pl.BlockSpec((None, k1t, k2t, BRANCH_WIDTH), lambda b, a, t: (b, t, a, 0)),
            tok_spec,
            tok_spec,
            pl.BlockSpec((1, N_ADA, D_MODEL), lambda b, a, t: (b, 0, 0)),
            _const_spec((1, D_MODEL)),
            _const_spec((1, D_MODEL)),
            _const_spec((1, D_MODEL)),
            _const_spec((BRANCH_WIDTH, D_MODEL)),
            _const_spec((D_MODEL, D_MODEL)),
            _const_spec((D_MODEL, D_FF)),
            _const_spec((D_MODEL, D_FF)),
            _const_spec((D_FF, D_MODEL)),
        ],
        out_specs=tok_spec,
        out_shape=jax.ShapeDtypeStruct((batch, n2, n1 // k1t, k1t, d), F32),
        compiler_params=pltpu.CompilerParams(
            dimension_semantics=("arbitrary", "arbitrary", "arbitrary"),
            vmem_limit_bytes=VMEM_LIMIT_BYTES),
        name="mix_out",
    )(by_k(x), y4, by_k(g_f), by_k(p), mods, g_post_mix, g_pre_ffn, g_post_ffn,
      w_fo, w_out, w_gate, w_up, w_down).reshape(batch, seq_len, d)


def _layer(x, mods, early, gains, late):
    batch, seq_len, d = x.shape
    to_cast = late if late[0].dtype != BF16 else ()
    u_f, g_f, p, *cast = _mix_in(x.reshape(batch * seq_len, d), mods, seq_len, *early,
                                 cast_weights=to_cast)
    late = tuple(cast) if to_cast else late
    y4 = _dft(u_f.reshape(batch, seq_len // SEQ_MINOR, SEQ_MINOR, BRANCH_WIDTH),
              jnp.asarray(_dft_tables(seq_len)[0]).astype(BF16), _stage2_tables(seq_len))
    return _mix_out(x, y4, g_f, p, mods, *gains, *late), late


def kernel(x_prompt, x_sample, c_prompt, c_sample, w_ada, b_ada, g_pre_mix, w_in, w_fo, w_pg,
           pool_scale, w_po, w_out, g_post_mix, g_pre_ffn, w_gate, w_up, w_down, g_post_ffn):
    depth = w_ada.shape[0]
    y_prompt, y_sample = x_prompt, x_sample
    bp, bs = c_prompt.shape[0], c_sample.shape[0]
    pad_rows = -(bp + bs) % BF16_ROWS
    c_pad = jnp.concatenate(
        [c_prompt, c_sample, jnp.zeros((pad_rows, D_MODEL), F32)], axis=0)
    for l in range(depth):
        ada, w_in_bf16, w_pool = _prep(c_pad, w_ada[l], b_ada[l][None, :], w_in[l], w_pg[l],
                                       pool_scale[l][None, :], w_po[l])
        mods_p = ada[:bp].reshape(bp, N_ADA, D_MODEL)
        mods_s = ada[bp:bp + bs].reshape(bs, N_ADA, D_MODEL)
        early = (g_pre_mix[l][None, :], w_in_bf16, w_pool)
        gains = (g_post_mix[l][None, :], g_pre_ffn[l][None, :], g_post_ffn[l][None, :])
        late = (w_fo[l], w_out[l], w_gate[l], w_up[l], w_down[l])
        y_prompt, late = _layer(y_prompt, mods_p, early, gains, late)
        y_sample, _ = _layer(y_sample, mods_s, early, gains, late)
    return (y_prompt, y_sample)
```

```python
import functools

import numpy as np
import jax
import jax.numpy as jnp
from jax import lax
from jax.experimental import pallas as pl
from jax.experimental.pallas import tpu as pltpu

D_MODEL = 1024
N_GROUPS = 4
GROUP_DIM = 128
BRANCH_WIDTH = N_GROUPS * GROUP_DIM
POOL_WINDOWS = (2, 4, 8, 16)
D_FF = 2816
N_ADA = 6
EPS = 1e-6

F32_ROWS = 8
BF16_ROWS = 16
MXU_DIM = 256
VMEM_LIMIT_BYTES = 60 * 1024 * 1024

SEQ_MINOR = 128
HALO = BF16_ROWS
SUB_TILE = 256
MIX_IN_TILE = 1024
MIX_OUT_TILE = 1024
MIX_OUT_SUB = 256
SUBS_PER_GROUP = 2
K1_PER_TILE = BF16_ROWS
K2_PER_TILE = MIX_OUT_TILE // K1_PER_TILE
ROW_BLOCKS_PER_STEP = 16
PREP_ROWS = 128
FF_CHUNKS = ((0, 1024), (1024, 1024), (2048, 768))

BF16 = jnp.bfloat16
F32 = jnp.float32


def _dot(a, b):
    return jnp.dot(a, b, preferred_element_type=F32)


def _sigmoid(v):
    return 1.0 / (1.0 + jnp.exp(-v))


def _rms_scale(v):
    return lax.rsqrt(jnp.mean(v * v, axis=-1, keepdims=True) + EPS)


def _prep_kernel(c_ref, wada_ref, b_ref, win_ref, wpg_ref, ps_ref, wpo_ref,
                 ada_ref, winb_ref, wpool_ref):
    k = pl.program_id(0)

    @pl.when(k == 0)
    def _():
        ada_ref[...] = jnp.broadcast_to(b_ref[...], ada_ref.shape)
        for g in range(N_GROUPS):
            rows = pl.ds(g * GROUP_DIM, GROUP_DIM)
            scaled = wpg_ref[g] * ps_ref[:, rows]
            wpool_ref[rows, :] = jnp.dot(
                scaled, wpo_ref[rows, :], preferred_element_type=F32,
                precision=lax.Precision.HIGHEST).astype(wpool_ref.dtype)

    c = c_ref[:, pl.ds(pl.multiple_of(k * PREP_ROWS, PREP_ROWS), PREP_ROWS)]
    s = (c * _sigmoid(c)).astype(BF16)
    ada_ref[...] += _dot(s, wada_ref[...].astype(BF16))
    winb_ref[...] = win_ref[...].astype(winb_ref.dtype)


def _prep(c_pad, w_ada, b_ada, w_in, w_pg, pool_scale, w_po):
    rows, d = c_pad.shape
    n_ada, n_in = w_ada.shape[1], w_in.shape[1]
    block = lambda cols: pl.BlockSpec((PREP_ROWS, cols), lambda k: (k, 0))
    whole = lambda a: pl.BlockSpec(a.shape, lambda k: (0,) * a.ndim)
    return pl.pallas_call(
        _prep_kernel,
        grid=(d // PREP_ROWS,),
        in_specs=[whole(c_pad), block(n_ada), whole(b_ada), block(n_in),
                  whole(w_pg), whole(pool_scale), whole(w_po)],
        out_specs=[pl.BlockSpec((rows, n_ada), lambda k: (0, 0)), block(n_in),
                   pl.BlockSpec(w_po.shape, lambda k: (0, 0))],
        out_shape=[jax.ShapeDtypeStruct((rows, n_ada), F32),
                   jax.ShapeDtypeStruct(w_in.shape, BF16),
                   jax.ShapeDtypeStruct(w_po.shape, BF16)],
        compiler_params=pltpu.CompilerParams(
            dimension_semantics=("arbitrary",), vmem_limit_bytes=VMEM_LIMIT_BYTES),
        name="prep",
    )(c_pad, w_ada, b_ada, w_in, w_pg, pool_scale, w_po)


def _window_sum(ext, w):
    n = ext.shape[0]
    s = ext
    k = 1
    while k < w // 2:
        s = s + pltpu.roll(s, n - k, axis=0)
        k *= 2
    return s + pltpu.roll(s, w // 2, axis=0)


def _ada_vectors(ada_ref, row):
    mod = ada_ref[pl.ds(row, 1), :]
    return [mod[:, k * D_MODEL:(k + 1) * D_MODEL] for k in range(N_ADA)]


def _mix_in_kernel(tiles_per_seq, seq_len, n_cast, has_dft_job, first_row,
                   x_ref, xp_ref, xn_ref, ada_ref, g_ref, win_ref, wpool_ref, *rest):
    wf_ref = win_ref.at[:, pl.ds(0, BRANCH_WIDTH)]
    wp_ref = win_ref.at[:, pl.ds(BRANCH_WIDTH, BRANCH_WIDTH)]
    wg_ref = win_ref.at[:, pl.ds(2 * BRANCH_WIDTH, 2 * D_MODEL)]
    n_extra_in = n_cast + 2 * has_dft_job
    cast_in, dft_in = rest[:n_cast], rest[n_cast:n_extra_in]
    uf_ref, gf_ref, p_ref = rest[n_extra_in:n_extra_in + 3]
    cast_out = rest[n_extra_in + 3:n_extra_in + 3 + n_cast]
    dft = _Stage1Body(*dft_in, rest[-1]) if has_dft_job else None
    if dft:
        dft.load()
    for src, dst in zip(cast_in, cast_out):
        dst[...] = src[...].astype(dst.dtype)
    t = MIX_IN_TILE
    n_sub = t // SUB_TILE
    ti = lax.rem(pl.program_id(0), tiles_per_seq)
    shift, scale1 = _ada_vectors(ada_ref, first_row + pl.program_id(0) // tiles_per_seq)[:2]
    scale = g_ref[...] * (1.0 + scale1)

    def normed(v):
        return v * _rms_scale(v) * scale + shift

    ts = SUB_TILE
    subs = [pl.ds(s * ts, ts) for s in range(n_sub)]
    hs = [normed(x_ref[rows, :]).astype(BF16) for rows in subs]
    h_prev = jnp.where(ti == 0, 0.0, normed(xp_ref[...])).astype(BF16)
    h_next = jnp.where(ti == tiles_per_seq - 1, 0.0, normed(xn_ref[...])).astype(BF16)
    afters = [h[:HALO] for h in hs[1:]] + [h_next]

    slab = F32_ROWS
    row = lax.broadcasted_iota(jnp.int32, (slab, GROUP_DIM), 0)

    def pool_diff(up, first_pos):
        diffs = []
        for g, w in enumerate(POOL_WINDOWS):
            assert w & (w - 1) == 0 and w // 2 <= slab
            left = w // 2
            right = w - 1 - left
            ext = up[:, g * GROUP_DIM:(g + 1) * GROUP_DIM]
            win = _window_sum(ext, w)[HALO:HALO + ts]
            u = ext[HALO:HALO + ts]

            def count(pos):
                return (jnp.minimum(pos + right, seq_len - 1)
                        - jnp.maximum(pos - left, 0) + 1).astype(F32)

            diffs.append(jnp.concatenate([
                win[:slab] / count(first_pos + row) - u[:slab],
                win[slab:ts - slab] * (1.0 / w) - u[slab:ts - slab],
                win[ts - slab:] / count(first_pos + (ts - slab) + row) - u[ts - slab:],
            ], axis=0))
        return diffs

    def project(s):
        head = [h_prev, hs[0][:HALO]] if s == 0 else []
        up = _dot(jnp.concatenate(head + [hs[s][HALO:], afters[s]], axis=0), wp_ref[...])
        uf_ref[subs[s], :] = _dot(hs[s], wf_ref[...]).astype(uf_ref.dtype)
        return up, _dot(hs[s], wg_ref[...])

    def pool_rows(s):
        if s == 0:
            return proj[0][0]
        return jnp.concatenate([proj[s - 1][0][-2 * HALO:], proj[s][0]], axis=0)

    def finish(s, up, gate_logits):
        d = jnp.concatenate(pool_diff(up, ti * t + s * ts), axis=1).astype(BF16)
        g = _sigmoid(gate_logits)
        gf_ref[subs[s], :] = g[:, :D_MODEL].astype(gf_ref.dtype)
        p_ref[subs[s], :] = (g[:, D_MODEL:] * _dot(d, wpool_ref[...])).astype(p_ref.dtype)

    proj = {}
    for step in range(n_sub + 1):
        if step < n_sub:
            proj[step] = project(step)
        if dft and step == 0:
            dft.matmul()
            dft.store()
        if step >= 1:
            finish(step - 1, pool_rows(step - 1), proj[step - 1][1])


def _const_spec(shape):
    zeros = (0,) * len(shape)
    return pl.BlockSpec(shape, lambda *_: zeros, pipeline_mode=pl.Buffered(1))


def _cast_specs(weights, n_steps):
    specs, shapes = [], []
    for w in weights:
        rows, cols = w.shape
        n_blocks = n_steps
        while rows % n_blocks or (rows // n_blocks) % BF16_ROWS:
            n_blocks //= 2
        assert n_blocks >= 1
        spec = pl.BlockSpec((rows // n_blocks, cols),
                            lambda i, last=n_blocks - 1: (jnp.minimum(i, last), 0))
        specs.append(spec)
        shapes.append(jax.ShapeDtypeStruct(w.shape, BF16))
    return specs, shapes


def _mix_in(x2d, mods, seq_len, g_pre, w_in, w_pool, cast_weights=(), dft_job=None):
    ada, first_row = mods
    n_tok = x2d.shape[0]
    t = MIX_IN_TILE
    tiles_per_seq = seq_len // t
    halo_blocks_per_tile = t // HALO
    n_halo_blocks = n_tok // HALO
    row = lambda i: (i, 0)
    cast_specs, cast_shapes = _cast_specs(cast_weights, n_tok // t)
    job = dft_job(n_tok // t, lambda i: i) if dft_job else None
    job_in = job.in_specs if job else []
    job_out = [job.out_spec] if job else []
    job_shape = [job.out_shape] if job else []
    job_args = job.args if job else ()
    return pl.pallas_call(
        functools.partial(_mix_in_kernel, tiles_per_seq, seq_len, len(cast_weights),
                          job is not None, first_row),
        grid=(n_tok // t,),
        in_specs=[
            pl.BlockSpec((t, D_MODEL), row),
            pl.BlockSpec((HALO, D_MODEL),
                         lambda i: (jnp.maximum(i * halo_blocks_per_tile - 1, 0), 0)),
            pl.BlockSpec((HALO, D_MODEL),
                         lambda i: (jnp.minimum((i + 1) * halo_blocks_per_tile,
                                                n_halo_blocks - 1), 0)),
            _const_spec(ada.shape),
            _const_spec((1, D_MODEL)),
            _const_spec((D_MODEL, 2 * BRANCH_WIDTH + 2 * D_MODEL)),
            _const_spec((BRANCH_WIDTH, D_MODEL)),
        ] + cast_specs + job_in,
        out_specs=[
            pl.BlockSpec((t, BRANCH_WIDTH), row),
            pl.BlockSpec((t, D_MODEL), row),
            pl.BlockSpec((t, D_MODEL), row),
        ] + cast_specs + job_out,
        out_shape=[
            jax.ShapeDtypeStruct((n_tok, BRANCH_WIDTH), BF16),
            jax.ShapeDtypeStruct((n_tok, D_MODEL), BF16),
            jax.ShapeDtypeStruct((n_tok, D_MODEL), BF16),
        ] + cast_shapes + job_shape,
        compiler_params=pltpu.CompilerParams(
            dimension_semantics=("arbitrary",), vmem_limit_bytes=VMEM_LIMIT_BYTES),
        name="mix_in",
    )(x2d, x2d, x2d, ada, g_pre, w_in, w_pool, *cast_weights, *job_args)


class _Stage1Body:
    def __init__(self, x_ref, w_ref, o_ref):
        self.x_ref, self.w_ref, self.o_ref = x_ref, w_ref, o_ref

    def load(self):
        n1, c2, width = self.x_ref.shape
        self.xt = jnp.swapaxes(self.x_ref[...], 0, 1).reshape(
            n1 * c2 // SEQ_MINOR, SEQ_MINOR, width)

    def matmul(self):
        w = self.w_ref[...]
        self.z = jnp.stack([_dot(w, self.xt[g]).astype(BF16)
                            for g in range(self.xt.shape[0])])

    def store(self):
        n1, c2, width = self.x_ref.shape
        self.o_ref[...] = jnp.swapaxes(self.z.reshape(c2, n1, width), 0, 1)

    def run(self):
        self.load()
        self.matmul()
        self.store()


class _Stage1Job:
    def __init__(self, u4, w_bd, n_steps, lin):
        batch, n1, n2, width = u4.shape
        assert n_steps % batch == 0
        chunks = n_steps // batch
        c2 = n2 // chunks
        assert n2 % chunks == 0 and c2 % BF16_ROWS == 0 and (c2 * n1) % SEQ_MINOR == 0
        index = lambda *g: (lin(*g) // chunks, 0, lin(*g) % chunks, 0)
        self.args = (u4, w_bd)
        self.in_specs = [pl.BlockSpec((None, n1, c2, width), index),
                         _const_spec((SEQ_MINOR, SEQ_MINOR))]
        self.out_spec = pl.BlockSpec((None, n1, c2, width), index)
        self.out_shape = jax.ShapeDtypeStruct((batch, n1, n2, width), BF16)


def _packed_windows(n1, step):
    rb, half = ROW_BLOCKS_PER_STEP, n1 // 2
    upper = rb * step >= max(half, rb)
    re_start = jnp.where(upper, n1 - rb * step - (rb - 1), rb * step)
    im_start = jnp.minimum(jnp.where(upper, re_start + half, half + rb * step), n1 - rb)
    return re_start, im_start


def _stage2_body(n1, step, re_row, im_row, c2_ref, s2_ref, tc_ref, ts_ref, cs_ref, o_ref):
    c2 = c2_ref[...]
    s2 = s2_ref[...]
    half = n1 // 2
    qs = []
    for j in range(ROW_BLOCKS_PER_STEP):
        k1 = step * ROW_BLOCKS_PER_STEP + j
        mirrored = k1 > half
        kp = jnp.where(mirrored, n1 - k1, k1)
        real_only = (kp == 0) | (kp == half)
        sign = jnp.where(real_only, 0.0, jnp.where(mirrored, -1.0, 1.0))
        tc = tc_ref[pl.ds(k1, 1), :]
        ts = ts_ref[pl.ds(k1, 1), :]
        cos_t = c2 * tc - s2 * ts
        sin_t = s2 * tc + c2 * ts
        lhs = jnp.concatenate(
            [jnp.concatenate([cos_t, sign * sin_t], axis=1),
             jnp.concatenate([-sin_t, sign * cos_t], axis=1)], axis=0).astype(BF16)
        z = jnp.concatenate(
            [re_row(kp), im_row(jnp.where(real_only, half + 1, half + kp))], axis=0)
        qs.append(_dot(lhs, z).astype(BF16))
    halves = []
    for pair in range(N_GROUPS // 2):
        rows = []
        for q in qs:
            parts = []
            for g in (2 * pair, 2 * pair + 1):
                lanes = slice(g * GROUP_DIM, (g + 1) * GROUP_DIM)
                parts += [q[:SEQ_MINOR, lanes], q[SEQ_MINOR:, lanes]]
            rows.append(jnp.concatenate(parts, axis=1))
        halves.append(_dot(jnp.concatenate(rows, axis=0), cs_ref[...]))
    y = jnp.concatenate(halves, axis=1).astype(o_ref.dtype)
    o_ref[...] = y.reshape(o_ref.shape)


def _dft_stage2_kernel(n1, zr_ref, zi_ref, *table_and_out_refs):
    step = pl.program_id(1)
    re_start, im_start = _packed_windows(n1, step)
    _stage2_body(n1, step, lambda r: zr_ref[0, r - re_start],
                 lambda r: zi_ref[0, jnp.maximum(r - im_start, 0)], *table_and_out_refs)


def _dft_stage2(z4, tables):
    batch, n1, n2, width = z4.shape
    rb = ROW_BLOCKS_PER_STEP
    assert n1 % rb == 0 and n2 == SEQ_MINOR
    window = tuple(pl.Element(n) for n in (1, rb, n2, width))
    return pl.pallas_call(
        functools.partial(_dft_stage2_kernel, n1),
        grid=(batch, n1 // rb),
        in_specs=[
            pl.BlockSpec(window, lambda b, i: (b, _packed_windows(n1, i)[0], 0, 0)),
            pl.BlockSpec(window, lambda b, i: (b, _packed_windows(n1, i)[1], 0, 0)),
        ] + _stage2_table_specs(n1),
        out_specs=pl.BlockSpec((None, rb, n2, width), lambda b, i: (b, i, 0, 0)),
        out_shape=jax.ShapeDtypeStruct((batch, n1, n2, width), BF16),
        compiler_params=pltpu.CompilerParams(
            dimension_semantics=("arbitrary", "arbitrary"),
            vmem_limit_bytes=VMEM_LIMIT_BYTES),
        name="dft_stage2",
    )(z4, z4, *tables)


def _stage2_table_specs(n1):
    return [_const_spec((SEQ_MINOR, SEQ_MINOR)), _const_spec((SEQ_MINOR, SEQ_MINOR)),
            _const_spec((n1, SEQ_MINOR)), _const_spec((n1, SEQ_MINOR)),
            _const_spec((2 * MXU_DIM, MXU_DIM))]


def _dft_fused_kernel(x_ref, w_ref, c2_ref, s2_ref, tc_ref, ts_ref, cs_ref, o_ref, z_ref):
    _Stage1Body(x_ref, w_ref, z_ref).run()
    packed_row = lambda r: z_ref[r]
    _stage2_body(x_ref.shape[0], 0, packed_row, packed_row,
                 c2_ref, s2_ref, tc_ref, ts_ref, cs_ref, o_ref)


def _dft_fused(u4, w_bd, tables):
    batch, n1, n2, width = u4.shape
    seq_block = pl.BlockSpec((None, n1, n2, width), lambda b: (b, 0, 0, 0))
    return pl.pallas_call(
        _dft_fused_kernel,
        grid=(batch,),
        in_specs=[seq_block, _const_spec((SEQ_MINOR, SEQ_MINOR))] + _stage2_table_specs(n1),
        out_specs=seq_block,
        out_shape=jax.ShapeDtypeStruct((batch, n1, n2, width), BF16),
        scratch_shapes=[pltpu.VMEM((n1, n2, width), BF16)],
        compiler_params=pltpu.CompilerParams(
            dimension_semantics=("arbitrary",), vmem_limit_bytes=VMEM_LIMIT_BYTES),
        name="dft_fused",
    )(u4, w_bd, *tables)


@functools.lru_cache(maxsize=None)
def _dft_tables(seq_len):
    n2 = SEQ_MINOR
    n1 = seq_len // n2
    reps = n2 // n1
    k = np.arange(n1, dtype=np.float64)
    ang1 = 2.0 * np.pi * np.outer(k, k) / n1
    half = n1 // 2
    w1 = np.concatenate([np.cos(ang1)[:half + 1], -np.sin(ang1)[1:half]], axis=0)
    w_bd = np.kron(np.eye(reps), w1)
    m = np.arange(n2, dtype=np.float64)
    ang2 = 2.0 * np.pi * np.outer(m, m) / n2
    angt = 2.0 * np.pi * np.outer(k, m) / seq_len
    as32 = lambda a: np.asarray(a, dtype=np.float32)
    return (as32(w_bd), as32(np.cos(ang2)), as32(np.sin(ang2)),
            as32(np.cos(angt)), as32(np.sin(angt)))


@functools.lru_cache(maxsize=None)
def _channel_dft_table(seq_len):
    c = np.arange(GROUP_DIM, dtype=np.float64)
    ang = 2.0 * np.pi * np.outer(c, c) / GROUP_DIM
    norm = 1.0 / np.sqrt(float(seq_len) * GROUP_DIM)
    cs = np.concatenate([np.cos(ang), np.sin(ang)], axis=0) * norm
    return np.asarray(np.kron(np.eye(2), cs), dtype=np.float32)


def _stage1_weights(seq_len):
    return jnp.asarray(_dft_tables(seq_len)[0]).astype(BF16)


def _stage2_tables(seq_len):
    _, c2, s2, tc, ts = _dft_tables(seq_len)
    return (jnp.asarray(c2), jnp.asarray(s2), jnp.asarray(tc), jnp.asarray(ts),
            jnp.asarray(_channel_dft_table(seq_len)).astype(BF16))


def _as_blocks(u_f, batch, seq_len):
    return u_f.reshape(batch, seq_len // SEQ_MINOR, SEQ_MINOR, BRANCH_WIDTH)


def _mix_out_kernel(first_row, x_ref, y_ref, gf_ref, p_ref, ada_ref, gpm_ref, gpf_ref, gqf_ref,
                    wfo_ref, wout_ref, wgate_ref, wup_ref, wdown_ref, o_ref):
    _, _, gate1, shift2, scale2, gate2 = _ada_vectors(ada_ref, first_row + pl.program_id(0))
    n_sub = MIX_OUT_TILE // MIX_OUT_SUB
    gate1 = gate1 * gpm_ref[...]
    scale2 = gpf_ref[...] * (1.0 + scale2)
    gate2 = gate2 * gqf_ref[...]
    k2_sub = MIX_OUT_SUB // K1_PER_TILE

    def rows_of(ref, k2s):
        return ref[k2s].reshape(MIX_OUT_SUB, ref.shape[-1])

    def up_proj(h2, chunk):
        cols = pl.ds(*chunk)
        return _dot(h2, wgate_ref[:, cols]), _dot(h2, wup_ref[:, cols])

    def down_proj(ab, chunk):
        a, b = ab
        return _dot((a * _sigmoid(a) * b).astype(BF16), wdown_ref[pl.ds(*chunk), :])

    def front(group):
        ys = [jnp.swapaxes(y_ref[:, k2s, :], 0, 1).reshape(MIX_OUT_SUB, BRANCH_WIDTH)
              for k2s in group]
        y_fs = [_dot(y, wfo_ref[...]) for y in ys]
        ms = [_dot((rows_of(gf_ref, k2s).astype(F32) * y_f
                    + rows_of(p_ref, k2s).astype(F32)).astype(BF16), wout_ref[...])
              for k2s, y_f in zip(group, y_fs)]
        x1s = [rows_of(x_ref, k2s) + gate1 * (m * _rms_scale(m))
               for k2s, m in zip(group, ms)]
        h2s = [(x1 * _rms_scale(x1) * scale2 + shift2).astype(BF16) for x1 in x1s]
        return x1s, h2s

    def ffn(h2s):
        zs = [None] * len(h2s)
        pending = [up_proj(h2, FF_CHUNKS[0]) for h2 in h2s]
        for c, chunk in enumerate(FF_CHUNKS):
            last = c + 1 == len(FF_CHUNKS)
            nxt = None if last else [up_proj(h2, FF_CHUNKS[c + 1]) for h2 in h2s]
            for s in range(len(h2s)):
                part = down_proj(pending[s], chunk)
                zs[s] = part if zs[s] is None else zs[s] + part
            pending = nxt
        return zs

    def run_group(g):
        base = pl.multiple_of(g * (SUBS_PER_GROUP * k2_sub), SUBS_PER_GROUP * k2_sub)
        group = [pl.ds(base + s * k2_sub, k2_sub) for s in range(SUBS_PER_GROUP)]
        x1s, h2s = front(group)
        for k2s, x1, z in zip(group, x1s, ffn(h2s)):
            out = x1 + gate2 * (z * _rms_scale(z))
            o_ref[k2s] = out.reshape(k2_sub, K1_PER_TILE, D_MODEL)

    n_groups = n_sub // SUBS_PER_GROUP
    if n_groups == 1:
        run_group(0)
    else:
        lax.fori_loop(0, n_groups, lambda g, carry: (run_group(g), carry)[1], 0)


def _mix_out(x, y4, g_f, p, mods, g_post_mix, g_pre_ffn, g_post_ffn,
             w_fo, w_out, w_gate, w_up, w_down):
    batch, seq_len, d = x.shape
    n1 = y4.shape[1]
    n2 = SEQ_MINOR
    k1t, k2t = K1_PER_TILE, K2_PER_TILE
    assert n1 % k1t == 0 and n2 % k2t == 0 and k1t * k2t == MIX_OUT_TILE
    by_k = lambda a: a.reshape(batch, n2, n1 // k1t, k1t, a.shape[-1])
    tok_spec = pl.BlockSpec((None, k2t, None, k1t, d), lambda b, a, t: (b, a, t, 0, 0))
    ada, first_row = mods
    return pl.pallas_call(
        functools.partial(_mix_out_kernel, first_row),
        grid=(batch, n2 // k2t, n1 // k1t),
        in_specs=[
            tok_spec,
            pl.BlockSpec((None, k1t, k2t, BRANCH_WIDTH), lambda b, a, t: (b, t, a, 0)),
            tok_spec,
            tok_spec,
            _const_spec(ada.shape),
            _const_spec((1, D_MODEL)),
            _const_spec((1, D_MODEL)),
            _const_spec((1, D_MODEL)),
            _const_spec((BRANCH_WIDTH, D_MODEL)),
            _const_spec((D_MODEL, D_MODEL)),
            _const_spec((D_MODEL, D_FF)),
            _const_spec((D_MODEL, D_FF)),
            _const_spec((D_FF, D_MODEL)),
        ],
        out_specs=tok_spec,
        out_shape=jax.ShapeDtypeStruct((batch, n2, n1 // k1t, k1t, d), F32),
        compiler_params=pltpu.CompilerParams(
            dimension_semantics=("arbitrary", "arbitrary", "arbitrary"),
            vmem_limit_bytes=VMEM_LIMIT_BYTES),
        name="mix_out",
    )(by_k(x), y4, by_k(g_f), by_k(p), ada, g_post_mix, g_pre_ffn, g_post_ffn,
      w_fo, w_out, w_gate, w_up, w_down).reshape(batch, seq_len, d)


def _layer_pair(x_a, mods_a, x_b, mods_b, early, gains, late):
    (ba, sa, d), (bb, sb, _) = x_a.shape, x_b.shape
    assert sb == ROW_BLOCKS_PER_STEP * SEQ_MINOR
    u_a, gf_a, p_a, *late = _mix_in(x_a.reshape(ba * sa, d), mods_a, sa, *early,
                                    cast_weights=late)
    u4_a, w_a = _as_blocks(u_a, ba, sa), _stage1_weights(sa)
    u_b, gf_b, p_b, z_a = _mix_in(
        x_b.reshape(bb * sb, d), mods_b, sb, *early,
        dft_job=lambda n_steps, lin: _Stage1Job(u4_a, w_a, n_steps, lin))
    y_a = _dft_stage2(z_a, _stage2_tables(sa))
    out_a = _mix_out(x_a, y_a, gf_a, p_a, mods_a, *gains, *late)
    y_b = _dft_fused(_as_blocks(u_b, bb, sb), _stage1_weights(sb), _stage2_tables(sb))
    out_b = _mix_out(x_b, y_b, gf_b, p_b, mods_b, *gains, *late)
    return out_a, out_b


def kernel(x_prompt, x_sample, c_prompt, c_sample, w_ada, b_ada, g_pre_mix, w_in, w_fo, w_pg,
           pool_scale, w_po, w_out, g_post_mix, g_pre_ffn, w_gate, w_up, w_down, g_post_ffn):
    depth = w_ada.shape[0]
    y_prompt, y_sample = x_prompt, x_sample
    bp, bs = c_prompt.shape[0], c_sample.shape[0]
    pad_rows = -(bp + bs) % BF16_ROWS
    c_pad = jnp.concatenate(
        [c_prompt, c_sample, jnp.zeros((pad_rows, D_MODEL), F32)], axis=0)
    for l in range(depth):
        ada, w_in_bf16, w_pool = _prep(c_pad, w_ada[l], b_ada[l][None, :], w_in[l], w_pg[l],
                                       pool_scale[l][None, :], w_po[l])
        mods_p, mods_s = (ada, 0), (ada, bp)
        early = (g_pre_mix[l][None, :], w_in_bf16, w_pool)
        gains = (g_post_mix[l][None, :], g_pre_ffn[l][None, :], g_post_ffn[l][None, :])
        late = (w_fo[l], w_out[l], w_gate[l], w_up[l], w_down[l])
        y_prompt, y_sample = _layer_pair(y_prompt, mods_p, y_sample, mods_s, early, gains, late)
    return (y_prompt, y_sample)
```

```python
import functools

import numpy as np
import jax
import jax.numpy as jnp
from jax import lax
from jax.experimental import pallas as pl
from jax.experimental.pallas import tpu as pltpu

D_MODEL = 1024
N_GROUPS = 4
GROUP_DIM = 128
BRANCH_WIDTH = N_GROUPS * GROUP_DIM
POOL_WINDOWS = (2, 4, 8, 16)
D_FF = 2816
N_ADA = 6
EPS = 1e-6

F32_ROWS = 8
BF16_ROWS = 16
MXU_DIM = 256
VMEM_LIMIT_BYTES = 60 * 1024 * 1024

SEQ_MINOR = 128
HALO = BF16_ROWS
SUB_TILE = 256
MIX_IN_TILE = 1024
MIX_OUT_TILE = 1024
MIX_OUT_SUB = 256
SUBS_PER_GROUP = 2
K1_PER_TILE = BF16_ROWS
K2_PER_TILE = MIX_OUT_TILE // K1_PER_TILE
ROW_BLOCKS_PER_STEP = 16
PREP_ROWS = 128
FF_CHUNKS = ((0, 1024), (1024, 1024), (2048, 768))

BF16 = jnp.bfloat16
F32 = jnp.float32


def _dot(a, b):
    return jnp.dot(a, b, preferred_element_type=F32)


def _sigmoid(v):
    return 1.0 / (1.0 + jnp.exp(-v))


def _rms_scale(v):
    return lax.rsqrt(jnp.mean(v * v, axis=-1, keepdims=True) + EPS)


def _prep_kernel(c_ref, wada_ref, b_ref, win_ref, wpg_ref, ps_ref, wpo_ref,
                 ada_ref, winb_ref, wpool_ref):
    k = pl.program_id(0)

    @pl.when(k == 0)
    def _():
        ada_ref[...] = jnp.broadcast_to(b_ref[...], ada_ref.shape)
        for g in range(N_GROUPS):
            rows = pl.ds(g * GROUP_DIM, GROUP_DIM)
            scaled = wpg_ref[g] * ps_ref[:, rows]
            wpool_ref[rows, :] = jnp.dot(
                scaled, wpo_ref[rows, :], preferred_element_type=F32,
                precision=lax.Precision.HIGHEST).astype(wpool_ref.dtype)

    c = c_ref[:, pl.ds(pl.multiple_of(k * PREP_ROWS, PREP_ROWS), PREP_ROWS)]
    s = (c * _sigmoid(c)).astype(BF16)
    ada_ref[...] += _dot(s, wada_ref[...].astype(BF16))
    winb_ref[...] = win_ref[...].astype(winb_ref.dtype)


def _prep(c_pad, w_ada, b_ada, w_in, w_pg, pool_scale, w_po):
    rows, d = c_pad.shape
    n_ada, n_in = w_ada.shape[1], w_in.shape[1]
    block = lambda cols: pl.BlockSpec((PREP_ROWS, cols), lambda k: (k, 0))
    whole = lambda a: pl.BlockSpec(a.shape, lambda k: (0,) * a.ndim)
    return pl.pallas_call(
        _prep_kernel,
        grid=(d // PREP_ROWS,),
        in_specs=[whole(c_pad), block(n_ada), whole(b_ada), block(n_in),
                  whole(w_pg), whole(pool_scale), whole(w_po)],
        out_specs=[pl.BlockSpec((rows, n_ada), lambda k: (0, 0)), block(n_in),
                   pl.BlockSpec(w_po.shape, lambda k: (0, 0))],
        out_shape=[jax.ShapeDtypeStruct((rows, n_ada), F32),
                   jax.ShapeDtypeStruct(w_in.shape, BF16),
                   jax.ShapeDtypeStruct(w_po.shape, BF16)],
        compiler_params=pltpu.CompilerParams(
            dimension_semantics=("arbitrary",), vmem_limit_bytes=VMEM_LIMIT_BYTES),
        name="prep",
    )(c_pad, w_ada, b_ada, w_in, w_pg, pool_scale, w_po)


def _window_sum(ext, w):
    n = ext.shape[0]
    s = ext
    k = 1
    while k < w // 2:
        s = s + pltpu.roll(s, n - k, axis=0)
        k *= 2
    return s + pltpu.roll(s, w // 2, axis=0)


def _ada_vectors(ada_ref, row):
    mod = ada_ref[pl.ds(row, 1), :]
    return [mod[:, k * D_MODEL:(k + 1) * D_MODEL] for k in range(N_ADA)]


def _mix_in_kernel(tiles_per_seq, seq_len, n_cast, has_dft_job, first_row,
                   x_ref, xp_ref, xn_ref, ada_ref, g_ref, win_ref, wpool_ref, *rest):
    wf_ref = win_ref.at[:, pl.ds(0, BRANCH_WIDTH)]
    wp_ref = win_ref.at[:, pl.ds(BRANCH_WIDTH, BRANCH_WIDTH)]
    wg_ref = win_ref.at[:, pl.ds(2 * BRANCH_WIDTH, 2 * D_MODEL)]
    n_extra_in = n_cast + 2 * has_dft_job
    cast_in, dft_in = rest[:n_cast], rest[n_cast:n_extra_in]
    uf_ref, gf_ref, p_ref = rest[n_extra_in:n_extra_in + 3]
    cast_out = rest[n_extra_in + 3:n_extra_in + 3 + n_cast]
    dft = _Stage1Body(*dft_in, rest[-1]) if has_dft_job else None
    if dft:
        dft.load()
    for src, dst in zip(cast_in, cast_out):
        dst[...] = src[...].astype(dst.dtype)
    t = MIX_IN_TILE
    n_sub = t // SUB_TILE
    ti = lax.rem(pl.program_id(0), tiles_per_seq)
    shift, scale1 = _ada_vectors(ada_ref, first_row + pl.program_id(0) // tiles_per_seq)[:2]
    scale = g_ref[...] * (1.0 + scale1)

    def normed(v):
        return v * _rms_scale(v) * scale + shift

    ts = SUB_TILE
    subs = [pl.ds(s * ts, ts) for s in range(n_sub)]
    hs = [normed(x_ref[rows, :]).astype(BF16) for rows in subs]
    h_prev = jnp.where(ti == 0, 0.0, normed(xp_ref[...])).astype(BF16)
    h_next = jnp.where(ti == tiles_per_seq - 1, 0.0, normed(xn_ref[...])).astype(BF16)
    afters = [h[:HALO] for h in hs[1:]] + [h_next]

    slab = F32_ROWS
    row = lax.broadcasted_iota(jnp.int32, (slab, GROUP_DIM), 0)

    def pool_diff(up, first_pos):
        diffs = []
        for g, w in enumerate(POOL_WINDOWS):
            assert w & (w - 1) == 0 and w // 2 <= slab
            left = w // 2
            right = w - 1 - left
            ext = up[:, g * GROUP_DIM:(g + 1) * GROUP_DIM]
            win = _window_sum(ext, w)[HALO:HALO + ts]
            u = ext[HALO:HALO + ts]

            def count(pos):
                return (jnp.minimum(pos + right, seq_len - 1)
                        - jnp.maximum(pos - left, 0) + 1).astype(F32)

            diffs.append(jnp.concatenate([
                win[:slab] / count(first_pos + row) - u[:slab],
                win[slab:ts - slab] * (1.0 / w) - u[slab:ts - slab],
                win[ts - slab:] / count(first_pos + (ts - slab) + row) - u[ts - slab:],
            ], axis=0))
        return diffs

    def project(s):
        head = [h_prev, hs[0][:HALO]] if s == 0 else []
        up = _dot(jnp.concatenate(head + [hs[s][HALO:], afters[s]], axis=0), wp_ref[...])
        uf_ref[subs[s], :] = _dot(hs[s], wf_ref[...]).astype(uf_ref.dtype)
        return up, _dot(hs[s], wg_ref[...])

    def pool_rows(s):
        if s == 0:
            return proj[0][0]
        return jnp.concatenate([proj[s - 1][0][-2 * HALO:], proj[s][0]], axis=0)

    def finish(s, up, gate_logits):
        d = jnp.concatenate(pool_diff(up, ti * t + s * ts), axis=1).astype(BF16)
        g = _sigmoid(gate_logits)
        gf_ref[subs[s], :] = g[:, :D_MODEL].astype(gf_ref.dtype)
        p_ref[subs[s], :] = (g[:, D_MODEL:] * _dot(d, wpool_ref[...])).astype(p_ref.dtype)

    proj = {}
    for step in range(n_sub + 1):
        if step < n_sub:
            proj[step] = project(step)
        if dft and step == 0:
            dft.matmul()
            dft.store()
        if step >= 1:
            finish(step - 1, pool_rows(step - 1), proj[step - 1][1])


def _const_spec(shape):
    zeros = (0,) * len(shape)
    return pl.BlockSpec(shape, lambda *_: zeros, pipeline_mode=pl.Buffered(1))


def _cast_specs(weights, n_steps):
    specs, shapes = [], []
    for w in weights:
        rows, cols = w.shape
        n_blocks = n_steps
        while rows % n_blocks or (rows // n_blocks) % BF16_ROWS:
            n_blocks //= 2
        assert n_blocks >= 1
        spec = pl.BlockSpec((rows // n_blocks, cols),
                            lambda i, last=n_blocks - 1: (jnp.minimum(i, last), 0))
        specs.append(spec)
        shapes.append(jax.ShapeDtypeStruct(w.shape, BF16))
    return specs, shapes


def _mix_in(x2d, mods, seq_len, g_pre, w_in, w_pool, cast_weights=(), dft_job=None):
    ada, first_row = mods
    n_tok = x2d.shape[0]
    t = MIX_IN_TILE
    tiles_per_seq = seq_len // t
    halo_blocks_per_tile = t // HALO
    n_halo_blocks = n_tok // HALO
    row = lambda i: (i, 0)
    cast_specs, cast_shapes = _cast_specs(cast_weights, n_tok // t)
    job = dft_job(n_tok // t, lambda i: i) if dft_job else None
    job_in = job.in_specs if job else []
    job_out = [job.out_spec] if job else []
    job_shape = [job.out_shape] if job else []
    job_args = job.args if job else ()
    return pl.pallas_call(
        functools.partial(_mix_in_kernel, tiles_per_seq, seq_len, len(cast_weights),
                          job is not None, first_row),
        grid=(n_tok // t,),
        in_specs=[
            pl.BlockSpec((t, D_MODEL), row),
            pl.BlockSpec((HALO, D_MODEL),
                         lambda i: (jnp.maximum(i * halo_blocks_per_tile - 1, 0), 0)),
            pl.BlockSpec((HALO, D_MODEL),
                         lambda i: (jnp.minimum((i + 1) * halo_blocks_per_tile,
                                                n_halo_blocks - 1), 0)),
            _const_spec(ada.shape),
            _const_spec((1, D_MODEL)),
            _const_spec((D_MODEL, 2 * BRANCH_WIDTH + 2 * D_MODEL)),
            _const_spec((BRANCH_WIDTH, D_MODEL)),
        ] + cast_specs + job_in,
        out_specs=[
            pl.BlockSpec((t, BRANCH_WIDTH), row),
            pl.BlockSpec((t, D_MODEL), row),
            pl.BlockSpec((t, D_MODEL), row),
        ] + cast_specs + job_out,
        out_shape=[
            jax.ShapeDtypeStruct((n_tok, BRANCH_WIDTH), BF16),
            jax.ShapeDtypeStruct((n_tok, D_MODEL), BF16),
            jax.ShapeDtypeStruct((n_tok, D_MODEL), BF16),
        ] + cast_shapes + job_shape,
        compiler_params=pltpu.CompilerParams(
            dimension_semantics=("arbitrary",), vmem_limit_bytes=VMEM_LIMIT_BYTES),
        name="mix_in",
    )(x2d, x2d, x2d, ada, g_pre, w_in, w_pool, *cast_weights, *job_args)


class _Stage1Body:
    def __init__(self, x_ref, w_ref, o_ref):
        self.x_ref, self.w_ref, self.o_ref = x_ref, w_ref, o_ref

    def load(self):
        n1, c2, width = self.x_ref.shape
        self.xt = jnp.swapaxes(self.x_ref[...], 0, 1).reshape(
            n1 * c2 // SEQ_MINOR, SEQ_MINOR, width)

    def matmul(self):
        w = self.w_ref[...]
        self.z = jnp.stack([_dot(w, self.xt[g]).astype(BF16)
                            for g in range(self.xt.shape[0])])

    def store(self):
        n1, c2, width = self.x_ref.shape
        self.o_ref[...] = jnp.swapaxes(self.z.reshape(c2, n1, width), 0, 1)

    def run(self):
        self.load()
        self.matmul()
        self.store()


class _Stage1Job:
    def __init__(self, u4, w_bd, n_steps, lin):
        batch, n1, n2, width = u4.shape
        assert n_steps % batch == 0
        chunks = n_steps // batch
        c2 = n2 // chunks
        assert n2 % chunks == 0 and c2 % BF16_ROWS == 0 and (c2 * n1) % SEQ_MINOR == 0
        index = lambda *g: (lin(*g) // chunks, 0, lin(*g) % chunks, 0)
        self.args = (u4, w_bd)
        self.in_specs = [pl.BlockSpec((None, n1, c2, width), index),
                         _const_spec((SEQ_MINOR, SEQ_MINOR))]
        self.out_spec = pl.BlockSpec((None, n1, c2, width), index)
        self.out_shape = jax.ShapeDtypeStruct((batch, n1, n2, width), BF16)


def _packed_windows(n1, step):
    rb, half = ROW_BLOCKS_PER_STEP, n1 // 2
    upper = rb * step >= max(half, rb)
    re_start = jnp.where(upper, n1 - rb * step - (rb - 1), rb * step)
    im_start = jnp.minimum(jnp.where(upper, re_start + half, half + rb * step), n1 - rb)
    return re_start, im_start


def _stage2_body(n1, step, re_row, im_row, c2_ref, s2_ref, tc_ref, ts_ref, cs_ref, o_ref):
    c2 = c2_ref[...]
    s2 = s2_ref[...]
    half = n1 // 2
    qs = []
    for j in range(ROW_BLOCKS_PER_STEP):
        k1 = step * ROW_BLOCKS_PER_STEP + j
        mirrored = k1 > half
        kp = jnp.where(mirrored, n1 - k1, k1)
        real_only = (kp == 0) | (kp == half)
        sign = jnp.where(real_only, 0.0, jnp.where(mirrored, -1.0, 1.0))
        tc = tc_ref[pl.ds(k1, 1), :]
        ts = ts_ref[pl.ds(k1, 1), :]
        cos_t = c2 * tc - s2 * ts
        sin_t = s2 * tc + c2 * ts
        lhs = jnp.concatenate(
            [jnp.concatenate([cos_t, sign * sin_t], axis=1),
             jnp.concatenate([-sin_t, sign * cos_t], axis=1)], axis=0).astype(BF16)
        z = jnp.concatenate(
            [re_row(kp), im_row(jnp.where(real_only, half + 1, half + kp))], axis=0)
        qs.append(_dot(lhs, z).astype(BF16))
    halves = []
    for pair in range(N_GROUPS // 2):
        rows = []
        for q in qs:
            parts = []
            for g in (2 * pair, 2 * pair + 1):
                lanes = slice(g * GROUP_DIM, (g + 1) * GROUP_DIM)
                parts += [q[:SEQ_MINOR, lanes], q[SEQ_MINOR:, lanes]]
            rows.append(jnp.concatenate(parts, axis=1))
        halves.append(_dot(jnp.concatenate(rows, axis=0), cs_ref[...]))
    y = jnp.concatenate(halves, axis=1).astype(o_ref.dtype)
    o_ref[...] = y.reshape(o_ref.shape)


def _dft_stage2_kernel(n1, zr_ref, zi_ref, *table_and_out_refs):
    step = pl.program_id(1)
    re_start, im_start = _packed_windows(n1, step)
    _stage2_body(n1, step, lambda r: zr_ref[0, r - re_start],
                 lambda r: zi_ref[0, jnp.maximum(r - im_start, 0)], *table_and_out_refs)


def _dft_stage2(z4, tables):
    batch, n1, n2, width = z4.shape
    rb = ROW_BLOCKS_PER_STEP
    assert n1 % rb == 0 and n2 == SEQ_MINOR
    window = tuple(pl.Element(n) for n in (1, rb, n2, width))
    return pl.pallas_call(
        functools.partial(_dft_stage2_kernel, n1),
        grid=(batch, n1 // rb),
        in_specs=[
            pl.BlockSpec(window, lambda b, i: (b, _packed_windows(n1, i)[0], 0, 0)),
            pl.BlockSpec(window, lambda b, i: (b, _packed_windows(n1, i)[1], 0, 0)),
        ] + _stage2_table_specs(n1),
        out_specs=pl.BlockSpec((None, rb, n2, width), lambda b, i: (b, i, 0, 0)),
        out_shape=jax.ShapeDtypeStruct((batch, n1, n2, width), BF16),
        compiler_params=pltpu.CompilerParams(
            dimension_semantics=("arbitrary", "arbitrary"),
            vmem_limit_bytes=VMEM_LIMIT_BYTES),
        name="dft_stage2",
    )(z4, z4, *tables)


def _stage2_table_specs(n1):
    return [_const_spec((SEQ_MINOR, SEQ_MINOR)), _const_spec((SEQ_MINOR, SEQ_MINOR)),
            _const_spec((n1, SEQ_MINOR)), _const_spec((n1, SEQ_MINOR)),
            _const_spec((2 * MXU_DIM, MXU_DIM))]


def _dft_fused_kernel(x_ref, w_ref, c2_ref, s2_ref, tc_ref, ts_ref, cs_ref, o_ref, z_ref):
    _Stage1Body(x_ref, w_ref, z_ref).run()
    packed_row = lambda r: z_ref[r]
    _stage2_body(x_ref.shape[0], 0, packed_row, packed_row,
                 c2_ref, s2_ref, tc_ref, ts_ref, cs_ref, o_ref)


def _dft_fused(u4, w_bd, tables):
    batch, n1, n2, width = u4.shape
    seq_block = pl.BlockSpec((None, n1, n2, width), lambda b: (b, 0, 0, 0))
    return pl.pallas_call(
        _dft_fused_kernel,
        grid=(batch,),
        in_specs=[seq_block, _const_spec((SEQ_MINOR, SEQ_MINOR))] + _stage2_table_specs(n1),
        out_specs=seq_block,
        out_shape=jax.ShapeDtypeStruct((batch, n1, n2, width), BF16),
        scratch_shapes=[pltpu.VMEM((n1, n2, width), BF16)],
        compiler_params=pltpu.CompilerParams(
            dimension_semantics=("arbitrary",), vmem_limit_bytes=VMEM_LIMIT_BYTES),
        name="dft_fused",
    )(u4, w_bd, *tables)


@functools.lru_cache(maxsize=None)
def _dft_tables(seq_len):
    n2 = SEQ_MINOR
    n1 = seq_len // n2
    reps = n2 // n1
    k = np.arange(n1, dtype=np.float64)
    ang1 = 2.0 * np.pi * np.outer(k, k) / n1
    half = n1 // 2
    w1 = np.concatenate([np.cos(ang1)[:half + 1], -np.sin(ang1)[1:half]], axis=0)
    w_bd = np.kron(np.eye(reps), w1)
    m = np.arange(n2, dtype=np.float64)
    ang2 = 2.0 * np.pi * np.outer(m, m) / n2
    angt = 2.0 * np.pi * np.outer(k, m) / seq_len
    as32 = lambda a: np.asarray(a, dtype=np.float32)
    return (as32(w_bd), as32(np.cos(ang2)), as32(np.sin(ang2)),
            as32(np.cos(angt)), as32(np.sin(angt)))


@functools.lru_cache(maxsize=None)
def _channel_dft_table(seq_len):
    c = np.arange(GROUP_DIM, dtype=np.float64)
    ang = 2.0 * np.pi * np.outer(c, c) / GROUP_DIM
    norm = 1.0 / np.sqrt(float(seq_len) * GROUP_DIM)
    cs = np.concatenate([np.cos(ang), np.sin(ang)], axis=0) * norm
    return np.asarray(np.kron(np.eye(2), cs), dtype=np.float32)


def _stage1_weights(seq_len):
    return jnp.asarray(_dft_tables(seq_len)[0]).astype(BF16)


def _stage2_tables(seq_len):
    _, c2, s2, tc, ts = _dft_tables(seq_len)
    return (jnp.asarray(c2), jnp.asarray(s2), jnp.asarray(tc), jnp.asarray(ts),
            jnp.asarray(_channel_dft_table(seq_len)).astype(BF16))


def _as_blocks(u_f, batch, seq_len):
    return u_f.reshape(batch, seq_len // SEQ_MINOR, SEQ_MINOR, BRANCH_WIDTH)


def _mix_out_pipeline_kernel(grid, in_specs, out_spec, *refs):
    (x_hbm, y_hbm, gf_hbm, p_hbm, ada_hbm), consts, o_hbm = refs[:5], refs[5:-1], refs[-1]
    pltpu.emit_pipeline(
        functools.partial(_mix_out_kernel, *consts), grid=grid,
        in_specs=in_specs, out_specs=[out_spec],
    )(x_hbm, y_hbm, gf_hbm, p_hbm, ada_hbm, o_hbm)


def _mix_out_kernel(gpm_ref, gpf_ref, gqf_ref, wfo_ref, wout_ref, wgate_ref, wup_ref, wdown_ref,
                    x_ref, y_ref, gf_ref, p_ref, ada_ref, o_ref):
    _, _, gate1, shift2, scale2, gate2 = _ada_vectors(ada_ref, 0)
    n_sub = MIX_OUT_TILE // MIX_OUT_SUB
    gate1 = gate1 * gpm_ref[...]
    scale2 = gpf_ref[...] * (1.0 + scale2)
    gate2 = gate2 * gqf_ref[...]
    k2_sub = MIX_OUT_SUB // K1_PER_TILE

    def rows_of(ref, k2s):
        return ref[k2s].reshape(MIX_OUT_SUB, ref.shape[-1])

    def up_proj(h2, chunk):
        cols = pl.ds(*chunk)
        return _dot(h2, wgate_ref[:, cols]), _dot(h2, wup_ref[:, cols])

    def down_proj(ab, chunk):
        a, b = ab
        return _dot((a * _sigmoid(a) * b).astype(BF16), wdown_ref[pl.ds(*chunk), :])

    def front(group):
        ys = [jnp.swapaxes(y_ref[:, k2s, :], 0, 1).reshape(MIX_OUT_SUB, BRANCH_WIDTH)
              for k2s in group]
        y_fs = [_dot(y, wfo_ref[...]) for y in ys]
        ms = [_dot((rows_of(gf_ref, k2s).astype(F32) * y_f
                    + rows_of(p_ref, k2s).astype(F32)).astype(BF16), wout_ref[...])
              for k2s, y_f in zip(group, y_fs)]
        x1s = [rows_of(x_ref, k2s) + gate1 * (m * _rms_scale(m))
               for k2s, m in zip(group, ms)]
        h2s = [(x1 * _rms_scale(x1) * scale2 + shift2).astype(BF16) for x1 in x1s]
        return x1s, h2s

    def ffn(h2s):
        zs = [None] * len(h2s)
        pending = [up_proj(h2, FF_CHUNKS[0]) for h2 in h2s]
        for c, chunk in enumerate(FF_CHUNKS):
            last = c + 1 == len(FF_CHUNKS)
            nxt = None if last else [up_proj(h2, FF_CHUNKS[c + 1]) for h2 in h2s]
            for s in range(len(h2s)):
                part = down_proj(pending[s], chunk)
                zs[s] = part if zs[s] is None else zs[s] + part
            pending = nxt
        return zs

    def run_group(g):
        base = pl.multiple_of(g * (SUBS_PER_GROUP * k2_sub), SUBS_PER_GROUP * k2_sub)
        group = [pl.ds(base + s * k2_sub, k2_sub) for s in range(SUBS_PER_GROUP)]
        x1s, h2s = front(group)
        for k2s, x1, z in zip(group, x1s, ffn(h2s)):
            out = x1 + gate2 * (z * _rms_scale(z))
            o_ref[k2s] = out.reshape(k2_sub, K1_PER_TILE, D_MODEL)

    n_groups = n_sub // SUBS_PER_GROUP
    if n_groups == 1:
        run_group(0)
    else:
        lax.fori_loop(0, n_groups, lambda g, carry: (run_group(g), carry)[1], 0)


def _mix_out(x, y4, g_f, p, mods, g_post_mix, g_pre_ffn, g_post_ffn,
             w_fo, w_out, w_gate, w_up, w_down):
    batch, seq_len, d = x.shape
    n1 = y4.shape[1]
    n2 = SEQ_MINOR
    k1t, k2t = K1_PER_TILE, K2_PER_TILE
    assert n1 % k1t == 0 and n2 % k2t == 0 and k1t * k2t == MIX_OUT_TILE
    by_k = lambda a: a.reshape(batch, n2, n1 // k1t, k1t, a.shape[-1])
    tok_spec = pl.BlockSpec((None, k2t, None, k1t, d), lambda b, a, t: (b, a, t, 0, 0))
    ada, first_row = mods
    tile_specs = [
        tok_spec,
        pl.BlockSpec((None, k1t, k2t, BRANCH_WIDTH), lambda b, a, t: (b, t, a, 0)),
        tok_spec,
        tok_spec,
        pl.BlockSpec((None, 1, ada.shape[1]), lambda b, a, t: (first_row + b, 0, 0)),
    ]
    consts = (g_post_mix, g_pre_ffn, g_post_ffn, w_fo, w_out, w_gate, w_up, w_down)
    in_hbm = pl.BlockSpec(memory_space=pl.ANY)
    return pl.pallas_call(
        functools.partial(_mix_out_pipeline_kernel, (batch, n2 // k2t, n1 // k1t),
                          tile_specs, tok_spec),
        in_specs=[in_hbm] * len(tile_specs)
        + [pl.BlockSpec(memory_space=pltpu.VMEM)] * len(consts),
        out_specs=in_hbm,
        out_shape=jax.ShapeDtypeStruct((batch, n2, n1 // k1t, k1t, d), F32),
        compiler_params=pltpu.CompilerParams(vmem_limit_bytes=VMEM_LIMIT_BYTES),
        name="mix_out",
    )(by_k(x), y4, by_k(g_f), by_k(p), ada.reshape(ada.shape[0], 1, ada.shape[1]),
      *consts).reshape(batch, seq_len, d)


def _layer_pair(x_a, mods_a, x_b, mods_b, early, gains, late):
    (ba, sa, d), (bb, sb, _) = x_a.shape, x_b.shape
    assert sb == ROW_BLOCKS_PER_STEP * SEQ_MINOR
    u_a, gf_a, p_a, *late = _mix_in(x_a.reshape(ba * sa, d), mods_a, sa, *early,
                                    cast_weights=late)
    u4_a, w_a = _as_blocks(u_a, ba, sa), _stage1_weights(sa)
    u_b, gf_b, p_b, z_a = _mix_in(
        x_b.reshape(bb * sb, d), mods_b, sb, *early,
        dft_job=lambda n_steps, lin: _Stage1Job(u4_a, w_a, n_steps, lin))
    y_a = _dft_stage2(z_a, _stage2_tables(sa))
    out_a = _mix_out(x_a, y_a, gf_a, p_a, mods_a, *gains, *late)
    y_b = _dft_fused(_as_blocks(u_b, bb, sb), _stage1_weights(sb), _stage2_tables(sb))
    out_b = _mix_out(x_b, y_b, gf_b, p_b, mods_b, *gains, *late)
    return out_a, out_b


def kernel(x_prompt, x_sample, c_prompt, c_sample, w_ada, b_ada, g_pre_mix, w_in, w_fo, w_pg,
           pool_scale, w_po, w_out, g_post_mix, g_pre_ffn, w_gate, w_up, w_down, g_post_ffn):
    depth = w_ada.shape[0]
    y_prompt, y_sample = x_prompt, x_sample
    bp, bs = c_prompt.shape[0], c_sample.shape[0]
    pad_rows = -(bp + bs) % BF16_ROWS
    c_pad = jnp.concatenate(
        [c_prompt, c_sample, jnp.zeros((pad_rows, D_MODEL), F32)], axis=0)
    for l in range(depth):
        ada, w_in_bf16, w_pool = _prep(c_pad, w_ada[l], b_ada[l][None, :], w_in[l], w_pg[l],
                                       pool_scale[l][None, :], w_po[l])
        mods_p, mods_s = (ada, 0), (ada, bp)
        early = (g_pre_mix[l][None, :], w_in_bf16, w_pool)
        gains = (g_post_mix[l][None, :], g_pre_ffn[l][None, :], g_post_ffn[l][None, :])
        late = (w_fo[l], w_out[l], w_gate[l], w_up[l], w_down[l])
        y_prompt, y_sample = _layer_pair(y_prompt, mods_p, y_sample, mods_s, early, gains, late)
    return (y_prompt, y_sample)
```

```python
import functools

import numpy as np
import jax
import jax.numpy as jnp
from jax import lax
from jax.experimental import pallas as pl
from jax.experimental.pallas import tpu as pltpu

D_MODEL = 1024
N_GROUPS = 4
GROUP_DIM = 128
BRANCH_WIDTH = N_GROUPS * GROUP_DIM
POOL_WINDOWS = (2, 4, 8, 16)
D_FF = 2816
N_ADA = 6
EPS = 1e-6

F32_ROWS = 8
BF16_ROWS = 16
MXU_DIM = 256
VMEM_LIMIT_BYTES = 60 * 1024 * 1024

SEQ_MINOR = 128
HALO = BF16_ROWS
SUB_TILE = 256
MIX_IN_TILE = 1024
MIX_OUT_TILE = 1024
MIX_OUT_SUB = 256
SUBS_PER_GROUP = 2
K1_PER_TILE = BF16_ROWS
K2_PER_TILE = MIX_OUT_TILE // K1_PER_TILE
ROW_BLOCKS_PER_STEP = 16
PREP_ROWS = 256
FF_CHUNKS = ((0, 1024), (1024, 1024), (2048, 768))

BF16 = jnp.bfloat16
F32 = jnp.float32


def _dot(a, b):
    return jnp.dot(a, b, preferred_element_type=F32)


def _sigmoid(v):
    return 1.0 / (1.0 + jnp.exp(-v))


def _rms_scale(v):
    return lax.rsqrt(jnp.mean(v * v, axis=-1, keepdims=True) + EPS)


def _prep_kernel(c_ref, wada_ref, b_ref, win_ref, wpg_ref, ps_ref, wpo_ref,
                 ada_ref, winb_ref, wpool_ref):
    k = pl.program_id(0)

    @pl.when(k == 0)
    def _():
        ada_ref[...] = jnp.broadcast_to(b_ref[...], ada_ref.shape)
        for g in range(N_GROUPS):
            rows = pl.ds(g * GROUP_DIM, GROUP_DIM)
            scaled = wpg_ref[g] * ps_ref[:, rows]
            wpool_ref[rows, :] = jnp.dot(
                scaled, wpo_ref[rows, :], preferred_element_type=F32,
                precision=lax.Precision.HIGHEST).astype(wpool_ref.dtype)

    c = c_ref[:, pl.ds(pl.multiple_of(k * PREP_ROWS, PREP_ROWS), PREP_ROWS)]
    s = (c * _sigmoid(c)).astype(BF16)
    ada_ref[...] += _dot(s, wada_ref[...].astype(BF16))
    winb_ref[...] = win_ref[...].astype(winb_ref.dtype)


def _prep(c_pad, w_ada, b_ada, w_in, w_pg, pool_scale, w_po):
    rows, d = c_pad.shape
    n_ada, n_in = w_ada.shape[1], w_in.shape[1]
    block = lambda cols: pl.BlockSpec((PREP_ROWS, cols), lambda k: (k, 0))
    whole = lambda a: pl.BlockSpec(a.shape, lambda k: (0,) * a.ndim)
    return pl.pallas_call(
        _prep_kernel,
        grid=(d // PREP_ROWS,),
        in_specs=[whole(c_pad), block(n_ada), whole(b_ada), block(n_in),
                  whole(w_pg), whole(pool_scale), whole(w_po)],
        out_specs=[pl.BlockSpec((rows, n_ada), lambda k: (0, 0)), block(n_in),
                   pl.BlockSpec(w_po.shape, lambda k: (0, 0))],
        out_shape=[jax.ShapeDtypeStruct((rows, n_ada), F32),
                   jax.ShapeDtypeStruct(w_in.shape, BF16),
                   jax.ShapeDtypeStruct(w_po.shape, BF16)],
        compiler_params=pltpu.CompilerParams(
            dimension_semantics=("arbitrary",), vmem_limit_bytes=VMEM_LIMIT_BYTES),
        name="prep",
    )(c_pad, w_ada, b_ada, w_in, w_pg, pool_scale, w_po)


def _window_sum(ext, w):
    n = ext.shape[0]
    s = ext
    k = 1
    while k < w // 2:
        s = s + pltpu.roll(s, n - k, axis=0)
        k *= 2
    return s + pltpu.roll(s, w // 2, axis=0)


def _ada_vectors(ada_ref, row):
    mod = ada_ref[pl.ds(row, 1), :]
    return [mod[:, k * D_MODEL:(k + 1) * D_MODEL] for k in range(N_ADA)]


def _mix_in_kernel(tiles_per_seq, seq_len, n_cast, has_dft_job, first_row,
                   x_ref, xp_ref, xn_ref, ada_ref, g_ref, win_ref, wpool_ref, *rest):
    wf_ref = win_ref.at[:, pl.ds(0, BRANCH_WIDTH)]
    wp_ref = win_ref.at[:, pl.ds(BRANCH_WIDTH, BRANCH_WIDTH)]
    wg_ref = win_ref.at[:, pl.ds(2 * BRANCH_WIDTH, 2 * D_MODEL)]
    n_extra_in = n_cast + 2 * has_dft_job
    cast_in, dft_in = rest[:n_cast], rest[n_cast:n_extra_in]
    uf_ref, gf_ref, p_ref = rest[n_extra_in:n_extra_in + 3]
    cast_out = rest[n_extra_in + 3:n_extra_in + 3 + n_cast]
    dft = _Stage1Body(*dft_in, rest[-1]) if has_dft_job else None
    if dft:
        dft.load()
    for src, dst in zip(cast_in, cast_out):
        dst[...] = src[...].astype(dst.dtype)
    t = MIX_IN_TILE
    n_sub = t // SUB_TILE
    ti = lax.rem(pl.program_id(0), tiles_per_seq)
    shift, scale1 = _ada_vectors(ada_ref, first_row + pl.program_id(0) // tiles_per_seq)[:2]
    scale = g_ref[...] * (1.0 + scale1)

    def normed(v):
        return v * _rms_scale(v) * scale + shift

    ts = SUB_TILE
    subs = [pl.ds(s * ts, ts) for s in range(n_sub)]
    hs = [normed(x_ref[rows, :]).astype(BF16) for rows in subs]
    h_prev = jnp.where(ti == 0, 0.0, normed(xp_ref[...])).astype(BF16)
    h_next = jnp.where(ti == tiles_per_seq - 1, 0.0, normed(xn_ref[...])).astype(BF16)
    afters = [h[:HALO] for h in hs[1:]] + [h_next]

    slab = F32_ROWS
    row = lax.broadcasted_iota(jnp.int32, (slab, GROUP_DIM), 0)

    def pool_diff(up, first_pos):
        diffs = []
        for g, w in enumerate(POOL_WINDOWS):
            assert w & (w - 1) == 0 and w // 2 <= slab
            left = w // 2
            right = w - 1 - left
            ext = up[:, g * GROUP_DIM:(g + 1) * GROUP_DIM]
            win = _window_sum(ext, w)[HALO:HALO + ts]
            u = ext[HALO:HALO + ts]

            def count(pos):
                return (jnp.minimum(pos + right, seq_len - 1)
                        - jnp.maximum(pos - left, 0) + 1).astype(F32)

            diffs.append(jnp.concatenate([
                win[:slab] / count(first_pos + row) - u[:slab],
                win[slab:ts - slab] * (1.0 / w) - u[slab:ts - slab],
                win[ts - slab:] / count(first_pos + (ts - slab) + row) - u[ts - slab:],
            ], axis=0))
        return diffs

    def project(s):
        head = [h_prev, hs[0][:HALO]] if s == 0 else []
        up = _dot(jnp.concatenate(head + [hs[s][HALO:], afters[s]], axis=0), wp_ref[...])
        uf_ref[subs[s], :] = _dot(hs[s], wf_ref[...]).astype(uf_ref.dtype)
        return up, _dot(hs[s], wg_ref[...])

    def pool_rows(s):
        if s == 0:
            return proj[0][0]
        return jnp.concatenate([proj[s - 1][0][-2 * HALO:], proj[s][0]], axis=0)

    def finish(s, up, gate_logits):
        d = jnp.concatenate(pool_diff(up, ti * t + s * ts), axis=1).astype(BF16)
        g = _sigmoid(gate_logits)
        gf_ref[subs[s], :] = g[:, :D_MODEL].astype(gf_ref.dtype)
        p_ref[subs[s], :] = (g[:, D_MODEL:] * _dot(d, wpool_ref[...])).astype(p_ref.dtype)

    proj = {}
    for step in range(n_sub + 1):
        if step < n_sub:
            proj[step] = project(step)
        if dft and step == 0:
            dft.matmul()
            dft.store()
        if step >= 1:
            finish(step - 1, pool_rows(step - 1), proj[step - 1][1])


def _const_spec(shape):
    zeros = (0,) * len(shape)
    return pl.BlockSpec(shape, lambda *_: zeros, pipeline_mode=pl.Buffered(1))


def _cast_specs(weights, n_steps):
    specs, shapes = [], []
    for w in weights:
        rows, cols = w.shape
        n_blocks = n_steps
        while rows % n_blocks or (rows // n_blocks) % BF16_ROWS:
            n_blocks //= 2
        assert n_blocks >= 1
        spec = pl.BlockSpec((rows // n_blocks, cols),
                            lambda i, last=n_blocks - 1: (jnp.minimum(i, last), 0))
        specs.append(spec)
        shapes.append(jax.ShapeDtypeStruct(w.shape, BF16))
    return specs, shapes


def _mix_in(x2d, mods, seq_len, g_pre, w_in, w_pool, cast_weights=(), dft_job=None):
    ada, first_row = mods
    n_tok = x2d.shape[0]
    t = MIX_IN_TILE
    tiles_per_seq = seq_len // t
    halo_blocks_per_tile = t // HALO
    n_halo_blocks = n_tok // HALO
    row = lambda i: (i, 0)
    cast_specs, cast_shapes = _cast_specs(cast_weights, n_tok // t)
    job = dft_job(n_tok // t, lambda i: i) if dft_job else None
    job_in = job.in_specs if job else []
    job_out = [job.out_spec] if job else []
    job_shape = [job.out_shape] if job else []
    job_args = job.args if job else ()
    return pl.pallas_call(
        functools.partial(_mix_in_kernel, tiles_per_seq, seq_len, len(cast_weights),
                          job is not None, first_row),
        grid=(n_tok // t,),
        in_specs=[
            pl.BlockSpec((t, D_MODEL), row),
            pl.BlockSpec((HALO, D_MODEL),
                         lambda i: (jnp.maximum(i * halo_blocks_per_tile - 1, 0), 0)),
            pl.BlockSpec((HALO, D_MODEL),
                         lambda i: (jnp.minimum((i + 1) * halo_blocks_per_tile,
                                                n_halo_blocks - 1), 0)),
            _const_spec(ada.shape),
            _const_spec((1, D_MODEL)),
            _const_spec((D_MODEL, 2 * BRANCH_WIDTH + 2 * D_MODEL)),
            _const_spec((BRANCH_WIDTH, D_MODEL)),
        ] + cast_specs + job_in,
        out_specs=[
            pl.BlockSpec((t, BRANCH_WIDTH), row),
            pl.BlockSpec((t, D_MODEL), row),
            pl.BlockSpec((t, D_MODEL), row),
        ] + cast_specs + job_out,
        out_shape=[
            jax.ShapeDtypeStruct((n_tok, BRANCH_WIDTH), BF16),
            jax.ShapeDtypeStruct((n_tok, D_MODEL), BF16),
            jax.ShapeDtypeStruct((n_tok, D_MODEL), BF16),
        ] + cast_shapes + job_shape,
        compiler_params=pltpu.CompilerParams(
            dimension_semantics=("arbitrary",), vmem_limit_bytes=VMEM_LIMIT_BYTES),
        name="mix_in",
    )(x2d, x2d, x2d, ada, g_pre, w_in, w_pool, *cast_weights, *job_args)


class _Stage1Body:
    def __init__(self, x_ref, w_ref, o_ref):
        self.x_ref, self.w_ref, self.o_ref = x_ref, w_ref, o_ref

    def load(self):
        n1, c2, width = self.x_ref.shape
        self.xt = jnp.swapaxes(self.x_ref[...], 0, 1).reshape(
            n1 * c2 // SEQ_MINOR, SEQ_MINOR, width)

    def matmul(self):
        w = self.w_ref[...]
        self.z = jnp.stack([_dot(w, self.xt[g]).astype(BF16)
                            for g in range(self.xt.shape[0])])

    def store(self):
        n1, c2, width = self.x_ref.shape
        self.o_ref[...] = jnp.swapaxes(self.z.reshape(c2, n1, width), 0, 1)

    def run(self):
        self.load()
        self.matmul()
        self.store()


class _Stage1Job:
    def __init__(self, u4, w_bd, n_steps, lin):
        batch, n1, n2, width = u4.shape
        assert n_steps % batch == 0
        chunks = n_steps // batch
        c2 = n2 // chunks
        assert n2 % chunks == 0 and c2 % BF16_ROWS == 0 and (c2 * n1) % SEQ_MINOR == 0
        index = lambda *g: (lin(*g) // chunks, 0, lin(*g) % chunks, 0)
        self.args = (u4, w_bd)
        self.in_specs = [pl.BlockSpec((None, n1, c2, width), index),
                         _const_spec((SEQ_MINOR, SEQ_MINOR))]
        self.out_spec = pl.BlockSpec((None, n1, c2, width), index)
        self.out_shape = jax.ShapeDtypeStruct((batch, n1, n2, width), BF16)


def _packed_windows(n1, step):
    rb, half = ROW_BLOCKS_PER_STEP, n1 // 2
    upper = rb * step >= max(half, rb)
    re_start = jnp.where(upper, n1 - rb * step - (rb - 1), rb * step)
    im_start = jnp.minimum(jnp.where(upper, re_start + half, half + rb * step), n1 - rb)
    return re_start, im_start


def _stage2_body(n1, step, re_row, im_row, c2_ref, s2_ref, tc_ref, ts_ref, cs_ref, o_ref):
    c2 = c2_ref[...]
    s2 = s2_ref[...]
    half = n1 // 2
    qs = []
    for j in range(ROW_BLOCKS_PER_STEP):
        k1 = step * ROW_BLOCKS_PER_STEP + j
        mirrored = k1 > half
        kp = jnp.where(mirrored, n1 - k1, k1)
        real_only = (kp == 0) | (kp == half)
        sign = jnp.where(real_only, 0.0, jnp.where(mirrored, -1.0, 1.0))
        tc = tc_ref[pl.ds(k1, 1), :]
        ts = ts_ref[pl.ds(k1, 1), :]
        cos_t = c2 * tc - s2 * ts
        sin_t = s2 * tc + c2 * ts
        lhs = jnp.concatenate(
            [jnp.concatenate([cos_t, sign * sin_t], axis=1),
             jnp.concatenate([-sin_t, sign * cos_t], axis=1)], axis=0).astype(BF16)
        z = jnp.concatenate(
            [re_row(kp), im_row(jnp.where(real_only, half + 1, half + kp))], axis=0)
        qs.append(_dot(lhs, z).astype(BF16))
    halves = []
    for pair in range(N_GROUPS // 2):
        rows = []
        for q in qs:
            parts = []
            for g in (2 * pair, 2 * pair + 1):
                lanes = slice(g * GROUP_DIM, (g + 1) * GROUP_DIM)
                parts += [q[:SEQ_MINOR, lanes], q[SEQ_MINOR:, lanes]]
            rows.append(jnp.concatenate(parts, axis=1))
        halves.append(_dot(jnp.concatenate(rows, axis=0), cs_ref[...]))
    y = jnp.concatenate(halves, axis=1).astype(o_ref.dtype)
    o_ref[...] = y.reshape(o_ref.shape)


def _dft_stage2_kernel(n1, zr_ref, zi_ref, *table_and_out_refs):
    step = pl.program_id(1)
    re_start, im_start = _packed_windows(n1, step)
    _stage2_body(n1, step, lambda r: zr_ref[0, r - re_start],
                 lambda r: zi_ref[0, jnp.maximum(r - im_start, 0)], *table_and_out_refs)


def _dft_stage2(z4, tables):
    batch, n1, n2, width = z4.shape
    rb = ROW_BLOCKS_PER_STEP
    assert n1 % rb == 0 and n2 == SEQ_MINOR
    window = tuple(pl.Element(n) for n in (1, rb, n2, width))
    return pl.pallas_call(
        functools.partial(_dft_stage2_kernel, n1),
        grid=(batch, n1 // rb),
        in_specs=[
            pl.BlockSpec(window, lambda b, i: (b, _packed_windows(n1, i)[0], 0, 0)),
            pl.BlockSpec(window, lambda b, i: (b, _packed_windows(n1, i)[1], 0, 0)),
        ] + _stage2_table_specs(n1),
        out_specs=pl.BlockSpec((None, rb, n2, width), lambda b, i: (b, i, 0, 0)),
        out_shape=jax.ShapeDtypeStruct((batch, n1, n2, width), BF16),
        compiler_params=pltpu.CompilerParams(
            dimension_semantics=("arbitrary", "arbitrary"),
            vmem_limit_bytes=VMEM_LIMIT_BYTES),
        name="dft_stage2",
    )(z4, z4, *tables)


def _stage2_table_specs(n1):
    return [_const_spec((SEQ_MINOR, SEQ_MINOR)), _const_spec((SEQ_MINOR, SEQ_MINOR)),
            _const_spec((n1, SEQ_MINOR)), _const_spec((n1, SEQ_MINOR)),
            _const_spec((2 * MXU_DIM, MXU_DIM))]


def _dft_fused_kernel(x_ref, w_ref, c2_ref, s2_ref, tc_ref, ts_ref, cs_ref, o_ref, z_ref):
    _Stage1Body(x_ref, w_ref, z_ref).run()
    packed_row = lambda r: z_ref[r]
    _stage2_body(x_ref.shape[0], 0, packed_row, packed_row,
                 c2_ref, s2_ref, tc_ref, ts_ref, cs_ref, o_ref)


def _dft_fused(u4, w_bd, tables):
    batch, n1, n2, width = u4.shape
    seq_block = pl.BlockSpec((None, n1, n2, width), lambda b: (b, 0, 0, 0))
    return pl.pallas_call(
        _dft_fused_kernel,
        grid=(batch,),
        in_specs=[seq_block, _const_spec((SEQ_MINOR, SEQ_MINOR))] + _stage2_table_specs(n1),
        out_specs=seq_block,
        out_shape=jax.ShapeDtypeStruct((batch, n1, n2, width), BF16),
        scratch_shapes=[pltpu.VMEM((n1, n2, width), BF16)],
        compiler_params=pltpu.CompilerParams(
            dimension_semantics=("arbitrary",), vmem_limit_bytes=VMEM_LIMIT_BYTES),
        name="dft_fused",
    )(u4, w_bd, *tables)


@functools.lru_cache(maxsize=None)
def _dft_tables(seq_len):
    n2 = SEQ_MINOR
    n1 = seq_len // n2
    reps = n2 // n1
    k = np.arange(n1, dtype=np.float64)
    ang1 = 2.0 * np.pi * np.outer(k, k) / n1
    half = n1 // 2
    w1 = np.concatenate([np.cos(ang1)[:half + 1], -np.sin(ang1)[1:half]], axis=0)
    w_bd = np.kron(np.eye(reps), w1)
    m = np.arange(n2, dtype=np.float64)
    ang2 = 2.0 * np.pi * np.outer(m, m) / n2
    angt = 2.0 * np.pi * np.outer(k, m) / seq_len
    as32 = lambda a: np.asarray(a, dtype=np.float32)
    return (as32(w_bd), as32(np.cos(ang2)), as32(np.sin(ang2)),
            as32(np.cos(angt)), as32(np.sin(angt)))


@functools.lru_cache(maxsize=None)
def _channel_dft_table(seq_len):
    c = np.arange(GROUP_DIM, dtype=np.float64)
    ang = 2.0 * np.pi * np.outer(c, c) / GROUP_DIM
    norm = 1.0 / np.sqrt(float(seq_len) * GROUP_DIM)
    cs = np.concatenate([np.cos(ang), np.sin(ang)], axis=0) * norm
    return np.asarray(np.kron(np.eye(2), cs), dtype=np.float32)


def _stage1_weights(seq_len):
    return jnp.asarray(_dft_tables(seq_len)[0]).astype(BF16)


def _stage2_tables(seq_len):
    _, c2, s2, tc, ts = _dft_tables(seq_len)
    return (jnp.asarray(c2), jnp.asarray(s2), jnp.asarray(tc), jnp.asarray(ts),
            jnp.asarray(_channel_dft_table(seq_len)).astype(BF16))


def _as_blocks(u_f, batch, seq_len):
    return u_f.reshape(batch, seq_len // SEQ_MINOR, SEQ_MINOR, BRANCH_WIDTH)


def _mix_out_kernel(first_row, x_ref, y_ref, gf_ref, p_ref, ada_ref, gpm_ref, gpf_ref, gqf_ref,
                    wfo_ref, wout_ref, wgate_ref, wup_ref, wdown_ref, o_ref):
    _, _, gate1, shift2, scale2, gate2 = _ada_vectors(ada_ref, first_row + pl.program_id(0))
    n_sub = MIX_OUT_TILE // MIX_OUT_SUB
    gate1 = gate1 * gpm_ref[...]
    scale2 = gpf_ref[...] * (1.0 + scale2)
    gate2 = gate2 * gqf_ref[...]
    k2_sub = MIX_OUT_SUB // K1_PER_TILE

    def rows_of(ref, k2s):
        return ref[k2s].reshape(MIX_OUT_SUB, ref.shape[-1])

    def up_proj(h2, chunk):
        cols = pl.ds(*chunk)
        return _dot(h2, wgate_ref[:, cols]), _dot(h2, wup_ref[:, cols])

    def down_proj(ab, chunk):
        a, b = ab
        return _dot((a * _sigmoid(a) * b).astype(BF16), wdown_ref[pl.ds(*chunk), :])

    def front(group):
        ys = [jnp.swapaxes(y_ref[:, k2s, :], 0, 1).reshape(MIX_OUT_SUB, BRANCH_WIDTH)
              for k2s in group]
        y_fs = [_dot(y, wfo_ref[...]) for y in ys]
        ms = [_dot((rows_of(gf_ref, k2s).astype(F32) * y_f
                    + rows_of(p_ref, k2s).astype(F32)).astype(BF16), wout_ref[...])
              for k2s, y_f in zip(group, y_fs)]
        x1s = [rows_of(x_ref, k2s) + gate1 * (m * _rms_scale(m))
               for k2s, m in zip(group, ms)]
        h2s = [(x1 * _rms_scale(x1) * scale2 + shift2).astype(BF16) for x1 in x1s]
        return x1s, h2s

    def ffn(h2s):
        zs = [None] * len(h2s)
        pending = [up_proj(h2, FF_CHUNKS[0]) for h2 in h2s]
        for c, chunk in enumerate(FF_CHUNKS):
            last = c + 1 == len(FF_CHUNKS)
            nxt = None if last else [up_proj(h2, FF_CHUNKS[c + 1]) for h2 in h2s]
            for s in range(len(h2s)):
                part = down_proj(pending[s], chunk)
                zs[s] = part if zs[s] is None else zs[s] + part
            pending = nxt
        return zs

    def run_group(g):
        base = pl.multiple_of(g * (SUBS_PER_GROUP * k2_sub), SUBS_PER_GROUP * k2_sub)
        group = [pl.ds(base + s * k2_sub, k2_sub) for s in range(SUBS_PER_GROUP)]
        x1s, h2s = front(group)
        for k2s, x1, z in zip(group, x1s, ffn(h2s)):
            out = x1 + gate2 * (z * _rms_scale(z))
            o_ref[k2s] = out.reshape(k2_sub, K1_PER_TILE, D_MODEL)

    n_groups = n_sub // SUBS_PER_GROUP
    if n_groups == 1:
        run_group(0)
    else:
        lax.fori_loop(0, n_groups, lambda g, carry: (run_group(g), carry)[1], 0)


def _mix_out(x, y4, g_f, p, mods, g_post_mix, g_pre_ffn, g_post_ffn,
             w_fo, w_out, w_gate, w_up, w_down):
    batch, seq_len, d = x.shape
    n1 = y4.shape[1]
    n2 = SEQ_MINOR
    k1t, k2t = K1_PER_TILE, K2_PER_TILE
    assert n1 % k1t == 0 and n2 % k2t == 0 and k1t * k2t == MIX_OUT_TILE
    by_k = lambda a: a.reshape(batch, n2, n1 // k1t, k1t, a.shape[-1])
    tok_spec = pl.BlockSpec((None, k2t, None, k1t, d), lambda b, a, t: (b, a, t, 0, 0))
    ada, first_row = mods
    return pl.pallas_call(
        functools.partial(_mix_out_kernel, first_row),
        grid=(batch, n2 // k2t, n1 // k1t),
        in_specs=[
            tok_spec,
            pl.BlockSpec((None, k1t, k2t, BRANCH_WIDTH), lambda b, a, t: (b, t, a, 0)),
            tok_spec,
            tok_spec,
            _const_spec(ada.shape),
            _const_spec((1, D_MODEL)),
            _const_spec((1, D_MODEL)),
            _const_spec((1, D_MODEL)),
            _const_spec((BRANCH_WIDTH, D_MODEL)),
            _const_spec((D_MODEL, D_MODEL)),
            _const_spec((D_MODEL, D_FF)),
            _const_spec((D_MODEL, D_FF)),
            _const_spec((D_FF, D_MODEL)),
        ],
        out_specs=tok_spec,
        out_shape=jax.ShapeDtypeStruct((batch, n2, n1 // k1t, k1t, d), F32),
        compiler_params=pltpu.CompilerParams(
            dimension_semantics=("arbitrary", "arbitrary", "arbitrary"),
            vmem_limit_bytes=VMEM_LIMIT_BYTES),
        name="mix_out",
    )(by_k(x), y4, by_k(g_f), by_k(p), ada, g_post_mix, g_pre_ffn, g_post_ffn,
      w_fo, w_out, w_gate, w_up, w_down).reshape(batch, seq_len, d)


def _layer_pair(x_a, mods_a, x_b, mods_b, early, gains, late):
    (ba, sa, d), (bb, sb, _) = x_a.shape, x_b.shape
    assert sb == ROW_BLOCKS_PER_STEP * SEQ_MINOR
    u_a, gf_a, p_a, *late = _mix_in(x_a.reshape(ba * sa, d), mods_a, sa, *early,
                                    cast_weights=late)
    u4_a, w_a = _as_blocks(u_a, ba, sa), _stage1_weights(sa)
    u_b, gf_b, p_b, z_a = _mix_in(
        x_b.reshape(bb * sb, d), mods_b, sb, *early,
        dft_job=lambda n_steps, lin: _Stage1Job(u4_a, w_a, n_steps, lin))
    y_a = _dft_stage2(z_a, _stage2_tables(sa))
    out_a = _mix_out(x_a, y_a, gf_a, p_a, mods_a, *gains, *late)
    y_b = _dft_fused(_as_blocks(u_b, bb, sb), _stage1_weights(sb), _stage2_tables(sb))
    out_b = _mix_out(x_b, y_b, gf_b, p_b, mods_b, *gains, *late)
    return out_a, out_b


def kernel(x_prompt, x_sample, c_prompt, c_sample, w_ada, b_ada, g_pre_mix, w_in, w_fo, w_pg,
           pool_scale, w_po, w_out, g_post_mix, g_pre_ffn, w_gate, w_up, w_down, g_post_ffn):
    depth = w_ada.shape[0]
    y_prompt, y_sample = x_prompt, x_sample
    bp, bs = c_prompt.shape[0], c_sample.shape[0]
    pad_rows = -(bp + bs) % BF16_ROWS
    c_pad = jnp.concatenate(
        [c_prompt, c_sample, jnp.zeros((pad_rows, D_MODEL), F32)], axis=0)
    for l in range(depth):
        ada, w_in_bf16, w_pool = _prep(c_pad, w_ada[l], b_ada[l][None, :], w_in[l], w_pg[l],
                                       pool_scale[l][None, :], w_po[l])
        mods_p, mods_s = (ada, 0), (ada, bp)
        early = (g_pre_mix[l][None, :], w_in_bf16, w_pool)
        gains = (g_post_mix[l][None, :], g_pre_ffn[l][None, :], g_post_ffn[l][None, :])
        late = (w_fo[l], w_out[l], w_gate[l], w_up[l], w_down[l])
        y_prompt, y_sample = _layer_pair(y_prompt, mods_p, y_sample, mods_s, early, gains, late)
    return (y_prompt, y_sample)
```
